```python
import functools
import jax, jax.numpy as jnp
from jax import lax
import numpy as np

D_MODEL = 2048
BATCH = 2
SEQ = 8192
DEPTH = 1
DEC_BATCH = 32
DEC_SEQ = 4
PAST_LEN = 16384
PAGE_SIZE = 128

HEAD_DIM = 128
D_MIX = D_MODEL
H_ATT = D_MIX // (2 * HEAD_DIM)
H_MLSTM = D_MIX // HEAD_DIM - H_ATT
D_FF = 11 * D_MODEL // 4
CONV_W = 3
MOBA_BLOCK = 256
MOBA_TOP_K = 3
Q_BLOCK = 64
MLSTM_CHUNK = 64
F_BIAS_LO = 3.0
F_BIAS_HI = 6.0
EPS = 1e-6
N_IN = 3 * H_ATT * HEAD_DIM + 4 * H_MLSTM * HEAD_DIM + 2 * H_MLSTM

kernel_name = "hymba_moba_mlstm_convffn_step"


def _rmsnorm(x, g):
    xf = x.astype(jnp.float32)
    return (xf * lax.rsqrt(jnp.mean(xf * xf, axis=-1, keepdims=True) + EPS) * g).astype(x.dtype)


def _alibi_slopes(n):
    return 2.0 ** (-8.0 * jnp.arange(1, n + 1, dtype=jnp.float32) / n)


def _split_sizes():
    da = H_ATT * HEAD_DIM
    dm = H_MLSTM * HEAD_DIM
    return [da, da, da, dm, dm, dm, dm, H_MLSTM, H_MLSTM]


def _moba_attend(q, t_pos, k_own, v_own, s_own, slopes, k_sel=None, v_sel=None, s_sel=None, ok_sel=None):
    f32 = jnp.float32
    scale = q.shape[-1] ** -0.5
    d_own = (t_pos[:, None] - s_own[None, :]).astype(f32)
    lo = (jnp.einsum('bqhd,bshd->bqhs', q, k_own, preferred_element_type=f32) * scale
          - slopes[None, :, None] * d_own[:, None, :])
    lo = jnp.where((d_own >= 0)[:, None, :], lo, -jnp.inf)
    if k_sel is None:
        p = jax.nn.softmax(lo, axis=-1)
        return jnp.einsum('bqhs,bshd->bqhd', p, v_own, preferred_element_type=f32)
    d_sel = (t_pos[None, :, None, None, None] - s_sel).astype(f32)
    ls = (jnp.einsum('bqhd,bqhjsd->bqhjs', q, k_sel, preferred_element_type=f32) * scale
          - slopes[None, None, :, None, None] * d_sel)
    ls = jnp.where(ok_sel, ls, -jnp.inf)
    nb, nq, nh, nj, ns = ls.shape
    p = jax.nn.softmax(jnp.concatenate([ls.reshape(nb, nq, nh, nj * ns), lo], axis=-1), axis=-1)
    p_sel = p[..., :nj * ns].reshape(nb, nq, nh, nj, ns)
    p_own = p[..., nj * ns:]
    return (jnp.einsum('bqhjs,bqhjsd->bqhd', p_sel, v_sel, preferred_element_type=f32)
            + jnp.einsum('bqhs,bshd->bqhd', p_own, v_own, preferred_element_type=f32))


def _moba_prompt(q, k, v, slopes):
    B, T, H, d = q.shape
    nb = -(-T // MOBA_BLOCK)
    pad = nb * MOBA_BLOCK - T
    kb = jnp.pad(k, ((0, 0), (0, pad), (0, 0), (0, 0))).reshape(B, nb, MOBA_BLOCK, H, d)
    vb = jnp.pad(v, ((0, 0), (0, pad), (0, 0), (0, 0))).reshape(B, nb, MOBA_BLOCK, H, d)
    k_mean = jnp.mean(kb, axis=2, dtype=jnp.float32)
    n_sel = min(MOBA_TOP_K, nb)
    bi = jnp.arange(B)[:, None, None, None]
    hi = jnp.arange(H)[None, None, :, None]
    blk = jnp.arange(MOBA_BLOCK)

    def one(i):
        q0 = i * Q_BLOCK
        qb = lax.dynamic_slice_in_dim(q, q0, Q_BLOCK, axis=1)
        t_pos = q0 + jnp.arange(Q_BLOCK)
        bq = q0 // MOBA_BLOCK
        score = jnp.einsum('bqhd,bnhd->bqhn', qb, k_mean, preferred_element_type=jnp.float32)
        score = jnp.where(jnp.arange(nb) < bq, score, -jnp.inf)
        _, idx = lax.top_k(score, n_sel)
        k_sel = kb[bi, idx, :, hi]
        v_sel = vb[bi, idx, :, hi]
        s_sel = idx[..., None] * MOBA_BLOCK + blk
        ok = (jnp.arange(n_sel) < bq)[:, None]
        k_own = lax.dynamic_index_in_dim(kb, bq, axis=1, keepdims=False)
        v_own = lax.dynamic_index_in_dim(vb, bq, axis=1, keepdims=False)
        s_own = bq * MOBA_BLOCK + blk
        return _moba_attend(qb, t_pos, k_own, v_own, s_own, slopes, k_sel, v_sel, s_sel, ok)

    out = lax.map(one, jnp.arange(T // Q_BLOCK))
    return jnp.moveaxis(out, 0, 1).reshape(B, T, H, d)


def _moba_sample(q, k, v, cache_k, cache_v, page_table, slopes):
    f32 = jnp.float32
    DB, L, H, d = q.shape
    n_pages = page_table.shape[1]
    past = n_pages * PAGE_SIZE
    nbp = past // MOBA_BLOCK
    ppb = MOBA_BLOCK // PAGE_SIZE
    own_start = nbp * MOBA_BLOCK
    first_own = own_start // PAGE_SIZE
    n_own = n_pages - first_own
    t_pos = past + jnp.arange(L)
    pt_own = page_table[:, first_own:]
    k_own = jnp.concatenate([cache_k[pt_own].reshape(DB, n_own * PAGE_SIZE, H, d).astype(k.dtype), k], axis=1)
    v_own = jnp.concatenate([cache_v[pt_own].reshape(DB, n_own * PAGE_SIZE, H, d).astype(v.dtype), v], axis=1)
    s_own = own_start + jnp.arange(n_own * PAGE_SIZE + L)
    if nbp == 0:
        return _moba_attend(q, t_pos, k_own, v_own, s_own, slopes)
    pt_blk = page_table[:, :nbp * ppb]
    k_mean = jnp.mean(cache_k[pt_blk].reshape(DB, nbp, MOBA_BLOCK, H, d), axis=2, dtype=f32)
    score = jnp.einsum('blhd,bnhd->blhn', q, k_mean, preferred_element_type=f32)
    n_sel = min(MOBA_TOP_K, nbp)
    _, idx = lax.top_k(score, n_sel)
    logical = idx[..., None] * ppb + jnp.arange(ppb)
    phys = page_table[jnp.arange(DB)[:, None, None, None, None], logical]
    hi = jnp.arange(H)[None, None, :, None, None]
    k_sel = cache_k[phys, :, hi].reshape(DB, L, H, n_sel, MOBA_BLOCK, d)
    v_sel = cache_v[phys, :, hi].reshape(DB, L, H, n_sel, MOBA_BLOCK, d)
    s_sel = idx[..., None] * MOBA_BLOCK + jnp.arange(MOBA_BLOCK)
    ok = jnp.ones((n_sel, 1), dtype=bool)
    return _moba_attend(q, t_pos, k_own, v_own, s_own, slopes, k_sel, v_sel, s_sel, ok)


def _mlstm(q, k, v, i_pre, log_f, C0, n0, m0):
    f32 = jnp.float32
    B, T, H, d = q.shape
    L = MLSTM_CHUNK if T % MLSTM_CHUNK == 0 else T
    nc = T // L
    causal = jnp.tril(jnp.ones((L, L), dtype=bool))

    def chunks(a):
        a = a.astype(f32)
        return jnp.moveaxis(a.reshape((B, nc, L) + a.shape[2:]), 1, 0)

    def step(carry, xs):
        C, n, m = carry
        qc, kc, vc, ic, fc = xs
        b = jnp.cumsum(fc, axis=1)
        g = ic - b
        log_d = b[:, :, None, :] + g[:, None, :, :]
        log_d = jnp.where(causal[None, :, :, None], log_d, -jnp.inf)
        m_inter = b + m[:, None, :]
        m_t = jnp.maximum(m_inter, jnp.max(log_d, axis=2))
        s = jnp.einsum('bjhd,bshd->bjsh', qc, kc) * jnp.exp(log_d - m_t[:, :, None, :])
        a_inter = jnp.exp(m_inter - m_t)
        num = (jnp.einsum('bjsh,bshd->bjhd', s, vc)
               + a_inter[..., None] * jnp.einsum('bhvk,bjhk->bjhv', C, qc))
        den = jnp.sum(s, axis=2) + a_inter * jnp.einsum('bhk,bjhk->bjh', n, qc)
        h = num / jnp.maximum(jnp.abs(den), jnp.exp(-m_t))[..., None]
        m_new = m_t[:, -1]
        decay = jnp.exp(b[:, -1] + m - m_new)
        w = jnp.exp(b[:, -1:, :] + g - m_new[:, None, :])
        C_new = decay[..., None, None] * C + jnp.einsum('bsh,bshv,bshk->bhvk', w, vc, kc)
        n_new = decay[..., None] * n + jnp.einsum('bsh,bshk->bhk', w, kc)
        return (C_new, n_new, m_new), h

    carry0 = (C0.astype(f32), n0.astype(f32), m0.astype(f32))
    (C, n, m), hs = lax.scan(step, carry0, (chunks(q), chunks(k), chunks(v), chunks(i_pre), chunks(log_f)))
    return jnp.moveaxis(hs, 0, 1).reshape(B, T, H, d), (C, n, m)


def _conv_ffn(h, buf, w_gate, w_up, conv_w, w_down):
    T = h.shape[1]
    a = h @ w_gate
    u = h @ w_up
    ap = jnp.concatenate([buf.astype(a.dtype), a], axis=1)
    c = sum(conv_w[j] * ap[:, j:j + T] for j in range(CONV_W))
    y = (jax.nn.gelu(c, approximate=False) * u) @ w_down
    return y, ap[:, T:]


def _block(x, attend, C0, n0, m0, conv_buf, g_mix, w_in, b_if, w_out, g_ffn, w_gate, w_up, conv_w, w_down):
    f32 = jnp.float32
    B, T, _ = x.shape
    h = _rmsnorm(x, g_mix)
    z = jnp.einsum('btd,dn->btn', h, w_in)
    cuts = np.cumsum(_split_sizes()[:-1]).tolist()
    q_a, k_a, v_a, q_m, k_m, v_m, o_m, i_m, f_m = jnp.split(z, cuts, axis=-1)
    q_a = q_a.reshape(B, T, H_ATT, HEAD_DIM)
    k_a = k_a.reshape(B, T, H_ATT, HEAD_DIM)
    v_a = v_a.reshape(B, T, H_ATT, HEAD_DIM)
    att = attend(q_a, k_a, v_a)
    i_pre = (i_m + b_if[:H_MLSTM]).astype(f32)
    log_f = jax.nn.log_sigmoid((f_m + b_if[H_MLSTM:]).astype(f32))
    h_m, (C, n, m) = _mlstm(q_m.reshape(B, T, H_MLSTM, HEAD_DIM),
                            k_m.reshape(B, T, H_MLSTM, HEAD_DIM) * HEAD_DIM ** -0.5,
                            v_m.reshape(B, T, H_MLSTM, HEAD_DIM), i_pre, log_f, C0, n0, m0)
    h_m = jax.nn.sigmoid(o_m.astype(f32)) * h_m.reshape(B, T, H_MLSTM * HEAD_DIM)
    mix = jnp.concatenate([att.reshape(B, T, H_ATT * HEAD_DIM), h_m], axis=-1).astype(x.dtype)
    x = x + mix @ w_out
    y, buf = _conv_ffn(_rmsnorm(x, g_ffn), conv_buf, w_gate, w_up, conv_w, w_down)
    x = x + y
    return x, (k_a, v_a, C.astype(x.dtype), n.astype(x.dtype), m.astype(x.dtype), buf)


def setup_inputs(seed: int = 0) -> dict:
    key = jax.random.key(seed)
    ks = jax.random.split(key, 20)
    nrm = jax.random.normal
    n_pages = PAST_LEN // PAGE_SIZE
    n_used = DEC_BATCH * n_pages
    n_pool = n_used + max(1, n_used // 4)
    page_table = jax.random.permutation(ks[8], n_pool)[:n_used].reshape(DEC_BATCH, n_pages).astype(jnp.int32)
    b_if = jnp.concatenate([0.1 * nrm(ks[12], (DEPTH, H_MLSTM)),
                            jnp.linspace(F_BIAS_LO, F_BIAS_HI, H_MLSTM)[None, :] + 0.1 * nrm(ks[13], (DEPTH, H_MLSTM))],
                           axis=-1)
    return {
        'x_prompt': nrm(ks[0], (BATCH, SEQ, D_MODEL)),
        'x_sample': nrm(ks[1], (DEC_BATCH, DEC_SEQ, D_MODEL)),
        'cache_k': nrm(ks[2], (DEPTH, n_pool, PAGE_SIZE, H_ATT, HEAD_DIM)),
        'cache_v': nrm(ks[3], (DEPTH, n_pool, PAGE_SIZE, H_ATT, HEAD_DIM)),
        'state_C': nrm(ks[4], (DEPTH, DEC_BATCH, H_MLSTM, HEAD_DIM, HEAD_DIM)) * HEAD_DIM ** -0.5,
        'state_n': nrm(ks[5], (DEPTH, DEC_BATCH, H_MLSTM, HEAD_DIM)) * HEAD_DIM ** -0.5,
        'state_m': nrm(ks[6], (DEPTH, DEC_BATCH, H_MLSTM)),
        'state_conv': nrm(ks[7], (DEPTH, DEC_BATCH, CONV_W - 1, D_FF)),
        'page_table': page_table,
        'norm_mix': 1.0 + 0.02 * nrm(ks[9], (DEPTH, D_MODEL)),
        'w_in': nrm(ks[10], (DEPTH, D_MODEL, N_IN)) * D_MODEL ** -0.5,
        'b_if': b_if,
        'w_out': nrm(ks[11], (DEPTH, D_MIX, D_MODEL)) * D_MIX ** -0.5,
        'norm_ffn': 1.0 + 0.02 * nrm(ks[14], (DEPTH, D_MODEL)),
        'w_gate': nrm(ks[15], (DEPTH, D_MODEL, D_FF)) * D_MODEL ** -0.5,
        'w_up': nrm(ks[16], (DEPTH, D_MODEL, D_FF)) * D_MODEL ** -0.5,
        'conv_w': nrm(ks[17], (DEPTH, CONV_W, D_FF)) * CONV_W ** -0.5,
        'w_down': nrm(ks[18], (DEPTH, D_FF, D_MODEL)) * D_FF ** -0.5,
        'norm_final': 1.0 + 0.02 * nrm(ks[19], (D_MODEL,)),
    }


def reference(x_prompt, x_sample, cache_k, cache_v, state_C, state_n, state_m, state_conv, page_table,
              norm_mix, w_in, b_if, w_out, norm_ffn, w_gate, w_up, conv_w, w_down, norm_final):
    slopes = _alibi_slopes(H_ATT)
    B = x_prompt.shape[0]
    dt = x_prompt.dtype
    xp, xs = x_prompt, x_sample
    st_p, st_s = [], []
    for l in range(DEPTH):
        w = (norm_mix[l], w_in[l], b_if[l], w_out[l], norm_ffn[l], w_gate[l], w_up[l], conv_w[l], w_down[l])
        C0 = jnp.zeros((B, H_MLSTM, HEAD_DIM, HEAD_DIM), dt)
        n0 = jnp.zeros((B, H_MLSTM, HEAD_DIM), dt)
        m0 = jnp.zeros((B, H_MLSTM), dt)
        buf0 = jnp.zeros((B, CONV_W - 1, D_FF), dt)
        attend_p = functools.partial(_moba_prompt, slopes=slopes)
        xp, sp = _block(xp, attend_p, C0, n0, m0, buf0, *w)
        attend_s = functools.partial(_moba_sample, cache_k=cache_k[l], cache_v=cache_v[l],
                                     page_table=page_table, slopes=slopes)
        xs, ss = _block(xs, attend_s, state_C[l], state_n[l], state_m[l], state_conv[l], *w)
        st_p.append(sp)
        st_s.append(ss)
    k_p, v_p, C_p, n_p, m_p, conv_p = [jnp.stack(z) for z in zip(*st_p)]
    k_s, v_s, C_s, n_s, m_s, conv_s = [jnp.stack(z) for z in zip(*st_s)]
    y_prompt = _rmsnorm(xp, norm_final)
    y_sample = _rmsnorm(xs, norm_final)
    return (y_prompt, y_sample, k_p, v_p, C_p, n_p, m_p, conv_p, k_s, v_s, C_s, n_s, m_s, conv_s)
```

```python
import functools

import jax
import jax.numpy as jnp
from jax import lax
from jax.experimental import pallas as pl
from jax.experimental.pallas import tpu as pltpu

f32, bf16, i32 = jnp.float32, jnp.bfloat16, jnp.int32

HEAD_DIM = 128
H_ATT = 8
H_MLSTM = 8
SEG = H_ATT * HEAD_DIM
N_SEG = 7
CONV_W = 3
MOBA_BLOCK = 256
MOBA_TOP_K = 3
PAGE_SIZE = 128
PAGES_PER_BLOCK = MOBA_BLOCK // PAGE_SIZE
EPS = 1e-6
NEG = -1e30
LANES = 128
MAX_BLOCKS = 32
MIB = 1 << 20


def _params(sem, vmem_mib):
    return pltpu.CompilerParams(dimension_semantics=sem, vmem_limit_bytes=vmem_mib * MIB)


def _iota(shape, dim, dtype=i32):
    return lax.broadcasted_iota(dtype, shape, dim)


def _dot(a, b):
    return jnp.dot(a, b, preferred_element_type=f32)


def _dot_nt(a, b):
    return lax.dot_general(a, b, (((1,), (1,)), ((), ())), preferred_element_type=f32)


def _dot_tn(a, b):
    return lax.dot_general(a, b, (((0,), (0,)), ((), ())), preferred_element_type=f32)


def _rmsnorm(x, g):
    return x * lax.rsqrt(jnp.mean(x * x, axis=-1, keepdims=True) + EPS) * g


def _inproj_body(x_ref, g_ref, w_ref, wif_ref, bif_ref, q_ref, k_ref, v_ref, zm_ref, gate_ref, h_scr):
    n = pl.program_id(1)
    scale = HEAD_DIM ** -0.5

    @pl.when(n == 0)
    def _():
        hb = _rmsnorm(x_ref[...], g_ref[...]).astype(bf16)
        h_scr[...] = hb
        zg = _dot(hb, wif_ref[...]) + bif_ref[...]
        lane = _iota(zg.shape, 1)
        log_sig = jnp.minimum(zg, 0.0) - jnp.log1p(jnp.exp(-jnp.abs(zg)))
        gate_ref[...] = jnp.where(lane < H_MLSTM, zg, jnp.where(lane < 2 * H_MLSTM, log_sig, 0.0))

    z = _dot(h_scr[...], w_ref[...])

    @pl.when(n == 0)
    def _():
        q_ref[...] = (z * scale).astype(bf16)

    @pl.when(n == 1)
    def _():
        k_ref[...] = z

    @pl.when(n == 2)
    def _():
        v_ref[...] = z

    @pl.when(n == 4)
    def _():
        zm_ref[...] = (z * scale).astype(bf16)

    @pl.when((n == 3) | (n >= 5))
    def _():
        zm_ref[...] = z.astype(bf16)


def _inproj(x, g, w_main, w_if, b_if, *, tm):
    M, D = x.shape
    grid = (M // tm, N_SEG)
    return pl.pallas_call(
        _inproj_body,
        grid=grid,
        in_specs=[
            pl.BlockSpec((tm, D), lambda m, n: (m, 0)),
            pl.BlockSpec((1, D), lambda m, n: (0, 0)),
            pl.BlockSpec((D, SEG), lambda m, n: (0, n)),
            pl.BlockSpec((D, LANES), lambda m, n: (0, 0)),
            pl.BlockSpec((1, LANES), lambda m, n: (0, 0)),
        ],
        out_specs=[
            pl.BlockSpec((tm, SEG), lambda m, n: (m, 0)),
            pl.BlockSpec((tm, SEG), lambda m, n: (m, 0)),
            pl.BlockSpec((tm, SEG), lambda m, n: (m, 0)),
            pl.BlockSpec((tm, SEG), lambda m, n: (m, jnp.maximum(n - 3, 0))),
            pl.BlockSpec((tm, LANES), lambda m, n: (m, 0)),
        ],
        out_shape=[
            jax.ShapeDtypeStruct((M, SEG), bf16),
            jax.ShapeDtypeStruct((M, SEG), f32),
            jax.ShapeDtypeStruct((M, SEG), f32),
            jax.ShapeDtypeStruct((M, 4 * SEG), bf16),
            jax.ShapeDtypeStruct((M, LANES), f32),
        ],
        scratch_shapes=[pltpu.VMEM((tm, D), bf16)],
        compiler_params=_params(("arbitrary", "arbitrary"), 48),
        name="inproj",
    )(x, g, w_main, w_if, b_if)


def _top_blocks_mask(scores_t, n_valid, own, n_sel):
    R = scores_t.shape[0]
    rown = _iota(scores_t.shape, 0)
    s = jnp.where(rown < n_valid, scores_t, -jnp.inf)
    mask = jnp.where(rown == own, 0.0, NEG)
    for _ in range(n_sel):
        mx = jnp.max(s, axis=0, keepdims=True)
        idx = jnp.min(jnp.where(s == mx, rown, R), axis=0, keepdims=True)
        pick = (rown == idx) & (mx > -jnp.inf)
        mask = jnp.where(pick, 0.0, mask)
        s = jnp.where(pick, -jnp.inf, s)
    return mask


def _moba_prompt_body(slopes_ref, q_ref, k_ref, v_ref, o_ref, kaug, vt, kmean, qaug, *, nb):
    h = pl.program_id(1)
    qi = pl.program_id(2)
    slope = slopes_ref[h]
    B = MOBA_BLOCK

    @pl.when(qi == 0)
    def _init():
        if nb < MAX_BLOCKS:
            kmean[...] = jnp.zeros(kmean.shape, f32)
        lane = _iota((B, LANES), 1)
        s_lo = _iota((B, LANES), 0).astype(f32)
        for n in range(nb):
            kb = k_ref[0, n * B:(n + 1) * B, :]
            vb = v_ref[0, n * B:(n + 1) * B, :]
            kaug[n, :, 0:HEAD_DIM] = kb.astype(bf16)
            ext = jnp.where(lane == n, 1.0, 0.0)
            ext = jnp.where(lane == MAX_BLOCKS, -slope, ext)
            ext = jnp.where(lane == MAX_BLOCKS + 1, slope * s_lo, ext)
            ext = jnp.where(lane == MAX_BLOCKS + 2, -slope * B, ext)
            ext = jnp.where(lane == MAX_BLOCKS + 3, slope * (B * n), ext)
            kaug[n, :, HEAD_DIM:2 * HEAD_DIM] = ext.astype(bf16)
            vt[n] = vb.T.astype(bf16)
            kmean[n:n + 1, :] = jnp.sum(kb, axis=0, keepdims=True) * (1.0 / B)

    q_t = q_ref[0].astype(f32).T.astype(bf16)
    scores_t = _dot(kmean[...].astype(bf16), q_t)
    mask_t = _top_blocks_mask(scores_t, qi, qi, min(MOBA_TOP_K, nb))
    qaug[0:HEAD_DIM, :] = q_t
    qaug[HEAD_DIM:HEAD_DIM + MAX_BLOCKS, :] = mask_t.astype(bf16)
    n_ext = HEAD_DIM - MAX_BLOCKS
    rr = _iota((n_ext, B), 0)
    t_lo = _iota((n_ext, B), 1).astype(f32)
    ext = jnp.where(rr == 0, t_lo, jnp.where((rr == 1) | (rr == 3), 1.0, 0.0))
    ext = jnp.where(rr == 2, qi.astype(f32), ext)
    qaug[HEAD_DIM + MAX_BLOCKS:, :] = ext.astype(bf16)

    qa = qaug[...]
    s = _dot(kaug[qi], qa)
    s = jnp.where(_iota(s.shape, 0) <= _iota(s.shape, 1), s, NEG)
    m = jnp.max(s, axis=0, keepdims=True)
    p = jnp.exp(s - m)
    l = jnp.sum(p, axis=0, keepdims=True)
    acc = _dot(vt[qi], p.astype(bf16))

    def body(n, carry):
        m, l, acc = carry
        s = _dot(kaug[n], qa)
        m_new = jnp.maximum(m, jnp.max(s, axis=0, keepdims=True))
        alpha = jnp.exp(m - m_new)
        p = jnp.exp(s - m_new)
        l = alpha * l + jnp.sum(p, axis=0, keepdims=True)
        acc = alpha * acc + _dot(vt[n], p.astype(bf16))
        return m_new, l, acc

    m, l, acc = lax.fori_loop(0, qi, body, (m, l, acc))
    o_ref[0] = (acc / l).T.astype(bf16)


def _moba_prompt(slopes, q, k, v):
    Bsz, T, _ = q.shape
    assert T % MOBA_BLOCK == 0
    nb = T // MOBA_BLOCK
    assert nb <= MAX_BLOCKS
    blk = MOBA_BLOCK
    return pl.pallas_call(
        functools.partial(_moba_prompt_body, nb=nb),
        grid=(Bsz, H_ATT, nb),
        in_specs=[
            pl.BlockSpec(memory_space=pltpu.SMEM),
            pl.BlockSpec((1, blk, HEAD_DIM), lambda b, h, i: (b, i, h)),
            pl.BlockSpec((1, T, HEAD_DIM), lambda b, h, i: (b, 0, h)),
            pl.BlockSpec((1, T, HEAD_DIM), lambda b, h, i: (b, 0, h)),
        ],
        out_specs=pl.BlockSpec((1, blk, HEAD_DIM), lambda b, h, i: (b, i, h)),
        out_shape=jax.ShapeDtypeStruct((Bsz, T, H_ATT * HEAD_DIM), bf16),
        scratch_shapes=[
            pltpu.VMEM((nb, blk, 2 * HEAD_DIM), bf16),
            pltpu.VMEM((nb, HEAD_DIM, blk), bf16),
            pltpu.VMEM((MAX_BLOCKS, HEAD_DIM), f32),
            pltpu.VMEM((2 * HEAD_DIM, blk), bf16),
        ],
        compiler_params=_params(("arbitrary", "arbitrary", "arbitrary"), 48),
        name="moba_prompt",
    )(slopes, q, k, v)


KMEAN_PAGES = 16


def _kmean_body(pt_ref, *refs):
    pages, out_ref = refs[:KMEAN_PAGES], refs[KMEAN_PAGES]
    for b in range(KMEAN_PAGES // PAGES_PER_BLOCK):
        s = jnp.sum(pages[PAGES_PER_BLOCK * b][0], axis=0, keepdims=True)
        for p in range(1, PAGES_PER_BLOCK):
            s = s + jnp.sum(pages[PAGES_PER_BLOCK * b + p][0], axis=0, keepdims=True)
        out_ref[0, b:b + 1, :] = s * (1.0 / MOBA_BLOCK)


def _kmean_sample(page_table, cache_k):
    DB, n_pages = page_table.shape
    W = cache_k.shape[-1]
    assert n_pages % KMEAN_PAGES == 0
    nbp = n_pages // PAGES_PER_BLOCK
    bps = KMEAN_PAGES // PAGES_PER_BLOCK
    in_specs = [
        pl.BlockSpec((1, PAGE_SIZE, W), functools.partial(lambda d, j, pt, i: (pt[d, j * KMEAN_PAGES + i], 0, 0), i=i))
        for i in range(KMEAN_PAGES)
    ]
    return pl.pallas_call(
        _kmean_body,
        grid_spec=pltpu.PrefetchScalarGridSpec(
            num_scalar_prefetch=1,
            grid=(DB, n_pages // KMEAN_PAGES),
            in_specs=in_specs,
            out_specs=pl.BlockSpec((1, bps, W), lambda d, j, pt: (d, j, 0)),
        ),
        out_shape=jax.ShapeDtypeStruct((DB, nbp, W), f32),
        compiler_params=_params(("arbitrary", "arbitrary"), 40),
        name="kmean_sample",
    )(page_table, *([cache_k] * KMEAN_PAGES))


def _select_body(kmean_ref, q_ref, idx_ref, *, n_sel):
    nbp = kmean_ref.shape[1]
    L8 = q_ref.shape[1]
    idx_ref[...] = jnp.zeros(idx_ref.shape, i32)
    for h in range(H_ATT):
        km = kmean_ref[0, :, h * HEAD_DIM:(h + 1) * HEAD_DIM].astype(bf16)
        qh = q_ref[0, :, h * HEAD_DIM:(h + 1) * HEAD_DIM].astype(f32)
        qh = jnp.concatenate([qh, jnp.zeros((LANES - L8, HEAD_DIM), f32)], axis=0).astype(bf16)
        s = _dot_nt(km, qh)
        rown = _iota(s.shape, 0)
        for r in range(n_sel):
            mx = jnp.max(s, axis=0, keepdims=True)
            idx = jnp.min(jnp.where(s == mx, rown, nbp), axis=0, keepdims=True)
            idx_ref[0, h, r:r + 1, :] = idx
            s = jnp.where(rown == idx, -jnp.inf, s)


def _select_sample(kmean, q8, n_sel):
    DB, nbp, W = kmean.shape
    return pl.pallas_call(
        functools.partial(_select_body, n_sel=n_sel),
        grid=(DB,),
        in_specs=[
            pl.BlockSpec((1, nbp, W), lambda d: (d, 0, 0)),
            pl.BlockSpec((1, q8.shape[1], W), lambda d: (d, 0, 0)),
        ],
        out_specs=pl.BlockSpec((1, H_ATT, 8, LANES), lambda d: (d, 0, 0, 0)),
        out_shape=jax.ShapeDtypeStruct((DB, H_ATT, 8, LANES), i32),
        compiler_params=_params(("arbitrary",), 32),
        name="select_sample",
    )(kmean, q8)


def _sattn_body(idx_ref, pt_ref, slopes_ref, q_ref, kn_ref, vn_ref, *rest, L, n_sel, past):
    nblk = L * n_sel * PAGES_PER_BLOCK
    kp, vp, o_ref = rest[:nblk], rest[nblk:2 * nblk], rest[2 * nblk]
    d = pl.program_id(0)
    h = pl.program_id(1)
    slope = slopes_ref[h]
    qf = q_ref[0].astype(f32)
    kn = kn_ref[0]
    vn = vn_ref[0]
    lpos = _iota((L, 1), 0)
    ppos = _iota((PAGE_SIZE, 1), 0)
    rows = []
    for l in range(L):
        ql = qf[l:l + 1, :]
        lo = jnp.sum(kn * ql, axis=1, keepdims=True)
        lo = jnp.where(lpos <= l, lo - slope * (l - lpos).astype(f32), NEG)
        logits = []
        for j in range(n_sel):
            blk = idx_ref[((d * L + l) * H_ATT + h) * n_sel + j]
            for p in range(PAGES_PER_BLOCK):
                kb = kp[(l * n_sel + j) * PAGES_PER_BLOCK + p][0]
                dist = (past + l - blk * MOBA_BLOCK - p * PAGE_SIZE) - ppos
                logits.append(jnp.sum(kb * ql, axis=1, keepdims=True) - slope * dist.astype(f32))
        m = jnp.max(lo, axis=0, keepdims=True)
        for lg in logits:
            m = jnp.maximum(m, jnp.max(lg, axis=0, keepdims=True))
        p_own = jnp.exp(lo - m)
        den = jnp.sum(p_own, axis=0, keepdims=True)
        num = jnp.sum(p_own * vn, axis=0, keepdims=True)
        for i, lg in enumerate(logits):
            pi = jnp.exp(lg - m)
            den = den + jnp.sum(pi, axis=0, keepdims=True)
            num = num + jnp.sum(pi * vp[l * n_sel * PAGES_PER_BLOCK + i][0], axis=0, keepdims=True)
        rows.append(num / den)
    rows.append(jnp.zeros((8 - L, HEAD_DIM), f32))
    o_ref[0] = jnp.concatenate(rows, axis=0).astype(bf16)


def _sattn_sample(idx_flat, page_table, slopes, q8, k_new, v_new, cache_k, cache_v, *, L, n_sel, past):
    DB = q8.shape[0]
    assert L <= 8
    nblk = L * n_sel * PAGES_PER_BLOCK

    def page_map(d, h, idx, pt, lj, p):
        return (pt[d, idx[((d * L + lj // n_sel) * H_ATT + h) * n_sel + lj % n_sel] * PAGES_PER_BLOCK + p], 0, h)

    page_specs = [
        pl.BlockSpec((1, PAGE_SIZE, HEAD_DIM), functools.partial(page_map, lj=i // PAGES_PER_BLOCK, p=i % PAGES_PER_BLOCK))
        for i in range(nblk)
    ]
    return pl.pallas_call(
        functools.partial(_sattn_body, L=L, n_sel=n_sel, past=past),
        grid_spec=pltpu.PrefetchScalarGridSpec(
            num_scalar_prefetch=2,
            grid=(DB, H_ATT),
            in_specs=[
                pl.BlockSpec(memory_space=pltpu.SMEM),
                pl.BlockSpec((1, 8, HEAD_DIM), lambda d, h, idx, pt: (d, 0, h)),
                pl.BlockSpec((1, L, HEAD_DIM), lambda d, h, idx, pt: (d, 0, h)),
                pl.BlockSpec((1, L, HEAD_DIM), lambda d, h, idx, pt: (d, 0, h)),
            ] + page_specs + page_specs,
            out_specs=pl.BlockSpec((1, 8, HEAD_DIM), lambda d, h, idx, pt: (d, 0, h)),
        ),
        out_shape=jax.ShapeDtypeStruct((DB, 8, H_ATT * HEAD_DIM), bf16),
        compiler_params=_params(("arbitrary", "arbitrary"), 32),
        name="sattn_sample",
    )(idx_flat, page_table, slopes, q8, k_new, v_new, *([cache_k] * nblk), *([cache_v] * nblk))


def _split3_dot(a_bf16, x):
    x1 = x.astype(bf16)
    r1 = x - x1.astype(f32)
    x2 = r1.astype(bf16)
    x3 = (r1 - x2.astype(f32)).astype(bf16)
    return _dot(a_bf16, x1) + _dot(a_bf16, x2) + _dot(a_bf16, x3)


def _mlstm_body(zm_ref, gate_ref, c0_ref, n0_ref, m0_ref, hm_ref, c_out, n_out, m_out, c_scr, n_scr, m_scr, *, L, l_out):
    c = pl.program_id(1)
    nc = pl.num_programs(1)

    @pl.when(c == 0)
    def _():
        c_scr[...] = c0_ref[0]
        n_scr[...] = n0_ref[0]
        m_scr[...] = m0_ref[0]

    G = gate_ref[0]
    row = _iota((L, L), 0)
    col = _iota((L, L), 1)
    causal = row >= col
    csum = _split3_dot(jnp.where(causal, 1.0, 0.0).astype(bf16), G)
    lane = _iota((L, LANES), 1)
    X = jnp.where(lane < H_MLSTM, pltpu.roll(csum, LANES - H_MLSTM, 1), pltpu.roll(G, H_MLSTM, 1) - csum)
    XT = X.T
    for h in range(H_MLSTM):
        sl = slice(h * HEAD_DIM, (h + 1) * HEAD_DIM)
        q = zm_ref[0, :, sl]
        k = zm_ref[0, :, SEG + h * HEAD_DIM:SEG + (h + 1) * HEAD_DIM]
        v = zm_ref[0, :, 2 * SEG + h * HEAD_DIM:2 * SEG + (h + 1) * HEAD_DIM]
        o = zm_ref[0, :, 3 * SEG + h * HEAD_DIM:3 * SEG + (h + 1) * HEAD_DIM]
        b_col = X[:, h:h + 1]
        g_col = X[:, H_MLSTM + h:H_MLSTM + h + 1]
        g_row = XT[H_MLSTM + h:H_MLSTM + h + 1, :]
        m_prev = m_scr[h:h + 1, 0:1]
        log_d = jnp.where(causal, b_col + g_row, -jnp.inf)
        m_inter = b_col + m_prev
        m_t = jnp.maximum(m_inter, jnp.max(log_d, axis=1, keepdims=True))
        s = _dot_nt(q, k) * jnp.exp(log_d - m_t)
        a_inter = jnp.exp(m_inter - m_t)
        c_h = c_scr[h]
        n_h = n_scr[h:h + 1, :]
        num = _dot(s.astype(bf16), v) + a_inter * _dot_nt(q, c_h.astype(bf16))
        den = jnp.sum(s, axis=1, keepdims=True) + a_inter * jnp.sum(q.astype(f32) * n_h, axis=1, keepdims=True)
        hh = num / jnp.maximum(jnp.abs(den), jnp.exp(-m_t))
        out = (jax.nn.sigmoid(o.astype(f32)) * hh).astype(bf16)
        hm_ref[0, :, sl] = out[:l_out]
        m_new = m_t[L - 1:L, :]
        b_last = b_col[L - 1:L, :]
        decay = jnp.exp(b_last + m_prev - m_new)
        w = jnp.exp(b_last + g_col - m_new)
        wk = w * k.astype(f32)
        c_scr[h] = decay * c_h + _dot_tn(v, wk.astype(bf16))
        n_scr[h:h + 1, :] = decay * n_h + jnp.sum(wk, axis=0, keepdims=True)
        m_scr[h:h + 1, :] = jnp.broadcast_to(m_new, (1, LANES))

    @pl.when(c == nc - 1)
    def _():
        c_out[0] = c_scr[...]
        n_out[0] = n_scr[...]
        m_out[0] = m_scr[...]


def _mlstm(zm, gates, c0, n0, m0, *, L, l_out):
    Bsz, T, _ = zm.shape
    nc = T // L
    return pl.pallas_call(
        functools.partial(_mlstm_body, L=L, l_out=l_out),
        grid=(Bsz, nc),
        in_specs=[
            pl.BlockSpec((1, L, 4 * SEG), lambda b, c: (b, c, 0)),
            pl.BlockSpec((1, L, LANES), lambda b, c: (b, c, 0)),
            pl.BlockSpec((1, H_MLSTM, HEAD_DIM, HEAD_DIM), lambda b, c: (b, 0, 0, 0)),
            pl.BlockSpec((1, H_MLSTM, HEAD_DIM), lambda b, c: (b, 0, 0)),
            pl.BlockSpec((1, H_MLSTM, LANES), lambda b, c: (b, 0, 0)),
        ],
        out_specs=[
            pl.BlockSpec((1, l_out, SEG), lambda b, c: (b, c, 0)),
            pl.BlockSpec((1, H_MLSTM, HEAD_DIM, HEAD_DIM), lambda b, c: (b, 0, 0, 0)),
            pl.BlockSpec((1, H_MLSTM, HEAD_DIM), lambda b, c: (b, 0, 0)),
            pl.BlockSpec((1, H_MLSTM, LANES), lambda b, c: (b, 0, 0)),
        ],
        out_shape=[
            jax.ShapeDtypeStruct((Bsz, nc * l_out, SEG), bf16),
            jax.ShapeDtypeStruct((Bsz, H_MLSTM, HEAD_DIM, HEAD_DIM), f32),
            jax.ShapeDtypeStruct((Bsz, H_MLSTM, HEAD_DIM), f32),
            jax.ShapeDtypeStruct((Bsz, H_MLSTM, LANES), f32),
        ],
        scratch_shapes=[
            pltpu.VMEM((H_MLSTM, HEAD_DIM, HEAD_DIM), f32),
            pltpu.VMEM((H_MLSTM, HEAD_DIM), f32),
            pltpu.VMEM((H_MLSTM, LANES), f32),
        ],
        compiler_params=_params(("arbitrary", "arbitrary"), 32),
        name="mlstm",
    )(zm, gates, c0, n0, m0)


def _outproj_body(x_ref, att_ref, hm_ref, wa_ref, wm_ref, o_ref):
    o_ref[...] = x_ref[...] + _dot(att_ref[...], wa_ref[...]) + _dot(hm_ref[...], wm_ref[...])


def _outproj(x, att, hm, w_out, *, tm, tn):
    M, D = x.shape
    return pl.pallas_call(
        _outproj_body,
        grid=(M // tm, D // tn),
        in_specs=[
            pl.BlockSpec((tm, tn), lambda m, n: (m, n)),
            pl.BlockSpec((tm, SEG), lambda m, n: (m, 0)),
            pl.BlockSpec((tm, SEG), lambda m, n: (m, 0)),
            pl.BlockSpec((SEG, tn), lambda m, n: (0, n)),
            pl.BlockSpec((SEG, tn), lambda m, n: (1, n)),
        ],
        out_specs=pl.BlockSpec((tm, tn), lambda m, n: (m, n)),
        out_shape=jax.ShapeDtypeStruct((M, D), f32),
        compiler_params=_params(("arbitrary", "arbitrary"), 40),
        name="outproj",
    )(x, att, hm, w_out, w_out)


def _ffn_body(*refs, tm, seq_len, carry_mode):
    if carry_mode:
        x_ref, g_ref, gf_ref, wg_ref, wu_ref, cw_ref, wd_ref, y_ref, tail_ref, h_scr, acc_scr, carry = refs
    else:
        x_ref, g_ref, gf_ref, wg_ref, wu_ref, cw_ref, wd_ref, p1_ref, p2_ref, y_ref, a_ref, h_scr, acc_scr = refs
    m = pl.program_id(0)
    f = pl.program_id(1)
    nf = pl.num_programs(1)

    @pl.when(f == 0)
    def _():
        h_scr[...] = _rmsnorm(x_ref[...], g_ref[...]).astype(bf16)
        acc_scr[...] = jnp.zeros(acc_scr.shape, f32)

    hb = h_scr[...]
    a = _dot(hb, wg_ref[...])
    u = _dot(hb, wu_ref[...])
    row = _iota(a.shape, 0)
    a1 = pltpu.roll(a, 1, 0)
    a2 = pltpu.roll(a, 2, 0)
    if carry_mode:
        prev = carry[f]
        prev = jnp.where(m % (seq_len // tm) == 0, 0.0, prev)
        a1 = jnp.where(row == 0, prev[7:8, :], a1)
        a2 = jnp.where(row == 0, prev[6:7, :], jnp.where(row == 1, prev[7:8, :], a2))
        carry[f] = a[tm - 8:, :]
        tail_ref[0] = a[tm - 8:, :]
    else:
        lpos = row % seq_len
        a1 = jnp.where(lpos >= 1, a1, p1_ref[...])
        a2 = jnp.where(lpos >= 2, a2, p2_ref[...])
        a_ref[...] = a
    cw = cw_ref[...]
    cv = cw[0:1, :] * a2 + cw[1:2, :] * a1 + cw[2:3, :] * a
    gl = 0.5 * cv * (1.0 + lax.erf(cv * (0.5 ** 0.5)))
    acc_scr[...] += _dot((gl * u).astype(bf16), wd_ref[...])

    @pl.when(f == nf - 1)
    def _():
        y_ref[...] = _rmsnorm(x_ref[...] + acc_scr[...], gf_ref[...])


def _ffn(x1, g_ffn, g_final, w_gate, w_up, conv_w, w_down, *, tm, tf, seq_len, prev=None):
    M, D = x1.shape
    DF = w_gate.shape[1]
    nf = DF // tf
    carry_mode = prev is None
    in_specs = [
        pl.BlockSpec((tm, D), lambda m, f: (m, 0)),
        pl.BlockSpec((1, D), lambda m, f: (0, 0)),
        pl.BlockSpec((1, D), lambda m, f: (0, 0)),
        pl.BlockSpec((D, tf), lambda m, f: (0, f)),
        pl.BlockSpec((D, tf), lambda m, f: (0, f)),
        pl.BlockSpec((CONV_W, tf), lambda m, f: (0, f)),
        pl.BlockSpec((tf, D), lambda m, f: (f, 0)),
    ]
    args = [x1, g_ffn, g_final, w_gate, w_up, conv_w, w_down]
    scratch = [pltpu.VMEM((tm, D), bf16), pltpu.VMEM((tm, D), f32)]
    if carry_mode:
        assert seq_len % tm == 0
        out_specs = [pl.BlockSpec((tm, D), lambda m, f: (m, 0)), pl.BlockSpec((1, 8, tf), lambda m, f: (m, 0, f))]
        out_shape = [jax.ShapeDtypeStruct((M, D), f32), jax.ShapeDtypeStruct((M // tm, 8, DF), f32)]
        scratch.append(pltpu.VMEM((nf, 8, tf), f32))
    else:
        assert tm % seq_len == 0
        in_specs += [pl.BlockSpec((tm, tf), lambda m, f: (m, f)), pl.BlockSpec((tm, tf), lambda m, f: (m, f))]
        args += list(prev)
        out_specs = [pl.BlockSpec((tm, D), lambda m, f: (m, 0)), pl.BlockSpec((tm, tf), lambda m, f: (m, f))]
        out_shape = [jax.ShapeDtypeStruct((M, D), f32), jax.ShapeDtypeStruct((M, DF), f32)]
    return pl.pallas_call(
        functools.partial(_ffn_body, tm=tm, seq_len=seq_len, carry_mode=carry_mode),
        grid=(M // tm, nf),
        in_specs=in_specs,
        out_specs=out_specs,
        out_shape=out_shape,
        scratch_shapes=scratch,
        compiler_params=_params(("arbitrary", "arbitrary"), 52),
        name="convffn",
    )(*args)


def kernel(x_prompt, x_sample, cache_k, cache_v, state_C, state_n, state_m, state_conv, page_table,
           norm_mix, w_in, b_if, w_out, norm_ffn, w_gate, w_up, conv_w, w_down, norm_final):
    Bsz, T, D = x_prompt.shape
    DB, Ls, _ = x_sample.shape
    depth = w_in.shape[0]
    n_pages = page_table.shape[1]
    past = n_pages * PAGE_SIZE
    assert past % MOBA_BLOCK == 0 and past // MOBA_BLOCK >= 1, "sample path assumes a block-aligned, non-empty past"
    nbp = past // MOBA_BLOCK
    n_sel_s = min(MOBA_TOP_K, nbp)
    slopes = 2.0 ** (-8.0 * jnp.arange(1, H_ATT + 1, dtype=f32) / H_ATT)
    n_main = N_SEG * SEG
    TM_P, TM_S = 512, DB * Ls
    L_P, L_S = 256, 128

    xp = x_prompt.reshape(Bsz * T, D)
    xs = x_sample.reshape(DB * Ls, D)
    st_p, st_s = [], []
    for l in range(depth):
        w_main = w_in[l, :, :n_main].astype(bf16)
        w_if = jnp.pad(w_in[l, :, n_main:], ((0, 0), (0, LANES - 2 * H_MLSTM))).astype(bf16)
        bif = jnp.pad(b_if[l], (0, LANES - 2 * H_MLSTM)).reshape(1, LANES)
        wo = w_out[l].astype(bf16)
        wg, wu, wd = w_gate[l].astype(bf16), w_up[l].astype(bf16), w_down[l].astype(bf16)
        g_mix, g_ffn = norm_mix[l].reshape(1, D), norm_ffn[l].reshape(1, D)
        g_fin = norm_final.reshape(1, D) if l == depth - 1 else jnp.ones((1, D), f32)
        assert depth == 1, "final norm is fused into the last layer's FFN call"

        q, k, v, zm, gates = _inproj(xp, g_mix, w_main, w_if, bif, tm=TM_P)
        att = _moba_prompt(slopes, q.reshape(Bsz, T, SEG), k.reshape(Bsz, T, SEG), v.reshape(Bsz, T, SEG))
        hm, c_p, n_p, m_p = _mlstm(
            zm.reshape(Bsz, T, 4 * SEG), gates.reshape(Bsz, T, LANES),
            jnp.zeros((Bsz, H_MLSTM, HEAD_DIM, HEAD_DIM), f32), jnp.zeros((Bsz, H_MLSTM, HEAD_DIM), f32),
            jnp.zeros((Bsz, H_MLSTM, LANES), f32), L=L_P, l_out=L_P)
        x1 = _outproj(xp, att.reshape(Bsz * T, SEG), hm.reshape(Bsz * T, SEG), wo, tm=TM_P, tn=1024)
        xp, tail = _ffn(x1, g_ffn, g_fin, wg, wu, conv_w[l], wd, tm=TM_P, tf=512, seq_len=T)
        conv_p = tail.reshape(Bsz, T // TM_P, 8, -1)[:, -1, 8 - (CONV_W - 1):, :]
        st_p.append((k.reshape(Bsz, T, H_ATT, HEAD_DIM), v.reshape(Bsz, T, H_ATT, HEAD_DIM), c_p, n_p, m_p[:, :, 0], conv_p))

        q, k, v, zm, gates = _inproj(xs, g_mix, w_main, w_if, bif, tm=TM_S)
        ck = cache_k[l].reshape(-1, PAGE_SIZE, SEG)
        cv = cache_v[l].reshape(-1, PAGE_SIZE, SEG)
        kmean = _kmean_sample(page_table, ck)
        q8 = jnp.pad(q.reshape(DB, Ls, SEG), ((0, 0), (0, 8 - Ls), (0, 0)))
        idx = _select_sample(kmean, q8, n_sel_s)
        idx_flat = jnp.transpose(idx[:, :, :n_sel_s, :Ls], (0, 3, 1, 2)).reshape(-1)
        att8 = _sattn_sample(idx_flat, page_table, slopes, q8, k.reshape(DB, Ls, SEG), v.reshape(DB, Ls, SEG),
                             ck, cv, L=Ls, n_sel=n_sel_s, past=past)
        att = att8[:, :Ls].reshape(DB * Ls, SEG)
        pad_t = ((0, 0), (0, L_S - Ls), (0, 0))
        gate_pad = jnp.where(jnp.arange(LANES) < H_MLSTM, NEG, 0.0).astype(f32)
        gates_s = jnp.concatenate(
            [gates.reshape(DB, Ls, LANES), jnp.broadcast_to(gate_pad, (DB, L_S - Ls, LANES))], axis=1)
        hm8, c_s, n_s, m_s = _mlstm(
            jnp.pad(zm.reshape(DB, Ls, 4 * SEG), pad_t), gates_s, state_C[l], state_n[l],
            jnp.broadcast_to(state_m[l][:, :, None], (DB, H_MLSTM, LANES)), L=L_S, l_out=8)
        hm = hm8[:, :Ls].reshape(DB * Ls, SEG)
        x1 = _outproj(xs, att, hm, wo, tm=TM_S, tn=1024)
        buf = state_conv[l]
        zero = jnp.zeros_like(buf[:, 0])
        p1 = jnp.stack([buf[:, 1]] + [zero] * (Ls - 1), axis=1).reshape(DB * Ls, -1)
        p2 = jnp.stack([buf[:, 0], buf[:, 1]] + [zero] * (Ls - 2), axis=1).reshape(DB * Ls, -1)
        xs, a_full = _ffn(x1, g_ffn, g_fin, wg, wu, conv_w[l], wd, tm=TM_S, tf=512, seq_len=Ls, prev=(p1, p2))
        conv_s = a_full.reshape(DB, Ls, -1)[:, Ls - (CONV_W - 1):]
        st_s.append((k.reshape(DB, Ls, H_ATT, HEAD_DIM), v.reshape(DB, Ls, H_ATT, HEAD_DIM), c_s, n_s, m_s[:, :, 0], conv_s))

    k_p, v_p, C_p, n_p, m_p, conv_p = [jnp.stack(z) for z in zip(*st_p)]
    k_s, v_s, C_s, n_s, m_s, conv_s = [jnp.stack(z) for z in zip(*st_s)]
    return (xp.reshape(Bsz, T, D), xs.reshape(DB, Ls, D), k_p, v_p, C_p, n_p, m_p, conv_p,
            k_s, v_s, C_s, n_s, m_s, conv_s)
```

```python
import functools

import jax
import jax.numpy as jnp
from jax import lax
from jax.experimental import pallas as pl
from jax.experimental.pallas import tpu as pltpu

f32, bf16, i32 = jnp.float32, jnp.bfloat16, jnp.int32

HEAD_DIM = 128
H_ATT = 8
H_MLSTM = 8
SEG = H_ATT * HEAD_DIM
N_SEG = 7
CONV_W = 3
MOBA_BLOCK = 256
MOBA_TOP_K = 3
PAGE_SIZE = 128
PAGES_PER_BLOCK = MOBA_BLOCK // PAGE_SIZE
EPS = 1e-6
NEG = -1e30
LANES = 128
MAX_BLOCKS = 32
MIB = 1 << 20


def _params(sem, vmem_mib):
    return pltpu.CompilerParams(dimension_semantics=sem, vmem_limit_bytes=vmem_mib * MIB)


def _iota(shape, dim, dtype=i32):
    return lax.broadcasted_iota(dtype, shape, dim)


def _dot(a, b):
    return jnp.dot(a, b, preferred_element_type=f32)


def _dot_nt(a, b):
    return lax.dot_general(a, b, (((1,), (1,)), ((), ())), preferred_element_type=f32)


def _dot_tn(a, b):
    return lax.dot_general(a, b, (((0,), (0,)), ((), ())), preferred_element_type=f32)


def _rmsnorm(x, g):
    return x * lax.rsqrt(jnp.mean(x * x, axis=-1, keepdims=True) + EPS) * g


def _inproj_body(x_ref, g_ref, w_ref, wif_ref, bif_ref, q_ref, k_ref, v_ref, zm_ref, gate_ref, *rest, attn_aux):
    if attn_aux:
        kb_ref, vb_ref, kmean_ref, h_scr = rest
    else:
        (h_scr,) = rest
    n = pl.program_id(1)
    scale = HEAD_DIM ** -0.5

    @pl.when(n == 0)
    def _():
        hb = _rmsnorm(x_ref[...], g_ref[...]).astype(bf16)
        h_scr[...] = hb
        zg = _dot(hb, wif_ref[...]) + bif_ref[...]
        lane = _iota(zg.shape, 1)
        log_sig = jnp.minimum(zg, 0.0) - jnp.log1p(jnp.exp(-jnp.abs(zg)))
        gate_ref[...] = jnp.where(lane < H_MLSTM, zg, jnp.where(lane < 2 * H_MLSTM, log_sig, 0.0))

    z = _dot(h_scr[...], w_ref[...])

    @pl.when(n == 0)
    def _():
        q_ref[...] = (z * scale).astype(bf16)

    @pl.when(n == 1)
    def _():
        k_ref[...] = z
        if attn_aux:
            kb_ref[...] = z.astype(bf16)
            for i in range(z.shape[0] // MOBA_BLOCK):
                blk = z[i * MOBA_BLOCK:(i + 1) * MOBA_BLOCK, :]
                kmean_ref[0, i:i + 1, :] = jnp.sum(blk, axis=0, keepdims=True) * (1.0 / MOBA_BLOCK)

    @pl.when(n == 2)
    def _():
        v_ref[...] = z
        if attn_aux:
            vb_ref[...] = z.astype(bf16)

    @pl.when(n == 4)
    def _():
        zm_ref[...] = (z * scale).astype(bf16)

    @pl.when((n == 3) | (n >= 5))
    def _():
        zm_ref[...] = z.astype(bf16)


def _inproj(x, g, w_main, w_if, b_if, *, tm, attn_aux):
    M, D = x.shape
    grid = (M // tm, N_SEG)

    def row(m, n):
        return (m, 0)

    out_specs = [
        pl.BlockSpec((tm, SEG), row),
        pl.BlockSpec((tm, SEG), row),
        pl.BlockSpec((tm, SEG), row),
        pl.BlockSpec((tm, SEG), lambda m, n: (m, jnp.maximum(n - 3, 0))),
        pl.BlockSpec((tm, LANES), row),
    ]
    out_shape = [
        jax.ShapeDtypeStruct((M, SEG), bf16),
        jax.ShapeDtypeStruct((M, SEG), f32),
        jax.ShapeDtypeStruct((M, SEG), f32),
        jax.ShapeDtypeStruct((M, 4 * SEG), bf16),
        jax.ShapeDtypeStruct((M, LANES), f32),
    ]
    if attn_aux:
        assert tm % MOBA_BLOCK == 0
        bpt = tm // MOBA_BLOCK
        out_specs += [pl.BlockSpec((tm, SEG), row), pl.BlockSpec((tm, SEG), row),
                      pl.BlockSpec((1, bpt, SEG), lambda m, n: (m, 0, 0))]
        out_shape += [jax.ShapeDtypeStruct((M, SEG), bf16), jax.ShapeDtypeStruct((M, SEG), bf16),
                      jax.ShapeDtypeStruct((M // tm, bpt, SEG), f32)]
    return pl.pallas_call(
        functools.partial(_inproj_body, attn_aux=attn_aux),
        grid=grid,
        in_specs=[
            pl.BlockSpec((tm, D), row),
            pl.BlockSpec((1, D), lambda m, n: (0, 0)),
            pl.BlockSpec((D, SEG), lambda m, n: (0, n)),
            pl.BlockSpec((D, LANES), lambda m, n: (0, 0)),
            pl.BlockSpec((1, LANES), lambda m, n: (0, 0)),
        ],
        out_specs=out_specs,
        out_shape=out_shape,
        scratch_shapes=[pltpu.VMEM((tm, D), bf16)],
        compiler_params=_params(("arbitrary", "arbitrary"), 52),
        name="inproj",
    )(x, g, w_main, w_if, b_if)


def _top_blocks_mask(scores_t, n_valid, own, n_sel):
    R = scores_t.shape[0]
    rown = _iota(scores_t.shape, 0)
    s = jnp.where(rown < n_valid, scores_t, -jnp.inf)
    mask = jnp.where(rown == own, 0.0, NEG)
    for _ in range(n_sel):
        mx = jnp.max(s, axis=0, keepdims=True)
        idx = jnp.min(jnp.where(s == mx, rown, R), axis=0, keepdims=True)
        pick = (rown == idx) & (mx > -jnp.inf)
        mask = jnp.where(pick, 0.0, mask)
        s = jnp.where(pick, -jnp.inf, s)
    return mask


ATT_HEADS = 2
ATT_CHUNK = 4


def _moba_prompt_body(slopes_ref, q_ref, kb_ref, vb_ref, kmean_ref, o_ref, kaug, vt, qaug, acc_scr, *, nb):
    hp = pl.program_id(1)
    qi = pl.program_id(2)
    B = MOBA_BLOCK
    CK = ATT_CHUNK * B
    heads = [slice(hh * HEAD_DIM, (hh + 1) * HEAD_DIM) for hh in range(ATT_HEADS)]

    @pl.when(qi == 0)
    def _init():
        lane = _iota((B, LANES), 1)
        s_lo = _iota((B, LANES), 0).astype(f32)
        for hh, cs in enumerate(heads):
            slope = slopes_ref[hp * ATT_HEADS + hh]
            for n in range(nb):
                c, r0 = n // ATT_CHUNK, (n % ATT_CHUNK) * B
                kaug[hh, c, r0:r0 + B, 0:HEAD_DIM] = kb_ref[0, n * B:(n + 1) * B, cs]
                ext = jnp.where(lane == n, 1.0, 0.0)
                ext = jnp.where(lane == MAX_BLOCKS, -slope, ext)
                ext = jnp.where(lane == MAX_BLOCKS + 1, slope * s_lo, ext)
                ext = jnp.where(lane == MAX_BLOCKS + 2, -slope * B, ext)
                ext = jnp.where(lane == MAX_BLOCKS + 3, slope * (B * n), ext)
                kaug[hh, c, r0:r0 + B, HEAD_DIM:2 * HEAD_DIM] = ext.astype(bf16)
                vt[hh, c, :, r0:r0 + B] = vb_ref[0, n * B:(n + 1) * B, cs].astype(f32).T.astype(bf16)

    c_own = qi // ATT_CHUNK
    causal = (_iota((CK, B), 0) - _iota((CK, B), 1)) <= (qi - c_own * ATT_CHUNK) * B
    n_ext = HEAD_DIM - MAX_BLOCKS
    rr = _iota((n_ext, B), 0)
    t_lo = _iota((n_ext, B), 1).astype(f32)
    q_ext = jnp.where(rr == 0, t_lo, jnp.where((rr == 1) | (rr == 3), 1.0, 0.0))
    q_ext = jnp.where(rr == 2, qi.astype(f32), q_ext).astype(bf16)
    state = []
    for hh, cs in enumerate(heads):
        q_t = q_ref[0, :, cs].astype(f32).T.astype(bf16)
        scores_t = _dot(kmean_ref[0, :, cs].astype(bf16), q_t)
        mask_t = _top_blocks_mask(scores_t, qi, qi, min(MOBA_TOP_K, nb))
        if nb < MAX_BLOCKS:
            mask_t = jnp.concatenate([mask_t, jnp.full((MAX_BLOCKS - nb, B), NEG, f32)], axis=0)
        qaug[hh, 0:HEAD_DIM, :] = q_t
        qaug[hh, HEAD_DIM:HEAD_DIM + MAX_BLOCKS, :] = mask_t.astype(bf16)
        qaug[hh, HEAD_DIM + MAX_BLOCKS:, :] = q_ext
        s = jnp.where(causal, _dot(kaug[hh, c_own], qaug[hh]), NEG)
        m = jnp.max(s, axis=0, keepdims=True)
        p = jnp.exp(s - m)
        state.append((m, jnp.sum(p, axis=0, keepdims=True)))
        acc_scr[hh] = _dot(vt[hh, c_own], p.astype(bf16))

    def body(c, carry):
        new = []
        for hh in range(ATT_HEADS):
            m, l = carry[hh]
            s = _dot(kaug[hh, c], qaug[hh])
            m_new = jnp.maximum(m, jnp.max(s, axis=0, keepdims=True))
            alpha = jnp.exp(m - m_new)
            p = jnp.exp(s - m_new)
            acc_scr[hh] = alpha * acc_scr[hh] + _dot(vt[hh, c], p.astype(bf16))
            new.append((m_new, alpha * l + jnp.sum(p, axis=0, keepdims=True)))
        return tuple(new)

    state = lax.fori_loop(0, c_own, body, tuple(state))
    for hh, cs in enumerate(heads):
        o_ref[0, :, cs] = (acc_scr[hh] / state[hh][1]).T.astype(bf16)


def _moba_prompt(slopes, q, kb, vb, kmean):
    Bsz, T, _ = q.shape
    assert T % (MOBA_BLOCK * ATT_CHUNK) == 0 and H_ATT % ATT_HEADS == 0
    nb = T // MOBA_BLOCK
    assert nb <= MAX_BLOCKS
    blk = MOBA_BLOCK
    W = ATT_HEADS * HEAD_DIM
    nc = nb // ATT_CHUNK
    return pl.pallas_call(
        functools.partial(_moba_prompt_body, nb=nb),
        grid=(Bsz, H_ATT // ATT_HEADS, nb),
        in_specs=[
            pl.BlockSpec(memory_space=pltpu.SMEM),
            pl.BlockSpec((1, blk, W), lambda b, h, i: (b, i, h)),
            pl.BlockSpec((1, T, W), lambda b, h, i: (b, 0, h)),
            pl.BlockSpec((1, T, W), lambda b, h, i: (b, 0, h)),
            pl.BlockSpec((1, nb, W), lambda b, h, i: (b, 0, h)),
        ],
        out_specs=pl.BlockSpec((1, blk, W), lambda b, h, i: (b, i, h)),
        out_shape=jax.ShapeDtypeStruct((Bsz, T, H_ATT * HEAD_DIM), bf16),
        scratch_shapes=[
            pltpu.VMEM((ATT_HEADS, nc, ATT_CHUNK * blk, 2 * HEAD_DIM), bf16),
            pltpu.VMEM((ATT_HEADS, nc, HEAD_DIM, ATT_CHUNK * blk), bf16),
            pltpu.VMEM((ATT_HEADS, 2 * HEAD_DIM, blk), bf16),
            pltpu.VMEM((ATT_HEADS, HEAD_DIM, blk), f32),
        ],
        compiler_params=_params(("arbitrary", "arbitrary", "arbitrary"), 52),
        name="moba_prompt",
    )(slopes, q, kb, vb, kmean)


KMEAN_PAGES = 16


IDX_ROWS = 16


def _kmean_select_body(pt_ref, *refs, L, n_sel, nbp):
    pages = refs[:KMEAN_PAGES]
    q_ref, idx_ref, km_scr = refs[KMEAN_PAGES:]
    j = pl.program_id(1)
    bps = KMEAN_PAGES // PAGES_PER_BLOCK
    for b in range(bps):
        s = jnp.sum(pages[PAGES_PER_BLOCK * b][...], axis=0)
        for p in range(1, PAGES_PER_BLOCK):
            s = s + jnp.sum(pages[PAGES_PER_BLOCK * b + p][...], axis=0)
        km_scr[j * bps + b] = s * (1.0 / MOBA_BLOCK)

    @pl.when(j == pl.num_programs(1) - 1)
    def _select():
        idx_ref[...] = jnp.zeros(idx_ref.shape, i32)
        km = km_scr[...]
        blk = _iota((nbp, H_ATT, 1), 0)
        for l in range(L):
            sc = jnp.sum(km * q_ref[l][None], axis=-1, keepdims=True)
            for r in range(n_sel):
                mx = jnp.max(sc, axis=0, keepdims=True)
                idx = jnp.min(jnp.where(sc == mx, blk, nbp), axis=0, keepdims=True)
                idx_ref[l * n_sel + r] = jnp.broadcast_to(idx[0], (H_ATT, LANES))
                sc = jnp.where(blk == idx, -jnp.inf, sc)


def _kmean_select_sample(page_table, cache_k, q8, *, layer, L, n_sel):
    DB, n_pages = page_table.shape
    assert n_pages % KMEAN_PAGES == 0 and L * n_sel <= IDX_ROWS
    nbp = n_pages // PAGES_PER_BLOCK

    def page_map(d, j, pt, i):
        return (layer, pt[d, j * KMEAN_PAGES + i], 0, 0, 0)

    in_specs = [pl.BlockSpec((None, None, PAGE_SIZE, H_ATT, HEAD_DIM), functools.partial(page_map, i=i))
                for i in range(KMEAN_PAGES)]
    in_specs.append(pl.BlockSpec((None, 8, H_ATT, HEAD_DIM), lambda d, j, pt: (d, 0, 0, 0)))
    return pl.pallas_call(
        functools.partial(_kmean_select_body, L=L, n_sel=n_sel, nbp=nbp),
        grid_spec=pltpu.PrefetchScalarGridSpec(
            num_scalar_prefetch=1,
            grid=(DB, n_pages // KMEAN_PAGES),
            in_specs=in_specs,
            out_specs=pl.BlockSpec((None, IDX_ROWS, H_ATT, LANES), lambda d, j, pt: (d, 0, 0, 0)),
            scratch_shapes=[pltpu.VMEM((nbp, H_ATT, HEAD_DIM), f32)],
        ),
        out_shape=jax.ShapeDtypeStruct((DB, IDX_ROWS, H_ATT, LANES), i32),
        compiler_params=_params(("arbitrary", "arbitrary"), 40),
        name="kmean_select",
    )(page_table, *([cache_k] * KMEAN_PAGES), q8)


def _sattn_body(idx_ref, pt_ref, slopes_ref, q_ref, kn_ref, vn_ref, ck_hbm, cv_hbm, o_ref, kbuf, vbuf, sem,
                *, layer, L, n_sel, past):
    nblk = L * n_sel * PAGES_PER_BLOCK
    d = pl.program_id(0)
    h = pl.program_id(1)
    step = d * H_ATT + h
    slot = step % 2

    def page_copies(dd, hh, sl):
        out = []
        for i in range(nblk):
            lj, p = i // PAGES_PER_BLOCK, i % PAGES_PER_BLOCK
            blk = idx_ref[((dd * L + lj // n_sel) * H_ATT + hh) * n_sel + lj % n_sel]
            phys = pt_ref[dd, blk * PAGES_PER_BLOCK + p]
            out.append(pltpu.make_async_copy(ck_hbm.at[layer, phys, :, hh, :], kbuf.at[sl, i], sem.at[sl]))
            out.append(pltpu.make_async_copy(cv_hbm.at[layer, phys, :, hh, :], vbuf.at[sl, i], sem.at[sl]))
        return out

    @pl.when(step == 0)
    def _():
        for cp in page_copies(d, h, slot):
            cp.start()

    @pl.when(step + 1 < pl.num_programs(0) * H_ATT)
    def _():
        nxt = step + 1
        for cp in page_copies(nxt // H_ATT, nxt % H_ATT, 1 - slot):
            cp.start()

    for cp in page_copies(d, h, slot):
        cp.wait()

    kp = [kbuf.at[slot, i] for i in range(nblk)]
    vp = [vbuf.at[slot, i] for i in range(nblk)]
    slope = slopes_ref[h]
    qf = q_ref[0].astype(f32)
    kn = kn_ref[0]
    vn = vn_ref[0]
    lpos = _iota((L, 1), 0)
    ppos = _iota((PAGE_SIZE, 1), 0)
    rows = []
    for l in range(L):
        ql = qf[l:l + 1, :]
        lo = jnp.sum(kn * ql, axis=1, keepdims=True)
        lo = jnp.where(lpos <= l, lo - slope * (l - lpos).astype(f32), NEG)
        logits = []
        for j in range(n_sel):
            blk = idx_ref[((d * L + l) * H_ATT + h) * n_sel + j]
            for p in range(PAGES_PER_BLOCK):
                kb = kp[(l * n_sel + j) * PAGES_PER_BLOCK + p][...]
                dist = (past + l - blk * MOBA_BLOCK - p * PAGE_SIZE) - ppos
                logits.append(jnp.sum(kb * ql, axis=1, keepdims=True) - slope * dist.astype(f32))
        m = jnp.max(lo, axis=0, keepdims=True)
        for lg in logits:
            m = jnp.maximum(m, jnp.max(lg, axis=0, keepdims=True))
        p_own = jnp.exp(lo - m)
        den = jnp.sum(p_own, axis=0, keepdims=True)
        num = jnp.sum(p_own * vn, axis=0, keepdims=True)
        for i, lg in enumerate(logits):
            pi = jnp.exp(lg - m)
            den = den + jnp.sum(pi, axis=0, keepdims=True)
            num = num + jnp.sum(pi * vp[l * n_sel * PAGES_PER_BLOCK + i][...], axis=0, keepdims=True)
        rows.append(num / den)
    rows.append(jnp.zeros((8 - L, HEAD_DIM), f32))
    o_ref[0] = jnp.concatenate(rows, axis=0).astype(bf16)


def _sattn_sample(idx_flat, page_table, slopes, q8, k_new, v_new, cache_k, cache_v, *, layer, L, n_sel, past):
    DB = q8.shape[0]
    assert L <= 8
    nblk = L * n_sel * PAGES_PER_BLOCK
    return pl.pallas_call(
        functools.partial(_sattn_body, layer=layer, L=L, n_sel=n_sel, past=past),
        grid_spec=pltpu.PrefetchScalarGridSpec(
            num_scalar_prefetch=2,
            grid=(DB, H_ATT),
            in_specs=[
                pl.BlockSpec(memory_space=pltpu.SMEM),
                pl.BlockSpec((1, 8, HEAD_DIM), lambda d, h, idx, pt: (d, 0, h)),
                pl.BlockSpec((1, L, HEAD_DIM), lambda d, h, idx, pt: (d, 0, h)),
                pl.BlockSpec((1, L, HEAD_DIM), lambda d, h, idx, pt: (d, 0, h)),
                pl.BlockSpec(memory_space=pl.ANY),
                pl.BlockSpec(memory_space=pl.ANY),
            ],
            out_specs=pl.BlockSpec((1, 8, HEAD_DIM), lambda d, h, idx, pt: (d, 0, h)),
            scratch_shapes=[
                pltpu.VMEM((2, nblk, PAGE_SIZE, HEAD_DIM), f32),
                pltpu.VMEM((2, nblk, PAGE_SIZE, HEAD_DIM), f32),
                pltpu.SemaphoreType.DMA((2,)),
            ],
        ),
        out_shape=jax.ShapeDtypeStruct((DB, 8, H_ATT * HEAD_DIM), bf16),
        compiler_params=_params(("arbitrary", "arbitrary"), 32),
        name="sattn_sample",
    )(idx_flat, page_table, slopes, q8, k_new, v_new, cache_k, cache_v)


def _split3_dot(a_bf16, x):
    x1 = x.astype(bf16)
    r1 = x - x1.astype(f32)
    x2 = r1.astype(bf16)
    x3 = (r1 - x2.astype(f32)).astype(bf16)
    return _dot(a_bf16, x1) + _dot(a_bf16, x2) + _dot(a_bf16, x3)


def _mlstm_body(zm_ref, gate_ref, c0_ref, n0_ref, m0_ref, hm_ref, c_out, n_out, m_out, c_scr, n_scr, m_scr, *, L, l_out):
    c = pl.program_id(1)
    nc = pl.num_programs(1)

    @pl.when(c == 0)
    def _():
        c_scr[...] = c0_ref[0]
        n_scr[...] = n0_ref[0]
        m_scr[...] = m0_ref[0]

    G = gate_ref[0]
    row = _iota((L, L), 0)
    col = _iota((L, L), 1)
    causal = row >= col
    csum = _split3_dot(jnp.where(causal, 1.0, 0.0).astype(bf16), G)
    lane = _iota((L, LANES), 1)
    X = jnp.where(lane < H_MLSTM, pltpu.roll(csum, LANES - H_MLSTM, 1), pltpu.roll(G, H_MLSTM, 1) - csum)
    XT = X.T
    for h in range(H_MLSTM):
        sl = slice(h * HEAD_DIM, (h + 1) * HEAD_DIM)
        q = zm_ref[0, :, sl]
        k = zm_ref[0, :, SEG + h * HEAD_DIM:SEG + (h + 1) * HEAD_DIM]
        v = zm_ref[0, :, 2 * SEG + h * HEAD_DIM:2 * SEG + (h + 1) * HEAD_DIM]
        o = zm_ref[0, :, 3 * SEG + h * HEAD_DIM:3 * SEG + (h + 1) * HEAD_DIM]
        b_col = X[:, h:h + 1]
        g_col = X[:, H_MLSTM + h:H_MLSTM + h + 1]
        g_row = XT[H_MLSTM + h:H_MLSTM + h + 1, :]
        m_prev = m_scr[h:h + 1, 0:1]
        log_d = jnp.where(causal, b_col + g_row, -jnp.inf)
        m_inter = b_col + m_prev
        m_t = jnp.maximum(m_inter, jnp.max(log_d, axis=1, keepdims=True))
        s = _dot_nt(q, k) * jnp.exp(log_d - m_t)
        a_inter = jnp.exp(m_inter - m_t)
        c_h = c_scr[h]
        n_h = n_scr[h:h + 1, :]
        num = _dot(s.astype(bf16), v) + a_inter * _dot_nt(q, c_h.astype(bf16))
        den = jnp.sum(s, axis=1, keepdims=True) + a_inter * jnp.sum(q.astype(f32) * n_h, axis=1, keepdims=True)
        hh = num / jnp.maximum(jnp.abs(den), jnp.exp(-m_t))
        out = (jax.nn.sigmoid(o.astype(f32)) * hh).astype(bf16)
        hm_ref[0, :, sl] = out[:l_out]
        m_new = m_t[L - 1:L, :]
        b_last = b_col[L - 1:L, :]
        decay = jnp.exp(b_last + m_prev - m_new)
        w = jnp.exp(b_last + g_col - m_new)
        wk = w * k.astype(f32)
        c_scr[h] = decay * c_h + _dot_tn(v, wk.astype(bf16))
        n_scr[h:h + 1, :] = decay * n_h + jnp.sum(wk, axis=0, keepdims=True)
        m_scr[h:h + 1, :] = jnp.broadcast_to(m_new, (1, LANES))

    @pl.when(c == nc - 1)
    def _():
        c_out[0] = c_scr[...]
        n_out[0] = n_scr[...]
        m_out[0] = m_scr[...]


def _mlstm(zm, gates, c0, n0, m0, *, L, l_out):
    Bsz, T, _ = zm.shape
    nc = T // L
    return pl.pallas_call(
        functools.partial(_mlstm_body, L=L, l_out=l_out),
        grid=(Bsz, nc),
        in_specs=[
            pl.BlockSpec((1, L, 4 * SEG), lambda b, c: (b, c, 0)),
            pl.BlockSpec((1, L, LANES), lambda b, c: (b, c, 0)),
            pl.BlockSpec((1, H_MLSTM, HEAD_DIM, HEAD_DIM), lambda b, c: (b, 0, 0, 0)),
            pl.BlockSpec((1, H_MLSTM, HEAD_DIM), lambda b, c: (b, 0, 0)),
            pl.BlockSpec((1, H_MLSTM, LANES), lambda b, c: (b, 0, 0)),
        ],
        out_specs=[
            pl.BlockSpec((1, l_out, SEG), lambda b, c: (b, c, 0)),
            pl.BlockSpec((1, H_MLSTM, HEAD_DIM, HEAD_DIM), lambda b, c: (b, 0, 0, 0)),
            pl.BlockSpec((1, H_MLSTM, HEAD_DIM), lambda b, c: (b, 0, 0)),
            pl.BlockSpec((1, H_MLSTM, LANES), lambda b, c: (b, 0, 0)),
        ],
        out_shape=[
            jax.ShapeDtypeStruct((Bsz, nc * l_out, SEG), bf16),
            jax.ShapeDtypeStruct((Bsz, H_MLSTM, HEAD_DIM, HEAD_DIM), f32),
            jax.ShapeDtypeStruct((Bsz, H_MLSTM, HEAD_DIM), f32),
            jax.ShapeDtypeStruct((Bsz, H_MLSTM, LANES), f32),
        ],
        scratch_shapes=[
            pltpu.VMEM((H_MLSTM, HEAD_DIM, HEAD_DIM), f32),
            pltpu.VMEM((H_MLSTM, HEAD_DIM), f32),
            pltpu.VMEM((H_MLSTM, LANES), f32),
        ],
        compiler_params=_params(("arbitrary", "arbitrary"), 32),
        name="mlstm",
    )(zm, gates, c0, n0, m0)


def _outproj_body(x_ref, att_ref, hm_ref, wa_ref, wm_ref, o_ref):
    o_ref[...] = x_ref[...] + _dot(att_ref[...], wa_ref[...]) + _dot(hm_ref[...], wm_ref[...])


def _outproj(x, att, hm, w_out, *, tm, tn):
    M, D = x.shape
    return pl.pallas_call(
        _outproj_body,
        grid=(M // tm, D // tn),
        in_specs=[
            pl.BlockSpec((tm, tn), lambda m, n: (m, n)),
            pl.BlockSpec((tm, SEG), lambda m, n: (m, 0)),
            pl.BlockSpec((tm, SEG), lambda m, n: (m, 0)),
            pl.BlockSpec((SEG, tn), lambda m, n: (0, n)),
            pl.BlockSpec((SEG, tn), lambda m, n: (1, n)),
        ],
        out_specs=pl.BlockSpec((tm, tn), lambda m, n: (m, n)),
        out_shape=jax.ShapeDtypeStruct((M, D), f32),
        compiler_params=_params(("arbitrary", "arbitrary"), 40),
        name="outproj",
    )(x, att, hm, w_out, w_out)


def _ffn_body(*refs, tm, seq_len, carry_mode):
    if carry_mode:
        x_ref, g_ref, gf_ref, wg_ref, wu_ref, cw_ref, wd_ref, y_ref, tail_ref, h_scr, acc_scr, carry = refs
    else:
        x_ref, g_ref, gf_ref, wg_ref, wu_ref, cw_ref, wd_ref, p1_ref, p2_ref, y_ref, a_ref, h_scr, acc_scr = refs
    m = pl.program_id(0)
    f = pl.program_id(1)
    nf = pl.num_programs(1)

    @pl.when(f == 0)
    def _():
        h_scr[...] = _rmsnorm(x_ref[...], g_ref[...]).astype(bf16)
        acc_scr[...] = jnp.zeros(acc_scr.shape, f32)

    hb = h_scr[...]
    a = _dot(hb, wg_ref[...])
    u = _dot(hb, wu_ref[...])
    row = _iota(a.shape, 0)
    a1 = pltpu.roll(a, 1, 0)
    a2 = pltpu.roll(a, 2, 0)
    if carry_mode:
        prev = carry[f]
        prev = jnp.where(m % (seq_len // tm) == 0, 0.0, prev)
        a1 = jnp.where(row == 0, prev[7:8, :], a1)
        a2 = jnp.where(row == 0, prev[6:7, :], jnp.where(row == 1, prev[7:8, :], a2))
        carry[f] = a[tm - 8:, :]
        tail_ref[0] = a[tm - 8:, :]
    else:
        lpos = row % seq_len
        a1 = jnp.where(lpos >= 1, a1, p1_ref[...])
        a2 = jnp.where(lpos >= 2, a2, p2_ref[...])
        a_ref[...] = a
    cw = cw_ref[...]
    cv = cw[0:1, :] * a2 + cw[1:2, :] * a1 + cw[2:3, :] * a
    gl = 0.5 * cv * (1.0 + lax.erf(cv * (0.5 ** 0.5)))
    acc_scr[...] += _dot((gl * u).astype(bf16), wd_ref[...])

    @pl.when(f == nf - 1)
    def _():
        y_ref[...] = _rmsnorm(x_ref[...] + acc_scr[...], gf_ref[...])


def _ffn(x1, g_ffn, g_final, w_gate, w_up, conv_w, w_down, *, tm, tf, seq_len, prev=None):
    M, D = x1.shape
    DF = w_gate.shape[1]
    nf = DF // tf
    carry_mode = prev is None
    in_specs = [
        pl.BlockSpec((tm, D), lambda m, f: (m, 0)),
        pl.BlockSpec((1, D), lambda m, f: (0, 0)),
        pl.BlockSpec((1, D), lambda m, f: (0, 0)),
        pl.BlockSpec((D, tf), lambda m, f: (0, f)),
        pl.BlockSpec((D, tf), lambda m, f: (0, f)),
        pl.BlockSpec((CONV_W, tf), lambda m, f: (0, f)),
        pl.BlockSpec((tf, D), lambda m, f: (f, 0)),
    ]
    args = [x1, g_ffn, g_final, w_gate, w_up, conv_w, w_down]
    scratch = [pltpu.VMEM((tm, D), bf16), pltpu.VMEM((tm, D), f32)]
    if carry_mode:
        assert seq_len % tm == 0
        out_specs = [pl.BlockSpec((tm, D), lambda m, f: (m, 0)), pl.BlockSpec((1, 8, tf), lambda m, f: (m, 0, f))]
        out_shape = [jax.ShapeDtypeStruct((M, D), f32), jax.ShapeDtypeStruct((M // tm, 8, DF), f32)]
        scratch.append(pltpu.VMEM((nf, 8, tf), f32))
    else:
        assert tm % seq_len == 0
        in_specs += [pl.BlockSpec((tm, tf), lambda m, f: (m, f)), pl.BlockSpec((tm, tf), lambda m, f: (m, f))]
        args += list(prev)
        out_specs = [pl.BlockSpec((tm, D), lambda m, f: (m, 0)), pl.BlockSpec((tm, tf), lambda m, f: (m, f))]
        out_shape = [jax.ShapeDtypeStruct((M, D), f32), jax.ShapeDtypeStruct((M, DF), f32)]
    return pl.pallas_call(
        functools.partial(_ffn_body, tm=tm, seq_len=seq_len, carry_mode=carry_mode),
        grid=(M // tm, nf),
        in_specs=in_specs,
        out_specs=out_specs,
        out_shape=out_shape,
        scratch_shapes=scratch,
        compiler_params=_params(("arbitrary", "arbitrary"), 52),
        name="convffn",
    )(*args)


def kernel(x_prompt, x_sample, cache_k, cache_v, state_C, state_n, state_m, state_conv, page_table,
           norm_mix, w_in, b_if, w_out, norm_ffn, w_gate, w_up, conv_w, w_down, norm_final):
    Bsz, T, D = x_prompt.shape
    DB, Ls, _ = x_sample.shape
    depth = w_in.shape[0]
    n_pages = page_table.shape[1]
    past = n_pages * PAGE_SIZE
    assert past % MOBA_BLOCK == 0 and past // MOBA_BLOCK >= 1, "sample path assumes a block-aligned, non-empty past"
    nbp = past // MOBA_BLOCK
    n_sel_s = min(MOBA_TOP_K, nbp)
    slopes = 2.0 ** (-8.0 * jnp.arange(1, H_ATT + 1, dtype=f32) / H_ATT)
    n_main = N_SEG * SEG
    TM_P, TM_S = 512, DB * Ls
    L_P, L_S = 256, 128

    xp = x_prompt.reshape(Bsz * T, D)
    xs = x_sample.reshape(DB * Ls, D)
    st_p, st_s = [], []
    for l in range(depth):
        w_main = w_in[l, :, :n_main].astype(bf16)
        w_if = jnp.pad(w_in[l, :, n_main:], ((0, 0), (0, LANES - 2 * H_MLSTM))).astype(bf16)
        bif = jnp.pad(b_if[l], (0, LANES - 2 * H_MLSTM)).reshape(1, LANES)
        wo = w_out[l].astype(bf16)
        wg, wu, wd = w_gate[l].astype(bf16), w_up[l].astype(bf16), w_down[l].astype(bf16)
        g_mix, g_ffn = norm_mix[l].reshape(1, D), norm_ffn[l].reshape(1, D)
        g_fin = norm_final.reshape(1, D) if l == depth - 1 else jnp.ones((1, D), f32)
        assert depth == 1, "final norm is fused into the last layer's FFN call"

        q, k, v, zm, gates, kb, vb, kmean = _inproj(xp, g_mix, w_main, w_if, bif, tm=TM_P, attn_aux=True)
        att = _moba_prompt(slopes, q.reshape(Bsz, T, SEG), kb.reshape(Bsz, T, SEG), vb.reshape(Bsz, T, SEG),
                           kmean.reshape(Bsz, T // MOBA_BLOCK, SEG))
        hm, c_p, n_p, m_p = _mlstm(
            zm.reshape(Bsz, T, 4 * SEG), gates.reshape(Bsz, T, LANES),
            jnp.zeros((Bsz, H_MLSTM, HEAD_DIM, HEAD_DIM), f32), jnp.zeros((Bsz, H_MLSTM, HEAD_DIM), f32),
            jnp.zeros((Bsz, H_MLSTM, LANES), f32), L=L_P, l_out=L_P)
        x1 = _outproj(xp, att.reshape(Bsz * T, SEG), hm.reshape(Bsz * T, SEG), wo, tm=TM_P, tn=1024)
        xp, tail = _ffn(x1, g_ffn, g_fin, wg, wu, conv_w[l], wd, tm=TM_P, tf=512, seq_len=T)
        conv_p = tail.reshape(Bsz, T // TM_P, 8, -1)[:, -1, 8 - (CONV_W - 1):, :]
        st_p.append((k.reshape(Bsz, T, H_ATT, HEAD_DIM), v.reshape(Bsz, T, H_ATT, HEAD_DIM), c_p, n_p, m_p[:, :, 0], conv_p))

        q, k, v, zm, gates = _inproj(xs, g_mix, w_main, w_if, bif, tm=TM_S, attn_aux=False)
        q8 = jnp.pad(q.reshape(DB, Ls, SEG), ((0, 0), (0, 8 - Ls), (0, 0)))
        idx = _kmean_select_sample(page_table, cache_k, q8.astype(f32).reshape(DB, 8, H_ATT, HEAD_DIM),
                                   layer=l, L=Ls, n_sel=n_sel_s)
        idx = idx[:, :Ls * n_sel_s, :, 0].reshape(DB, Ls, n_sel_s, H_ATT)
        idx_flat = jnp.transpose(idx, (0, 1, 3, 2)).reshape(-1)
        att8 = _sattn_sample(idx_flat, page_table, slopes, q8, k.reshape(DB, Ls, SEG), v.reshape(DB, Ls, SEG),
                             cache_k, cache_v, layer=l, L=Ls, n_sel=n_sel_s, past=past)
        att = att8[:, :Ls].reshape(DB * Ls, SEG)
        pad_t = ((0, 0), (0, L_S - Ls), (0, 0))
        gate_pad = jnp.where(jnp.arange(LANES) < H_MLSTM, NEG, 0.0).astype(f32)
        gates_s = jnp.concatenate(
            [gates.reshape(DB, Ls, LANES), jnp.broadcast_to(gate_pad, (DB, L_S - Ls, LANES))], axis=1)
        hm8, c_s, n_s, m_s = _mlstm(
            jnp.pad(zm.reshape(DB, Ls, 4 * SEG), pad_t), gates_s, state_C[l], state_n[l],
            jnp.broadcast_to(state_m[l][:, :, None], (DB, H_MLSTM, LANES)), L=L_S, l_out=8)
        hm = hm8[:, :Ls].reshape(DB * Ls, SEG)
        x1 = _outproj(xs, att, hm, wo, tm=TM_S, tn=1024)
        buf = state_conv[l]
        zero = jnp.zeros_like(buf[:, 0])
        p1 = jnp.stack([buf[:, 1]] + [zero] * (Ls - 1), axis=1).reshape(DB * Ls, -1)
        p2 = jnp.stack([buf[:, 0], buf[:, 1]] + [zero] * (Ls - 2), axis=1).reshape(DB * Ls, -1)
        xs, a_full = _ffn(x1, g_ffn, g_fin, wg, wu, conv_w[l], wd, tm=TM_S, tf=512, seq_len=Ls, prev=(p1, p2))
        conv_s = a_full.reshape(DB, Ls, -1)[:, Ls - (CONV_W - 1):]
        st_s.append((k.reshape(DB, Ls, H_ATT, HEAD_DIM), v.reshape(DB, Ls, H_ATT, HEAD_DIM), c_s, n_s, m_s[:, :, 0], conv_s))

    k_p, v_p, C_p, n_p, m_p, conv_p = [jnp.stack(z) for z in zip(*st_p)]
    k_s, v_s, C_s, n_s, m_s, conv_s = [jnp.stack(z) for z in zip(*st_s)]
    return (xp.reshape(Bsz, T, D), xs.reshape(DB, Ls, D), k_p, v_p, C_p, n_p, m_p, conv_p,
            k_s, v_s, C_s, n_s, m_s, conv_s)
```

```python
import functools

import jax
import jax.numpy as jnp
from jax import lax
from jax.experimental import pallas as pl
from jax.experimental.pallas import tpu as pltpu

f32, bf16, i32 = jnp.float32, jnp.bfloat16, jnp.int32

HEAD_DIM = 128
H_ATT = 8
H_MLSTM = 8
SEG = H_ATT * HEAD_DIM
N_SEG = 7
CONV_W = 3
MOBA_BLOCK = 256
MOBA_TOP_K = 3
PAGE_SIZE = 128
PAGES_PER_BLOCK = MOBA_BLOCK // PAGE_SIZE
EPS = 1e-6
NEG = -1e30
LANES = 128
MAX_BLOCKS = 32
MIB = 1 << 20


def _params(sem, vmem_mib):
    return pltpu.CompilerParams(dimension_semantics=sem, vmem_limit_bytes=vmem_mib * MIB)


def _iota(shape, dim, dtype=i32):
    return lax.broadcasted_iota(dtype, shape, dim)


def _dot(a, b):
    return jnp.dot(a, b, preferred_element_type=f32)


def _dot_nt(a, b):
    return lax.dot_general(a, b, (((1,), (1,)), ((), ())), preferred_element_type=f32)


def _dot_tn(a, b):
    return lax.dot_general(a, b, (((0,), (0,)), ((), ())), preferred_element_type=f32)


def _rmsnorm(x, g):
    return x * lax.rsqrt(jnp.mean(x * x, axis=-1, keepdims=True) + EPS) * g


def _inproj_body(x_ref, g_ref, w_ref, wif_ref, bif_ref, q_ref, k_ref, v_ref, zm_ref, gate_ref, *rest, attn_aux):
    if attn_aux:
        kb_ref, vb_ref, kmean_ref, h_scr = rest
    else:
        (h_scr,) = rest
    n = pl.program_id(1)
    scale = HEAD_DIM ** -0.5

    @pl.when(n == 0)
    def _():
        hb = _rmsnorm(x_ref[...], g_ref[...]).astype(bf16)
        h_scr[...] = hb
        zg = _dot(hb, wif_ref[...]) + bif_ref[...]
        lane = _iota(zg.shape, 1)
        log_sig = jnp.minimum(zg, 0.0) - jnp.log1p(jnp.exp(-jnp.abs(zg)))
        gate_ref[...] = jnp.where(lane < H_MLSTM, zg, jnp.where(lane < 2 * H_MLSTM, log_sig, 0.0))

    z = _dot(h_scr[...], w_ref[...])

    @pl.when(n == 0)
    def _():
        q_ref[...] = (z * scale).astype(bf16)

    @pl.when(n == 1)
    def _():
        k_ref[...] = z
        if attn_aux:
            kb_ref[...] = z.astype(bf16)
            for i in range(z.shape[0] // MOBA_BLOCK):
                blk = z[i * MOBA_BLOCK:(i + 1) * MOBA_BLOCK, :]
                kmean_ref[0, i:i + 1, :] = jnp.sum(blk, axis=0, keepdims=True) * (1.0 / MOBA_BLOCK)

    @pl.when(n == 2)
    def _():
        v_ref[...] = z
        if attn_aux:
            vb_ref[...] = z.astype(bf16)

    @pl.when(n == 4)
    def _():
        zm_ref[...] = (z * scale).astype(bf16)

    @pl.when((n == 3) | (n >= 5))
    def _():
        zm_ref[...] = z.astype(bf16)


def _inproj(x, g, w_main, w_if, b_if, *, tm, attn_aux):
    M, D = x.shape
    grid = (M // tm, N_SEG)

    def row(m, n):
        return (m, 0)

    out_specs = [
        pl.BlockSpec((tm, SEG), row),
        pl.BlockSpec((tm, SEG), row),
        pl.BlockSpec((tm, SEG), row),
        pl.BlockSpec((tm, SEG), lambda m, n: (m, jnp.maximum(n - 3, 0))),
        pl.BlockSpec((tm, LANES), row),
    ]
    out_shape = [
        jax.ShapeDtypeStruct((M, SEG), bf16),
        jax.ShapeDtypeStruct((M, SEG), f32),
        jax.ShapeDtypeStruct((M, SEG), f32),
        jax.ShapeDtypeStruct((M, 4 * SEG), bf16),
        jax.ShapeDtypeStruct((M, LANES), f32),
    ]
    if attn_aux:
        assert tm % MOBA_BLOCK == 0
        bpt = tm // MOBA_BLOCK
        out_specs += [pl.BlockSpec((tm, SEG), row), pl.BlockSpec((tm, SEG), row),
                      pl.BlockSpec((1, bpt, SEG), lambda m, n: (m, 0, 0))]
        out_shape += [jax.ShapeDtypeStruct((M, SEG), bf16), jax.ShapeDtypeStruct((M, SEG), bf16),
                      jax.ShapeDtypeStruct((M // tm, bpt, SEG), f32)]
    return pl.pallas_call(
        functools.partial(_inproj_body, attn_aux=attn_aux),
        grid=grid,
        in_specs=[
            pl.BlockSpec((tm, D), row),
            pl.BlockSpec((1, D), lambda m, n: (0, 0)),
            pl.BlockSpec((D, SEG), lambda m, n: (0, n)),
            pl.BlockSpec((D, LANES), lambda m, n: (0, 0)),
            pl.BlockSpec((1, LANES), lambda m, n: (0, 0)),
        ],
        out_specs=out_specs,
        out_shape=out_shape,
        scratch_shapes=[pltpu.VMEM((tm, D), bf16)],
        compiler_params=_params(("arbitrary", "arbitrary"), 52),
        name="inproj",
    )(x, g, w_main, w_if, b_if)


def _top_blocks_mask(scores_t, n_valid, own, n_sel):
    R = scores_t.shape[0]
    rown = _iota(scores_t.shape, 0)
    s = jnp.where(rown < n_valid, scores_t, -jnp.inf)
    mask = jnp.where(rown == own, 0.0, NEG)
    for _ in range(n_sel):
        mx = jnp.max(s, axis=0, keepdims=True)
        idx = jnp.min(jnp.where(s == mx, rown, R), axis=0, keepdims=True)
        pick = (rown == idx) & (mx > -jnp.inf)
        mask = jnp.where(pick, 0.0, mask)
        s = jnp.where(pick, -jnp.inf, s)
    return mask


ATT_HEADS = 2
ATT_CHUNK = 4


def _moba_prompt_body(slopes_ref, q_ref, kb_ref, vb_ref, kmean_ref, o_ref, kaug, vt, qaug, acc_scr, s_scr, ml_scr,
                      *, nb):
    hp = pl.program_id(1)
    qi = pl.program_id(2)
    B = MOBA_BLOCK
    CK = ATT_CHUNK * B
    heads = [slice(hh * HEAD_DIM, (hh + 1) * HEAD_DIM) for hh in range(ATT_HEADS)]

    @pl.when(qi == 0)
    def _init():
        lane = _iota((B, LANES), 1)
        s_lo = _iota((B, LANES), 0).astype(f32)
        for hh, cs in enumerate(heads):
            slope = slopes_ref[hp * ATT_HEADS + hh]
            for n in range(nb):
                c, r0 = n // ATT_CHUNK, (n % ATT_CHUNK) * B
                kaug[hh, c, r0:r0 + B, 0:HEAD_DIM] = kb_ref[0, n * B:(n + 1) * B, cs]
                ext = jnp.where(lane == n, 1.0, 0.0)
                ext = jnp.where(lane == MAX_BLOCKS, -slope, ext)
                ext = jnp.where(lane == MAX_BLOCKS + 1, slope * s_lo, ext)
                ext = jnp.where(lane == MAX_BLOCKS + 2, -slope * B, ext)
                ext = jnp.where(lane == MAX_BLOCKS + 3, slope * (B * n), ext)
                kaug[hh, c, r0:r0 + B, HEAD_DIM:2 * HEAD_DIM] = ext.astype(bf16)
                vt[hh, c, :, r0:r0 + B] = vb_ref[0, n * B:(n + 1) * B, cs].astype(f32).T.astype(bf16)

    c_own = qi // ATT_CHUNK
    n_ext = HEAD_DIM - MAX_BLOCKS
    rr = _iota((n_ext, B), 0)
    t_lo = _iota((n_ext, B), 1).astype(f32)
    q_ext = jnp.where(rr == 0, t_lo, jnp.where((rr == 1) | (rr == 3), 1.0, 0.0))
    q_ext = jnp.where(rr == 2, qi.astype(f32), q_ext).astype(bf16)
    for hh, cs in enumerate(heads):
        q_t = q_ref[0, :, cs].astype(f32).T.astype(bf16)
        scores_t = _dot(kmean_ref[0, :, cs].astype(bf16), q_t)
        mask_t = _top_blocks_mask(scores_t, qi, qi, min(MOBA_TOP_K, nb))
        if nb < MAX_BLOCKS:
            mask_t = jnp.concatenate([mask_t, jnp.full((MAX_BLOCKS - nb, B), NEG, f32)], axis=0)
        qaug[hh, 0:HEAD_DIM, :] = q_t
        qaug[hh, HEAD_DIM:HEAD_DIM + MAX_BLOCKS, :] = mask_t.astype(bf16)
        qaug[hh, HEAD_DIM + MAX_BLOCKS:, :] = q_ext
        s_scr[hh, 0] = _dot(kaug[hh, 0], qaug[hh])
        ml_scr[2 * hh:2 * hh + 1, :] = jnp.full((1, B), NEG, f32)
        ml_scr[2 * hh + 1:2 * hh + 2, :] = jnp.zeros((1, B), f32)
        acc_scr[hh] = jnp.zeros((HEAD_DIM, B), f32)

    def softmax_pv(hh, c, s):
        m = ml_scr[2 * hh:2 * hh + 1, :]
        m_new = jnp.maximum(m, jnp.max(s, axis=0, keepdims=True))
        alpha = jnp.exp(m - m_new)
        p = jnp.exp(s - m_new)
        ml_scr[2 * hh:2 * hh + 1, :] = m_new
        ml_scr[2 * hh + 1:2 * hh + 2, :] = alpha * ml_scr[2 * hh + 1:2 * hh + 2, :] + jnp.sum(p, axis=0, keepdims=True)
        acc_scr[hh] = alpha * acc_scr[hh] + _dot(vt[hh, c], p.astype(bf16))

    def stage(c, cur, nxt):
        for hh in range(ATT_HEADS):
            s_scr[hh, nxt] = _dot(kaug[hh, c + 1], qaug[hh])
        for hh in range(ATT_HEADS):
            softmax_pv(hh, c, s_scr[hh, cur])

    def body(c, carry):
        @pl.when(c % 2 == 0)
        def _():
            stage(c, 0, 1)

        @pl.when(c % 2 == 1)
        def _():
            stage(c, 1, 0)

        return carry

    lax.fori_loop(0, c_own, body, 0)
    causal = (_iota((CK, B), 0) - _iota((CK, B), 1)) <= (qi - c_own * ATT_CHUNK) * B
    for hh, cs in enumerate(heads):
        softmax_pv(hh, c_own, jnp.where(causal, s_scr[hh, c_own % 2], NEG))
        o_ref[0, :, cs] = (acc_scr[hh] / ml_scr[2 * hh + 1:2 * hh + 2, :]).T.astype(bf16)


def _moba_prompt(slopes, q, kb, vb, kmean):
    Bsz, T, _ = q.shape
    assert T % (MOBA_BLOCK * ATT_CHUNK) == 0 and H_ATT % ATT_HEADS == 0
    nb = T // MOBA_BLOCK
    assert nb <= MAX_BLOCKS
    blk = MOBA_BLOCK
    W = ATT_HEADS * HEAD_DIM
    nc = nb // ATT_CHUNK
    return pl.pallas_call(
        functools.partial(_moba_prompt_body, nb=nb),
        grid=(Bsz, H_ATT // ATT_HEADS, nb),
        in_specs=[
            pl.BlockSpec(memory_space=pltpu.SMEM),
            pl.BlockSpec((1, blk, W), lambda b, h, i: (b, i, h)),
            pl.BlockSpec((1, T, W), lambda b, h, i: (b, 0, h)),
            pl.BlockSpec((1, T, W), lambda b, h, i: (b, 0, h)),
            pl.BlockSpec((1, nb, W), lambda b, h, i: (b, 0, h)),
        ],
        out_specs=pl.BlockSpec((1, blk, W), lambda b, h, i: (b, i, h)),
        out_shape=jax.ShapeDtypeStruct((Bsz, T, H_ATT * HEAD_DIM), bf16),
        scratch_shapes=[
            pltpu.VMEM((ATT_HEADS, nc, ATT_CHUNK * blk, 2 * HEAD_DIM), bf16),
            pltpu.VMEM((ATT_HEADS, nc, HEAD_DIM, ATT_CHUNK * blk), bf16),
            pltpu.VMEM((ATT_HEADS, 2 * HEAD_DIM, blk), bf16),
            pltpu.VMEM((ATT_HEADS, HEAD_DIM, blk), f32),
            pltpu.VMEM((ATT_HEADS, 2, ATT_CHUNK * blk, blk), f32),
            pltpu.VMEM((2 * ATT_HEADS, blk), f32),
        ],
        compiler_params=_params(("arbitrary", "arbitrary", "arbitrary"), 52),
        name="moba_prompt",
    )(slopes, q, kb, vb, kmean)


KMEAN_PAGES = 16


def _qsel_body(x_ref, g_ref, w_ref, o_ref):
    h = _rmsnorm(x_ref[...], g_ref[...])
    o_ref[...] = jnp.dot(h, w_ref[...], preferred_element_type=f32, precision=lax.Precision.HIGHEST)


def _qsel(x, g, w_q, *, tn=256):
    M, D = x.shape
    N = w_q.shape[1]
    return pl.pallas_call(
        _qsel_body,
        grid=(N // tn,),
        in_specs=[
            pl.BlockSpec((M, D), lambda n: (0, 0)),
            pl.BlockSpec((1, D), lambda n: (0, 0)),
            pl.BlockSpec((D, tn), lambda n: (0, n)),
        ],
        out_specs=pl.BlockSpec((M, tn), lambda n: (0, n)),
        out_shape=jax.ShapeDtypeStruct((M, N), f32),
        compiler_params=_params(("arbitrary",), 32),
        name="qsel",
    )(x, g, w_q)


IDX_ROWS = 16


def _kmean_select_body(pt_ref, *refs, L, n_sel, nbp):
    pages = refs[:KMEAN_PAGES]
    q_ref, idx_ref, km_scr = refs[KMEAN_PAGES:]
    j = pl.program_id(1)
    bps = KMEAN_PAGES // PAGES_PER_BLOCK
    for b in range(bps):
        s = jnp.sum(pages[PAGES_PER_BLOCK * b][...], axis=0)
        for p in range(1, PAGES_PER_BLOCK):
            s = s + jnp.sum(pages[PAGES_PER_BLOCK * b + p][...], axis=0)
        km_scr[j * bps + b] = s * (1.0 / MOBA_BLOCK)

    @pl.when(j == pl.num_programs(1) - 1)
    def _select():
        idx_ref[...] = jnp.zeros(idx_ref.shape, i32)
        km = km_scr[...]
        blk = _iota((nbp, H_ATT, 1), 0)
        for l in range(L):
            sc = jnp.sum(km * q_ref[l][None], axis=-1, keepdims=True)
            for r in range(n_sel):
                mx = jnp.max(sc, axis=0, keepdims=True)
                idx = jnp.min(jnp.where(sc == mx, blk, nbp), axis=0, keepdims=True)
                idx_ref[l * n_sel + r] = jnp.broadcast_to(idx[0], (H_ATT, LANES))
                sc = jnp.where(blk == idx, -jnp.inf, sc)


def _kmean_select_sample(page_table, cache_k, q8, *, layer, L, n_sel):
    DB, n_pages = page_table.shape
    assert n_pages % KMEAN_PAGES == 0 and L * n_sel <= IDX_ROWS
    nbp = n_pages // PAGES_PER_BLOCK

    def page_map(d, j, pt, i):
        return (layer, pt[d, j * KMEAN_PAGES + i], 0, 0, 0)

    in_specs = [pl.BlockSpec((None, None, PAGE_SIZE, H_ATT, HEAD_DIM), functools.partial(page_map, i=i))
                for i in range(KMEAN_PAGES)]
    in_specs.append(pl.BlockSpec((None, 8, H_ATT, HEAD_DIM), lambda d, j, pt: (d, 0, 0, 0)))
    return pl.pallas_call(
        functools.partial(_kmean_select_body, L=L, n_sel=n_sel, nbp=nbp),
        grid_spec=pltpu.PrefetchScalarGridSpec(
            num_scalar_prefetch=1,
            grid=(DB, n_pages // KMEAN_PAGES),
            in_specs=in_specs,
            out_specs=pl.BlockSpec((None, IDX_ROWS, H_ATT, LANES), lambda d, j, pt: (d, 0, 0, 0)),
            scratch_shapes=[pltpu.VMEM((nbp, H_ATT, HEAD_DIM), f32)],
        ),
        out_shape=jax.ShapeDtypeStruct((DB, IDX_ROWS, H_ATT, LANES), i32),
        compiler_params=_params(("arbitrary", "arbitrary"), 40),
        name="kmean_select",
    )(page_table, *([cache_k] * KMEAN_PAGES), q8)


def _sattn_body(idx_ref, pt_ref, slopes_ref, q_ref, kn_ref, vn_ref, ck_hbm, cv_hbm, o_ref, kbuf, vbuf, sem,
                *, layer, L, n_sel, past):
    nblk = L * n_sel * PAGES_PER_BLOCK
    d = pl.program_id(0)
    h = pl.program_id(1)
    step = d * H_ATT + h
    slot = step % 2

    def page_copies(dd, hh, sl):
        out = []
        for i in range(nblk):
            lj, p = i // PAGES_PER_BLOCK, i % PAGES_PER_BLOCK
            blk = idx_ref[((dd * L + lj // n_sel) * H_ATT + hh) * n_sel + lj % n_sel]
            phys = pt_ref[dd, blk * PAGES_PER_BLOCK + p]
            out.append(pltpu.make_async_copy(ck_hbm.at[layer, phys, :, hh, :], kbuf.at[sl, i], sem.at[sl]))
            out.append(pltpu.make_async_copy(cv_hbm.at[layer, phys, :, hh, :], vbuf.at[sl, i], sem.at[sl]))
        return out

    @pl.when(step == 0)
    def _():
        for cp in page_copies(d, h, slot):
            cp.start()

    @pl.when(step + 1 < pl.num_programs(0) * H_ATT)
    def _():
        nxt = step + 1
        for cp in page_copies(nxt // H_ATT, nxt % H_ATT, 1 - slot):
            cp.start()

    for cp in page_copies(d, h, slot):
        cp.wait()

    kp = [kbuf.at[slot, i] for i in range(nblk)]
    vp = [vbuf.at[slot, i] for i in range(nblk)]
    slope = slopes_ref[h]
    qf = q_ref[0].astype(f32)
    kn = kn_ref[0]
    vn = vn_ref[0]
    lpos = _iota((L, 1), 0)
    ppos = _iota((PAGE_SIZE, 1), 0)
    rows = []
    for l in range(L):
        ql = qf[l:l + 1, :]
        lo = jnp.sum(kn * ql, axis=1, keepdims=True)
        lo = jnp.where(lpos <= l, lo - slope * (l - lpos).astype(f32), NEG)
        logits = []
        for j in range(n_sel):
            blk = idx_ref[((d * L + l) * H_ATT + h) * n_sel + j]
            for p in range(PAGES_PER_BLOCK):
                kb = kp[(l * n_sel + j) * PAGES_PER_BLOCK + p][...]
                dist = (past + l - blk * MOBA_BLOCK - p * PAGE_SIZE) - ppos
                logits.append(jnp.sum(kb * ql, axis=1, keepdims=True) - slope * dist.astype(f32))
        m = jnp.max(lo, axis=0, keepdims=True)
        for lg in logits:
            m = jnp.maximum(m, jnp.max(lg, axis=0, keepdims=True))
        p_own = jnp.exp(lo - m)
        den = jnp.sum(p_own, axis=0, keepdims=True)
        num = jnp.sum(p_own * vn, axis=0, keepdims=True)
        for i, lg in enumerate(logits):
            pi = jnp.exp(lg - m)
            den = den + jnp.sum(pi, axis=0, keepdims=True)
            num = num + jnp.sum(pi * vp[l * n_sel * PAGES_PER_BLOCK + i][...], axis=0, keepdims=True)
        rows.append(num / den)
    rows.append(jnp.zeros((8 - L, HEAD_DIM), f32))
    o_ref[0] = jnp.concatenate(rows, axis=0).astype(bf16)


def _sattn_sample(idx_flat, page_table, slopes, q8, k_new, v_new, cache_k, cache_v, *, layer, L, n_sel, past):
    DB = q8.shape[0]
    assert L <= 8
    nblk = L * n_sel * PAGES_PER_BLOCK
    return pl.pallas_call(
        functools.partial(_sattn_body, layer=layer, L=L, n_sel=n_sel, past=past),
        grid_spec=pltpu.PrefetchScalarGridSpec(
            num_scalar_prefetch=2,
            grid=(DB, H_ATT),
            in_specs=[
                pl.BlockSpec(memory_space=pltpu.SMEM),
                pl.BlockSpec((1, 8, HEAD_DIM), lambda d, h, idx, pt: (d, 0, h)),
                pl.BlockSpec((1, L, HEAD_DIM), lambda d, h, idx, pt: (d, 0, h)),
                pl.BlockSpec((1, L, HEAD_DIM), lambda d, h, idx, pt: (d, 0, h)),
                pl.BlockSpec(memory_space=pl.ANY),
                pl.BlockSpec(memory_space=pl.ANY),
            ],
            out_specs=pl.BlockSpec((1, 8, HEAD_DIM), lambda d, h, idx, pt: (d, 0, h)),
            scratch_shapes=[
                pltpu.VMEM((2, nblk, PAGE_SIZE, HEAD_DIM), f32),
                pltpu.VMEM((2, nblk, PAGE_SIZE, HEAD_DIM), f32),
                pltpu.SemaphoreType.DMA((2,)),
            ],
        ),
        out_shape=jax.ShapeDtypeStruct((DB, 8, H_ATT * HEAD_DIM), bf16),
        compiler_params=_params(("arbitrary", "arbitrary"), 32),
        name="sattn_sample",
    )(idx_flat, page_table, slopes, q8, k_new, v_new, cache_k, cache_v)


def _split3_dot(a_bf16, x):
    x1 = x.astype(bf16)
    r1 = x - x1.astype(f32)
    x2 = r1.astype(bf16)
    x3 = (r1 - x2.astype(f32)).astype(bf16)
    return _dot(a_bf16, x1) + _dot(a_bf16, x2) + _dot(a_bf16, x3)


def _mlstm_body(zm_ref, gate_ref, c0_ref, n0_ref, m0_ref, hm_ref, c_out, n_out, m_out, c_scr, n_scr, m_scr, *, L, l_out):
    c = pl.program_id(1)
    nc = pl.num_programs(1)

    @pl.when(c == 0)
    def _():
        c_scr[...] = c0_ref[0]
        n_scr[...] = n0_ref[0]
        m_scr[...] = m0_ref[0]

    G = gate_ref[0]
    row = _iota((L, L), 0)
    col = _iota((L, L), 1)
    causal = row >= col
    csum = _split3_dot(jnp.where(causal, 1.0, 0.0).astype(bf16), G)
    lane = _iota((L, LANES), 1)
    X = jnp.where(lane < H_MLSTM, pltpu.roll(csum, LANES - H_MLSTM, 1), pltpu.roll(G, H_MLSTM, 1) - csum)
    XT = X.T
    for h in range(H_MLSTM):
        sl = slice(h * HEAD_DIM, (h + 1) * HEAD_DIM)
        q = zm_ref[0, :, sl]
        k = zm_ref[0, :, SEG + h * HEAD_DIM:SEG + (h + 1) * HEAD_DIM]
        v = zm_ref[0, :, 2 * SEG + h * HEAD_DIM:2 * SEG + (h + 1) * HEAD_DIM]
        o = zm_ref[0, :, 3 * SEG + h * HEAD_DIM:3 * SEG + (h + 1) * HEAD_DIM]
        b_col = X[:, h:h + 1]
        g_col = X[:, H_MLSTM + h:H_MLSTM + h + 1]
        g_row = XT[H_MLSTM + h:H_MLSTM + h + 1, :]
        m_prev = m_scr[h:h + 1, 0:1]
        log_d = jnp.where(causal, b_col + g_row, -jnp.inf)
        m_inter = b_col + m_prev
        m_t = jnp.maximum(m_inter, jnp.max(log_d, axis=1, keepdims=True))
        s = _dot_nt(q, k) * jnp.exp(log_d - m_t)
        a_inter = jnp.exp(m_inter - m_t)
        c_h = c_scr[h]
        n_h = n_scr[h:h + 1, :]
        num = _dot(s.astype(bf16), v) + a_inter * _dot_nt(q, c_h.astype(bf16))
        den = jnp.sum(s, axis=1, keepdims=True) + a_inter * jnp.sum(q.astype(f32) * n_h, axis=1, keepdims=True)
        hh = num / jnp.maximum(jnp.abs(den), jnp.exp(-m_t))
        out = (jax.nn.sigmoid(o.astype(f32)) * hh).astype(bf16)
        hm_ref[0, :, sl] = out[:l_out]
        m_new = m_t[L - 1:L, :]
        b_last = b_col[L - 1:L, :]
        decay = jnp.exp(b_last + m_prev - m_new)
        w = jnp.exp(b_last + g_col - m_new)
        wk = w * k.astype(f32)
        c_scr[h] = decay * c_h + _dot_tn(v, wk.astype(bf16))
        n_scr[h:h + 1, :] = decay * n_h + jnp.sum(wk, axis=0, keepdims=True)
        m_scr[h:h + 1, :] = jnp.broadcast_to(m_new, (1, LANES))

    @pl.when(c == nc - 1)
    def _():
        c_out[0] = c_scr[...]
        n_out[0] = n_scr[...]
        m_out[0] = m_scr[...]


def _mlstm(zm, gates, c0, n0, m0, *, L, l_out):
    Bsz, T, _ = zm.shape
    nc = T // L
    return pl.pallas_call(
        functools.partial(_mlstm_body, L=L, l_out=l_out),
        grid=(Bsz, nc),
        in_specs=[
            pl.BlockSpec((1, L, 4 * SEG), lambda b, c: (b, c, 0)),
            pl.BlockSpec((1, L, LANES), lambda b, c: (b, c, 0)),
            pl.BlockSpec((1, H_MLSTM, HEAD_DIM, HEAD_DIM), lambda b, c: (b, 0, 0, 0)),
            pl.BlockSpec((1, H_MLSTM, HEAD_DIM), lambda b, c: (b, 0, 0)),
            pl.BlockSpec((1, H_MLSTM, LANES), lambda b, c: (b, 0, 0)),
        ],
        out_specs=[
            pl.BlockSpec((1, l_out, SEG), lambda b, c: (b, c, 0)),
            pl.BlockSpec((1, H_MLSTM, HEAD_DIM, HEAD_DIM), lambda b, c: (b, 0, 0, 0)),
            pl.BlockSpec((1, H_MLSTM, HEAD_DIM), lambda b, c: (b, 0, 0)),
            pl.BlockSpec((1, H_MLSTM, LANES), lambda b, c: (b, 0, 0)),
        ],
        out_shape=[
            jax.ShapeDtypeStruct((Bsz, nc * l_out, SEG), bf16),
            jax.ShapeDtypeStruct((Bsz, H_MLSTM, HEAD_DIM, HEAD_DIM), f32),
            jax.ShapeDtypeStruct((Bsz, H_MLSTM, HEAD_DIM), f32),
            jax.ShapeDtypeStruct((Bsz, H_MLSTM, LANES), f32),
        ],
        scratch_shapes=[
            pltpu.VMEM((H_MLSTM, HEAD_DIM, HEAD_DIM), f32),
            pltpu.VMEM((H_MLSTM, HEAD_DIM), f32),
            pltpu.VMEM((H_MLSTM, LANES), f32),
        ],
        compiler_params=_params(("arbitrary", "arbitrary"), 32),
        name="mlstm",
    )(zm, gates, c0, n0, m0)


def _outproj_body(x_ref, att_ref, hm_ref, wa_ref, wm_ref, o_ref):
    o_ref[...] = x_ref[...] + _dot(att_ref[...], wa_ref[...]) + _dot(hm_ref[...], wm_ref[...])


def _outproj(x, att, hm, w_out, *, tm, tn):
    M, D = x.shape
    return pl.pallas_call(
        _outproj_body,
        grid=(M // tm, D // tn),
        in_specs=[
            pl.BlockSpec((tm, tn), lambda m, n: (m, n)),
            pl.BlockSpec((tm, SEG), lambda m, n: (m, 0)),
            pl.BlockSpec((tm, SEG), lambda m, n: (m, 0)),
            pl.BlockSpec((SEG, tn), lambda m, n: (0, n)),
            pl.BlockSpec((SEG, tn), lambda m, n: (1, n)),
        ],
        out_specs=pl.BlockSpec((tm, tn), lambda m, n: (m, n)),
        out_shape=jax.ShapeDtypeStruct((M, D), f32),
        compiler_params=_params(("arbitrary", "arbitrary"), 40),
        name="outproj",
    )(x, att, hm, w_out, w_out)


def _ffn_body(*refs, tm, seq_len, carry_mode):
    if carry_mode:
        x_ref, g_ref, gf_ref, wg_ref, wu_ref, cw_ref, wd_ref, y_ref, tail_ref, h_scr, acc_scr, carry = refs
    else:
        x_ref, g_ref, gf_ref, wg_ref, wu_ref, cw_ref, wd_ref, p1_ref, p2_ref, y_ref, a_ref, h_scr, acc_scr = refs
    m = pl.program_id(0)
    f = pl.program_id(1)
    nf = pl.num_programs(1)

    @pl.when(f == 0)
    def _():
        h_scr[...] = _rmsnorm(x_ref[...], g_ref[...]).astype(bf16)
        acc_scr[...] = jnp.zeros(acc_scr.shape, f32)

    hb = h_scr[...]
    a = _dot(hb, wg_ref[...])
    u = _dot(hb, wu_ref[...])
    row = _iota(a.shape, 0)
    a1 = pltpu.roll(a, 1, 0)
    a2 = pltpu.roll(a, 2, 0)
    if carry_mode:
        prev = carry[f]
        prev = jnp.where(m % (seq_len // tm) == 0, 0.0, prev)
        a1 = jnp.where(row == 0, prev[7:8, :], a1)
        a2 = jnp.where(row == 0, prev[6:7, :], jnp.where(row == 1, prev[7:8, :], a2))
        carry[f] = a[tm - 8:, :]
        tail_ref[0] = a[tm - 8:, :]
    else:
        lpos = row % seq_len
        a1 = jnp.where(lpos >= 1, a1, p1_ref[...])
        a2 = jnp.where(lpos >= 2, a2, p2_ref[...])
        a_ref[...] = a
    cw = cw_ref[...]
    cv = cw[0:1, :] * a2 + cw[1:2, :] * a1 + cw[2:3, :] * a
    gl = 0.5 * cv * (1.0 + lax.erf(cv * (0.5 ** 0.5)))
    acc_scr[...] += _dot((gl * u).astype(bf16), wd_ref[...])

    @pl.when(f == nf - 1)
    def _():
        y_ref[...] = _rmsnorm(x_ref[...] + acc_scr[...], gf_ref[...])


def _ffn(x1, g_ffn, g_final, w_gate, w_up, conv_w, w_down, *, tm, tf, seq_len, prev=None):
    M, D = x1.shape
    DF = w_gate.shape[1]
    nf = DF // tf
    carry_mode = prev is None
    in_specs = [
        pl.BlockSpec((tm, D), lambda m, f: (m, 0)),
        pl.BlockSpec((1, D), lambda m, f: (0, 0)),
        pl.BlockSpec((1, D), lambda m, f: (0, 0)),
        pl.BlockSpec((D, tf), lambda m, f: (0, f)),
        pl.BlockSpec((D, tf), lambda m, f: (0, f)),
        pl.BlockSpec((CONV_W, tf), lambda m, f: (0, f)),
        pl.BlockSpec((tf, D), lambda m, f: (f, 0)),
    ]
    args = [x1, g_ffn, g_final, w_gate, w_up, conv_w, w_down]
    scratch = [pltpu.VMEM((tm, D), bf16), pltpu.VMEM((tm, D), f32)]
    if carry_mode:
        assert seq_len % tm == 0
        out_specs = [pl.BlockSpec((tm, D), lambda m, f: (m, 0)), pl.BlockSpec((1, 8, tf), lambda m, f: (m, 0, f))]
        out_shape = [jax.ShapeDtypeStruct((M, D), f32), jax.ShapeDtypeStruct((M // tm, 8, DF), f32)]
        scratch.append(pltpu.VMEM((nf, 8, tf), f32))
    else:
        assert tm % seq_len == 0
        in_specs += [pl.BlockSpec((tm, tf), lambda m, f: (m, f)), pl.BlockSpec((tm, tf), lambda m, f: (m, f))]
        args += list(prev)
        out_specs = [pl.BlockSpec((tm, D), lambda m, f: (m, 0)), pl.BlockSpec((tm, tf), lambda m, f: (m, f))]
        out_shape = [jax.ShapeDtypeStruct((M, D), f32), jax.ShapeDtypeStruct((M, DF), f32)]
    return pl.pallas_call(
        functools.partial(_ffn_body, tm=tm, seq_len=seq_len, carry_mode=carry_mode),
        grid=(M // tm, nf),
        in_specs=in_specs,
        out_specs=out_specs,
        out_shape=out_shape,
        scratch_shapes=scratch,
        compiler_params=_params(("arbitrary", "arbitrary"), 52),
        name="convffn",
    )(*args)


def kernel(x_prompt, x_sample, cache_k, cache_v, state_C, state_n, state_m, state_conv, page_table,
           norm_mix, w_in, b_if, w_out, norm_ffn, w_gate, w_up, conv_w, w_down, norm_final):
    Bsz, T, D = x_prompt.shape
    DB, Ls, _ = x_sample.shape
    depth = w_in.shape[0]
    n_pages = page_table.shape[1]
    past = n_pages * PAGE_SIZE
    assert past % MOBA_BLOCK == 0 and past // MOBA_BLOCK >= 1, "sample path assumes a block-aligned, non-empty past"
    nbp = past // MOBA_BLOCK
    n_sel_s = min(MOBA_TOP_K, nbp)
    slopes = 2.0 ** (-8.0 * jnp.arange(1, H_ATT + 1, dtype=f32) / H_ATT)
    n_main = N_SEG * SEG
    TM_P, TM_S = 512, DB * Ls
    L_P, L_S = 256, 128

    xp = x_prompt.reshape(Bsz * T, D)
    xs = x_sample.reshape(DB * Ls, D)
    st_p, st_s = [], []
    for l in range(depth):
        w_main = w_in[l, :, :n_main].astype(bf16)
        w_if = jnp.pad(w_in[l, :, n_main:], ((0, 0), (0, LANES - 2 * H_MLSTM))).astype(bf16)
        bif = jnp.pad(b_if[l], (0, LANES - 2 * H_MLSTM)).reshape(1, LANES)
        wo = w_out[l].astype(bf16)
        wg, wu, wd = w_gate[l].astype(bf16), w_up[l].astype(bf16), w_down[l].astype(bf16)
        g_mix, g_ffn = norm_mix[l].reshape(1, D), norm_ffn[l].reshape(1, D)
        g_fin = norm_final.reshape(1, D) if l == depth - 1 else jnp.ones((1, D), f32)
        assert depth == 1, "final norm is fused into the last layer's FFN call"

        q, k, v, zm, gates, kb, vb, kmean = _inproj(xp, g_mix, w_main, w_if, bif, tm=TM_P, attn_aux=True)
        att = _moba_prompt(slopes, q.reshape(Bsz, T, SEG), kb.reshape(Bsz, T, SEG), vb.reshape(Bsz, T, SEG),
                           kmean.reshape(Bsz, T // MOBA_BLOCK, SEG))
        hm, c_p, n_p, m_p = _mlstm(
            zm.reshape(Bsz, T, 4 * SEG), gates.reshape(Bsz, T, LANES),
            jnp.zeros((Bsz, H_MLSTM, HEAD_DIM, HEAD_DIM), f32), jnp.zeros((Bsz, H_MLSTM, HEAD_DIM), f32),
            jnp.zeros((Bsz, H_MLSTM, LANES), f32), L=L_P, l_out=L_P)
        x1 = _outproj(xp, att.reshape(Bsz * T, SEG), hm.reshape(Bsz * T, SEG), wo, tm=TM_P, tn=1024)
        xp, tail = _ffn(x1, g_ffn, g_fin, wg, wu, conv_w[l], wd, tm=TM_P, tf=512, seq_len=T)
        conv_p = tail.reshape(Bsz, T // TM_P, 8, -1)[:, -1, 8 - (CONV_W - 1):, :]
        st_p.append((k.reshape(Bsz, T, H_ATT, HEAD_DIM), v.reshape(Bsz, T, H_ATT, HEAD_DIM), c_p, n_p, m_p[:, :, 0], conv_p))

        q, k, v, zm, gates = _inproj(xs, g_mix, w_main, w_if, bif, tm=TM_S, attn_aux=False)
        q8 = jnp.pad(q.reshape(DB, Ls, SEG), ((0, 0), (0, 8 - Ls), (0, 0)))
        q_rank = _qsel(xs, g_mix, w_in[l, :, :SEG])
        q_rank = jnp.pad(q_rank.reshape(DB, Ls, H_ATT, HEAD_DIM), ((0, 0), (0, 8 - Ls), (0, 0), (0, 0)))
        idx = _kmean_select_sample(page_table, cache_k, q_rank, layer=l, L=Ls, n_sel=n_sel_s)
        idx = idx[:, :Ls * n_sel_s, :, 0].reshape(DB, Ls, n_sel_s, H_ATT)
        idx_flat = jnp.transpose(idx, (0, 1, 3, 2)).reshape(-1)
        att8 = _sattn_sample(idx_flat, page_table, slopes, q8, k.reshape(DB, Ls, SEG), v.reshape(DB, Ls, SEG),
                             cache_k, cache_v, layer=l, L=Ls, n_sel=n_sel_s, past=past)
        att = att8[:, :Ls].reshape(DB * Ls, SEG)
        pad_t = ((0, 0), (0, L_S - Ls), (0, 0))
        gate_pad = jnp.where(jnp.arange(LANES) < H_MLSTM, NEG, 0.0).astype(f32)
        gates_s = jnp.concatenate(
            [gates.reshape(DB, Ls, LANES), jnp.broadcast_to(gate_pad, (DB, L_S - Ls, LANES))], axis=1)
        hm8, c_s, n_s, m_s = _mlstm(
            jnp.pad(zm.reshape(DB, Ls, 4 * SEG), pad_t), gates_s, state_C[l], state_n[l],
            jnp.broadcast_to(state_m[l][:, :, None], (DB, H_MLSTM, LANES)), L=L_S, l_out=8)
        hm = hm8[:, :Ls].reshape(DB * Ls, SEG)
        x1 = _outproj(xs, att, hm, wo, tm=TM_S, tn=1024)
        buf = state_conv[l]
        zero = jnp.zeros_like(buf[:, 0])
        p1 = jnp.stack([buf[:, 1]] + [zero] * (Ls - 1), axis=1).reshape(DB * Ls, -1)
        p2 = jnp.stack([buf[:, 0], buf[:, 1]] + [zero] * (Ls - 2), axis=1).reshape(DB * Ls, -1)
        xs, a_full = _ffn(x1, g_ffn, g_fin, wg, wu, conv_w[l], wd, tm=TM_S, tf=512, seq_len=Ls, prev=(p1, p2))
        conv_s = a_full.reshape(DB, Ls, -1)[:, Ls - (CONV_W - 1):]
        st_s.append((k.reshape(DB, Ls, H_ATT, HEAD_DIM), v.reshape(DB, Ls, H_ATT, HEAD_DIM), c_s, n_s, m_s[:, :, 0], conv_s))

    k_p, v_p, C_p, n_p, m_p, conv_p = [jnp.stack(z) for z in zip(*st_p)]
    k_s, v_s, C_s, n_s, m_s, conv_s = [jnp.stack(z) for z in zip(*st_s)]
    return (xp.reshape(Bsz, T, D), xs.reshape(DB, Ls, D), k_p, v_p, C_p, n_p, m_p, conv_p,
            k_s, v_s, C_s, n_s, m_s, conv_s)
```

```python
import functools
from typing import NamedTuple

import jax
import jax.numpy as jnp
from jax import lax
from jax.experimental import pallas as pl
from jax.experimental.pallas import tpu as pltpu

f32, bf16, i32 = jnp.float32, jnp.bfloat16, jnp.int32

HEAD_DIM = 128
H_ATT = 8
H_MLSTM = 8
SEG = H_ATT * HEAD_DIM
N_SEG = 7
CONV_W = 3
MOBA_BLOCK = 256
MOBA_TOP_K = 3
PAGE_SIZE = 128
PAGES_PER_BLOCK = MOBA_BLOCK // PAGE_SIZE
EPS = 1e-6
NEG = -1e30
LANES = 128
MAX_BLOCKS = 32
MIB = 1 << 20


def _params(sem, vmem_mib):
    return pltpu.CompilerParams(dimension_semantics=sem, vmem_limit_bytes=vmem_mib * MIB)


def _iota(shape, dim, dtype=i32):
    return lax.broadcasted_iota(dtype, shape, dim)


def _dot(a, b):
    return jnp.dot(a, b, preferred_element_type=f32)


def _dot_nt(a, b):
    return lax.dot_general(a, b, (((1,), (1,)), ((), ())), preferred_element_type=f32)


def _dot_tn(a, b):
    return lax.dot_general(a, b, (((0,), (0,)), ((), ())), preferred_element_type=f32)


def _rmsnorm(x, g):
    return x * lax.rsqrt(jnp.mean(x * x, axis=-1, keepdims=True) + EPS) * g


def _inproj_body(x_ref, g_ref, w_ref, wif_ref, bif_ref, q_ref, k_ref, v_ref, zm_ref, gate_ref, *rest, attn_aux):
    if attn_aux:
        kb_ref, vb_ref, kmean_ref, h_scr = rest
    else:
        (h_scr,) = rest
    n = pl.program_id(1)
    scale = HEAD_DIM ** -0.5

    @pl.when(n == 0)
    def _():
        hb = _rmsnorm(x_ref[...], g_ref[...]).astype(bf16)
        h_scr[...] = hb
        zg = _dot(hb, wif_ref[...]) + bif_ref[...]
        lane = _iota(zg.shape, 1)
        log_sig = jnp.minimum(zg, 0.0) - jnp.log1p(jnp.exp(-jnp.abs(zg)))
        gate_ref[...] = jnp.where(lane < H_MLSTM, zg, jnp.where(lane < 2 * H_MLSTM, log_sig, 0.0))

    z = _dot(h_scr[...], w_ref[...])

    @pl.when(n == 0)
    def _():
        q_ref[...] = (z * scale).astype(bf16)

    @pl.when(n == 1)
    def _():
        k_ref[...] = z
        if attn_aux:
            kb_ref[...] = z.astype(bf16)
            for i in range(z.shape[0] // MOBA_BLOCK):
                blk = z[i * MOBA_BLOCK:(i + 1) * MOBA_BLOCK, :]
                kmean_ref[0, i:i + 1, :] = jnp.sum(blk, axis=0, keepdims=True) * (1.0 / MOBA_BLOCK)

    @pl.when(n == 2)
    def _():
        v_ref[...] = z
        if attn_aux:
            vb_ref[...] = z.astype(bf16)

    @pl.when(n == 4)
    def _():
        zm_ref[...] = (z * scale).astype(bf16)

    @pl.when((n == 3) | (n >= 5))
    def _():
        zm_ref[...] = z.astype(bf16)


def _inproj(x, g, w_main, w_if, b_if, *, tm, attn_aux):
    M, D = x.shape
    grid = (M // tm, N_SEG)

    def row(m, n):
        return (m, 0)

    out_specs = [
        pl.BlockSpec((tm, SEG), row),
        pl.BlockSpec((tm, SEG), row),
        pl.BlockSpec((tm, SEG), row),
        pl.BlockSpec((tm, SEG), lambda m, n: (m, jnp.maximum(n - 3, 0))),
        pl.BlockSpec((tm, LANES), row),
    ]
    out_shape = [
        jax.ShapeDtypeStruct((M, SEG), bf16),
        jax.ShapeDtypeStruct((M, SEG), f32),
        jax.ShapeDtypeStruct((M, SEG), f32),
        jax.ShapeDtypeStruct((M, 4 * SEG), bf16),
        jax.ShapeDtypeStruct((M, LANES), f32),
    ]
    if attn_aux:
        assert tm % MOBA_BLOCK == 0
        bpt = tm // MOBA_BLOCK
        out_specs += [pl.BlockSpec((tm, SEG), row), pl.BlockSpec((tm, SEG), row),
                      pl.BlockSpec((1, bpt, SEG), lambda m, n: (m, 0, 0))]
        out_shape += [jax.ShapeDtypeStruct((M, SEG), bf16), jax.ShapeDtypeStruct((M, SEG), bf16),
                      jax.ShapeDtypeStruct((M // tm, bpt, SEG), f32)]
    return pl.pallas_call(
        functools.partial(_inproj_body, attn_aux=attn_aux),
        grid=grid,
        in_specs=[
            pl.BlockSpec((tm, D), row),
            pl.BlockSpec((1, D), lambda m, n: (0, 0)),
            pl.BlockSpec((D, SEG), lambda m, n: (0, n)),
            pl.BlockSpec((D, LANES), lambda m, n: (0, 0)),
            pl.BlockSpec((1, LANES), lambda m, n: (0, 0)),
        ],
        out_specs=out_specs,
        out_shape=out_shape,
        scratch_shapes=[pltpu.VMEM((tm, D), bf16)],
        compiler_params=_params(("arbitrary", "arbitrary"), 52),
        name="inproj",
    )(x, g, w_main, w_if, b_if)


def _top_blocks_mask(scores_t, n_valid, own, n_sel):
    R = scores_t.shape[0]
    rown = _iota(scores_t.shape, 0)
    s = jnp.where(rown < n_valid, scores_t, -jnp.inf)
    mask = jnp.where(rown == own, 0.0, NEG)
    for _ in range(n_sel):
        mx = jnp.max(s, axis=0, keepdims=True)
        idx = jnp.min(jnp.where(s == mx, rown, R), axis=0, keepdims=True)
        pick = (rown == idx) & (mx > -jnp.inf)
        mask = jnp.where(pick, 0.0, mask)
        s = jnp.where(pick, -jnp.inf, s)
    return mask


ATT_HEADS = 2
ATT_CHUNK = 4


def _moba_prompt_body(slopes_ref, q_ref, kb_ref, vb_ref, kmean_ref, o_ref, kaug, vt, qaug, acc_scr, s_scr, ml_scr,
                      *, nb):
    hp = pl.program_id(1)
    qi = pl.program_id(2)
    B = MOBA_BLOCK
    CK = ATT_CHUNK * B
    heads = [slice(hh * HEAD_DIM, (hh + 1) * HEAD_DIM) for hh in range(ATT_HEADS)]

    @pl.when(qi == 0)
    def _init():
        lane = _iota((B, LANES), 1)
        s_lo = _iota((B, LANES), 0).astype(f32)
        for hh, cs in enumerate(heads):
            slope = slopes_ref[hp * ATT_HEADS + hh]
            for n in range(nb):
                c, r0 = n // ATT_CHUNK, (n % ATT_CHUNK) * B
                kaug[hh, c, r0:r0 + B, 0:HEAD_DIM] = kb_ref[0, n * B:(n + 1) * B, cs]
                ext = jnp.where(lane == n, 1.0, 0.0)
                ext = jnp.where(lane == MAX_BLOCKS, -slope, ext)
                ext = jnp.where(lane == MAX_BLOCKS + 1, slope * s_lo, ext)
                ext = jnp.where(lane == MAX_BLOCKS + 2, -slope * B, ext)
                ext = jnp.where(lane == MAX_BLOCKS + 3, slope * (B * n), ext)
                kaug[hh, c, r0:r0 + B, HEAD_DIM:2 * HEAD_DIM] = ext.astype(bf16)
                vt[hh, c, :, r0:r0 + B] = vb_ref[0, n * B:(n + 1) * B, cs].astype(f32).T.astype(bf16)

    c_own = qi // ATT_CHUNK
    n_ext = HEAD_DIM - MAX_BLOCKS
    rr = _iota((n_ext, B), 0)
    t_lo = _iota((n_ext, B), 1).astype(f32)
    q_ext = jnp.where(rr == 0, t_lo, jnp.where((rr == 1) | (rr == 3), 1.0, 0.0))
    q_ext = jnp.where(rr == 2, qi.astype(f32), q_ext).astype(bf16)
    for hh, cs in enumerate(heads):
        q_t = q_ref[0, :, cs].astype(f32).T.astype(bf16)
        scores_t = _dot(kmean_ref[0, :, cs].astype(bf16), q_t)
        mask_t = _top_blocks_mask(scores_t, qi, qi, min(MOBA_TOP_K, nb))
        if nb < MAX_BLOCKS:
            mask_t = jnp.concatenate([mask_t, jnp.full((MAX_BLOCKS - nb, B), NEG, f32)], axis=0)
        qaug[hh, 0:HEAD_DIM, :] = q_t
        qaug[hh, HEAD_DIM:HEAD_DIM + MAX_BLOCKS, :] = mask_t.astype(bf16)
        qaug[hh, HEAD_DIM + MAX_BLOCKS:, :] = q_ext
        s_scr[hh, 0] = _dot(kaug[hh, 0], qaug[hh])
        ml_scr[2 * hh:2 * hh + 1, :] = jnp.full((1, B), NEG, f32)
        ml_scr[2 * hh + 1:2 * hh + 2, :] = jnp.zeros((1, B), f32)
        acc_scr[hh] = jnp.zeros((HEAD_DIM, B), f32)

    def softmax_pv(hh, c, s):
        m = ml_scr[2 * hh:2 * hh + 1, :]
        m_new = jnp.maximum(m, jnp.max(s, axis=0, keepdims=True))
        alpha = jnp.exp(m - m_new)
        p = jnp.exp(s - m_new)
        ml_scr[2 * hh:2 * hh + 1, :] = m_new
        ml_scr[2 * hh + 1:2 * hh + 2, :] = alpha * ml_scr[2 * hh + 1:2 * hh + 2, :] + jnp.sum(p, axis=0, keepdims=True)
        acc_scr[hh] = alpha * acc_scr[hh] + _dot(vt[hh, c], p.astype(bf16))

    def stage(c, cur, nxt):
        for hh in range(ATT_HEADS):
            s_scr[hh, nxt] = _dot(kaug[hh, c + 1], qaug[hh])
        for hh in range(ATT_HEADS):
            softmax_pv(hh, c, s_scr[hh, cur])

    def body(c, carry):
        @pl.when(c % 2 == 0)
        def _():
            stage(c, 0, 1)

        @pl.when(c % 2 == 1)
        def _():
            stage(c, 1, 0)

        return carry

    lax.fori_loop(0, c_own, body, 0)
    causal = (_iota((CK, B), 0) - _iota((CK, B), 1)) <= (qi - c_own * ATT_CHUNK) * B
    for hh, cs in enumerate(heads):
        softmax_pv(hh, c_own, jnp.where(causal, s_scr[hh, c_own % 2], NEG))
        o_ref[0, :, cs] = (acc_scr[hh] / ml_scr[2 * hh + 1:2 * hh + 2, :]).T.astype(bf16)


def _moba_prompt(slopes, q, kb, vb, kmean):
    Bsz, T, _ = q.shape
    assert T % (MOBA_BLOCK * ATT_CHUNK) == 0 and H_ATT % ATT_HEADS == 0
    nb = T // MOBA_BLOCK
    assert nb <= MAX_BLOCKS
    blk = MOBA_BLOCK
    W = ATT_HEADS * HEAD_DIM
    nc = nb // ATT_CHUNK
    return pl.pallas_call(
        functools.partial(_moba_prompt_body, nb=nb),
        grid=(Bsz, H_ATT // ATT_HEADS, nb),
        in_specs=[
            pl.BlockSpec(memory_space=pltpu.SMEM),
            pl.BlockSpec((1, blk, W), lambda b, h, i: (b, i, h)),
            pl.BlockSpec((1, T, W), lambda b, h, i: (b, 0, h)),
            pl.BlockSpec((1, T, W), lambda b, h, i: (b, 0, h)),
            pl.BlockSpec((1, nb, W), lambda b, h, i: (b, 0, h)),
        ],
        out_specs=pl.BlockSpec((1, blk, W), lambda b, h, i: (b, i, h)),
        out_shape=jax.ShapeDtypeStruct((Bsz, T, H_ATT * HEAD_DIM), bf16),
        scratch_shapes=[
            pltpu.VMEM((ATT_HEADS, nc, ATT_CHUNK * blk, 2 * HEAD_DIM), bf16),
            pltpu.VMEM((ATT_HEADS, nc, HEAD_DIM, ATT_CHUNK * blk), bf16),
            pltpu.VMEM((ATT_HEADS, 2 * HEAD_DIM, blk), bf16),
            pltpu.VMEM((ATT_HEADS, HEAD_DIM, blk), f32),
            pltpu.VMEM((ATT_HEADS, 2, ATT_CHUNK * blk, blk), f32),
            pltpu.VMEM((2 * ATT_HEADS, blk), f32),
        ],
        compiler_params=_params(("arbitrary", "arbitrary", "arbitrary"), 52),
        name="moba_prompt",
    )(slopes, q, kb, vb, kmean)


FFN_STREAM_BLOCKS = 3


class _KStream(NamedTuple):
    page_table: jax.Array
    cache_k: jax.Array
    layer: int
    base: int
    bps: int
    total: int


def _kstream_specs(ks, step_of):
    nbp = ks.page_table.shape[1] // PAGES_PER_BLOCK

    def page_map(*ids, i):
        g = jnp.minimum(ks.base + step_of(*ids[:-1]) * ks.bps + i // PAGES_PER_BLOCK, ks.total - 1)
        return (ks.layer, ids[-1][g // nbp, (g % nbp) * PAGES_PER_BLOCK + i % PAGES_PER_BLOCK], 0, 0, 0)

    in_specs = [pl.BlockSpec((None, None, PAGE_SIZE, H_ATT, HEAD_DIM), functools.partial(page_map, i=i))
                for i in range(ks.bps * PAGES_PER_BLOCK)]
    out_spec = pl.BlockSpec((ks.bps, H_ATT, HEAD_DIM), lambda *ids: (step_of(*ids[:-1]), 0, 0))
    return in_specs, out_spec


def _kstream_reduce(pages, out_ref):
    for b in range(len(pages) // PAGES_PER_BLOCK):
        s = jnp.sum(pages[PAGES_PER_BLOCK * b][...], axis=0)
        for p in range(1, PAGES_PER_BLOCK):
            s = s + jnp.sum(pages[PAGES_PER_BLOCK * b + p][...], axis=0)
        out_ref[b] = s * (1.0 / MOBA_BLOCK)


def _qsel_body(x_ref, g_ref, w_ref, o_ref):
    h = _rmsnorm(x_ref[...], g_ref[...])
    o_ref[...] = jnp.dot(h, w_ref[...], preferred_element_type=f32, precision=lax.Precision.HIGHEST)


def _qsel(x, g, w_q, *, tn=256):
    M, D = x.shape
    N = w_q.shape[1]
    return pl.pallas_call(
        _qsel_body,
        grid=(N // tn,),
        in_specs=[
            pl.BlockSpec((M, D), lambda n: (0, 0)),
            pl.BlockSpec((1, D), lambda n: (0, 0)),
            pl.BlockSpec((D, tn), lambda n: (0, n)),
        ],
        out_specs=pl.BlockSpec((M, tn), lambda n: (0, n)),
        out_shape=jax.ShapeDtypeStruct((M, N), f32),
        compiler_params=_params(("arbitrary",), 32),
        name="qsel",
    )(x, g, w_q)


IDX_ROWS = 16


def _select_body(km_ref, q_ref, idx_ref, *, L, n_sel):
    nbp = km_ref.shape[0]
    idx_ref[...] = jnp.zeros(idx_ref.shape, i32)
    km = km_ref[...]
    blk = _iota((nbp, H_ATT, 1), 0)
    for l in range(L):
        sc = jnp.sum(km * q_ref[l][None], axis=-1, keepdims=True)
        for r in range(n_sel):
            mx = jnp.max(sc, axis=0, keepdims=True)
            idx = jnp.min(jnp.where(sc == mx, blk, nbp), axis=0, keepdims=True)
            idx_ref[l * n_sel + r] = jnp.broadcast_to(idx[0], (H_ATT, LANES))
            sc = jnp.where(blk == idx, -jnp.inf, sc)


def _select_sample(kmean, q8, *, L, n_sel):
    DB, nbp = kmean.shape[:2]
    assert L * n_sel <= IDX_ROWS
    return pl.pallas_call(
        functools.partial(_select_body, L=L, n_sel=n_sel),
        grid=(DB,),
        in_specs=[
            pl.BlockSpec((None, nbp, H_ATT, HEAD_DIM), lambda d: (d, 0, 0, 0)),
            pl.BlockSpec((None, 8, H_ATT, HEAD_DIM), lambda d: (d, 0, 0, 0)),
        ],
        out_specs=pl.BlockSpec((None, IDX_ROWS, H_ATT, LANES), lambda d: (d, 0, 0, 0)),
        out_shape=jax.ShapeDtypeStruct((DB, IDX_ROWS, H_ATT, LANES), i32),
        compiler_params=_params(("arbitrary",), 32),
        name="select_sample",
    )(kmean, q8)


def _sattn_body(idx_ref, pt_ref, slopes_ref, q_ref, kn_ref, vn_ref, ck_hbm, cv_hbm, o_ref, kbuf, vbuf, sem,
                *, layer, L, n_sel, past):
    nblk = L * n_sel * PAGES_PER_BLOCK
    d = pl.program_id(0)
    h = pl.program_id(1)
    step = d * H_ATT + h
    slot = step % 2

    def page_copies(dd, hh, sl):
        out = []
        for i in range(nblk):
            lj, p = i // PAGES_PER_BLOCK, i % PAGES_PER_BLOCK
            blk = idx_ref[((dd * L + lj // n_sel) * H_ATT + hh) * n_sel + lj % n_sel]
            phys = pt_ref[dd, blk * PAGES_PER_BLOCK + p]
            out.append(pltpu.make_async_copy(ck_hbm.at[layer, phys, :, hh, :], kbuf.at[sl, i], sem.at[sl]))
            out.append(pltpu.make_async_copy(cv_hbm.at[layer, phys, :, hh, :], vbuf.at[sl, i], sem.at[sl]))
        return out

    @pl.when(step == 0)
    def _():
        for cp in page_copies(d, h, slot):
            cp.start()

    @pl.when(step + 1 < pl.num_programs(0) * H_ATT)
    def _():
        nxt = step + 1
        for cp in page_copies(nxt // H_ATT, nxt % H_ATT, 1 - slot):
            cp.start()

    for cp in page_copies(d, h, slot):
        cp.wait()

    kp = [kbuf.at[slot, i] for i in range(nblk)]
    vp = [vbuf.at[slot, i] for i in range(nblk)]
    slope = slopes_ref[h]
    qf = q_ref[0].astype(f32)
    kn = kn_ref[0]
    vn = vn_ref[0]
    lpos = _iota((L, 1), 0)
    ppos = _iota((PAGE_SIZE, 1), 0)
    rows = []
    for l in range(L):
        ql = qf[l:l + 1, :]
        lo = jnp.sum(kn * ql, axis=1, keepdims=True)
        lo = jnp.where(lpos <= l, lo - slope * (l - lpos).astype(f32), NEG)
        logits = []
        for j in range(n_sel):
            blk = idx_ref[((d * L + l) * H_ATT + h) * n_sel + j]
            for p in range(PAGES_PER_BLOCK):
                kb = kp[(l * n_sel + j) * PAGES_PER_BLOCK + p][...]
                dist = (past + l - blk * MOBA_BLOCK - p * PAGE_SIZE) - ppos
                logits.append(jnp.sum(kb * ql, axis=1, keepdims=True) - slope * dist.astype(f32))
        m = jnp.max(lo, axis=0, keepdims=True)
        for lg in logits:
            m = jnp.maximum(m, jnp.max(lg, axis=0, keepdims=True))
        p_own = jnp.exp(lo - m)
        den = jnp.sum(p_own, axis=0, keepdims=True)
        num = jnp.sum(p_own * vn, axis=0, keepdims=True)
        for i, lg in enumerate(logits):
            pi = jnp.exp(lg - m)
            den = den + jnp.sum(pi, axis=0, keepdims=True)
            num = num + jnp.sum(pi * vp[l * n_sel * PAGES_PER_BLOCK + i][...], axis=0, keepdims=True)
        rows.append(num / den)
    rows.append(jnp.zeros((8 - L, HEAD_DIM), f32))
    o_ref[0] = jnp.concatenate(rows, axis=0).astype(bf16)


def _sattn_sample(idx_flat, page_table, slopes, q8, k_new, v_new, cache_k, cache_v, *, layer, L, n_sel, past):
    DB = q8.shape[0]
    assert L <= 8
    nblk = L * n_sel * PAGES_PER_BLOCK
    return pl.pallas_call(
        functools.partial(_sattn_body, layer=layer, L=L, n_sel=n_sel, past=past),
        grid_spec=pltpu.PrefetchScalarGridSpec(
            num_scalar_prefetch=2,
            grid=(DB, H_ATT),
            in_specs=[
                pl.BlockSpec(memory_space=pltpu.SMEM),
                pl.BlockSpec((1, 8, HEAD_DIM), lambda d, h, idx, pt: (d, 0, h)),
                pl.BlockSpec((1, L, HEAD_DIM), lambda d, h, idx, pt: (d, 0, h)),
                pl.BlockSpec((1, L, HEAD_DIM), lambda d, h, idx, pt: (d, 0, h)),
                pl.BlockSpec(memory_space=pl.ANY),
                pl.BlockSpec(memory_space=pl.ANY),
            ],
            out_specs=pl.BlockSpec((1, 8, HEAD_DIM), lambda d, h, idx, pt: (d, 0, h)),
            scratch_shapes=[
                pltpu.VMEM((2, nblk, PAGE_SIZE, HEAD_DIM), f32),
                pltpu.VMEM((2, nblk, PAGE_SIZE, HEAD_DIM), f32),
                pltpu.SemaphoreType.DMA((2,)),
            ],
        ),
        out_shape=jax.ShapeDtypeStruct((DB, 8, H_ATT * HEAD_DIM), bf16),
        compiler_params=_params(("arbitrary", "arbitrary"), 32),
        name="sattn_sample",
    )(idx_flat, page_table, slopes, q8, k_new, v_new, cache_k, cache_v)


def _split3_dot(a_bf16, x):
    x1 = x.astype(bf16)
    r1 = x - x1.astype(f32)
    x2 = r1.astype(bf16)
    x3 = (r1 - x2.astype(f32)).astype(bf16)
    return _dot(a_bf16, x1) + _dot(a_bf16, x2) + _dot(a_bf16, x3)


def _mlstm_body(*refs, L, l_out, n_stream):
    if n_stream:
        refs = refs[1:]
        _kstream_reduce(refs[5:5 + n_stream], refs[5 + n_stream + 4])
        refs = refs[:5] + refs[5 + n_stream:5 + n_stream + 4] + refs[5 + n_stream + 5:]
    zm_ref, gate_ref, c0_ref, n0_ref, m0_ref, hm_ref, c_out, n_out, m_out, c_scr, n_scr, m_scr = refs
    c = pl.program_id(1)
    nc = pl.num_programs(1)

    @pl.when(c == 0)
    def _():
        c_scr[...] = c0_ref[0]
        n_scr[...] = n0_ref[0]
        m_scr[...] = m0_ref[0]

    G = gate_ref[0]
    row = _iota((L, L), 0)
    col = _iota((L, L), 1)
    causal = row >= col
    csum = _split3_dot(jnp.where(causal, 1.0, 0.0).astype(bf16), G)
    lane = _iota((L, LANES), 1)
    X = jnp.where(lane < H_MLSTM, pltpu.roll(csum, LANES - H_MLSTM, 1), pltpu.roll(G, H_MLSTM, 1) - csum)
    XT = X.T
    for h in range(H_MLSTM):
        sl = slice(h * HEAD_DIM, (h + 1) * HEAD_DIM)
        q = zm_ref[0, :, sl]
        k = zm_ref[0, :, SEG + h * HEAD_DIM:SEG + (h + 1) * HEAD_DIM]
        v = zm_ref[0, :, 2 * SEG + h * HEAD_DIM:2 * SEG + (h + 1) * HEAD_DIM]
        o = zm_ref[0, :, 3 * SEG + h * HEAD_DIM:3 * SEG + (h + 1) * HEAD_DIM]
        b_col = X[:, h:h + 1]
        g_col = X[:, H_MLSTM + h:H_MLSTM + h + 1]
        g_row = XT[H_MLSTM + h:H_MLSTM + h + 1, :]
        m_prev = m_scr[h:h + 1, 0:1]
        log_d = jnp.where(causal, b_col + g_row, -jnp.inf)
        m_inter = b_col + m_prev
        m_t = jnp.maximum(m_inter, jnp.max(log_d, axis=1, keepdims=True))
        s = _dot_nt(q, k) * jnp.exp(log_d - m_t)
        a_inter = jnp.exp(m_inter - m_t)
        c_h = c_scr[h]
        n_h = n_scr[h:h + 1, :]
        num = _dot(s.astype(bf16), v) + a_inter * _dot_nt(q, c_h.astype(bf16))
        den = jnp.sum(s, axis=1, keepdims=True) + a_inter * jnp.sum(q.astype(f32) * n_h, axis=1, keepdims=True)
        hh = num / jnp.maximum(jnp.abs(den), jnp.exp(-m_t))
        out = (jax.nn.sigmoid(o.astype(f32)) * hh).astype(bf16)
        hm_ref[0, :, sl] = out[:l_out]
        m_new = m_t[L - 1:L, :]
        b_last = b_col[L - 1:L, :]
        decay = jnp.exp(b_last + m_prev - m_new)
        w = jnp.exp(b_last + g_col - m_new)
        wk = w * k.astype(f32)
        c_scr[h] = decay * c_h + _dot_tn(v, wk.astype(bf16))
        n_scr[h:h + 1, :] = decay * n_h + jnp.sum(wk, axis=0, keepdims=True)
        m_scr[h:h + 1, :] = jnp.broadcast_to(m_new, (1, LANES))

    @pl.when(c == nc - 1)
    def _():
        c_out[0] = c_scr[...]
        n_out[0] = n_scr[...]
        m_out[0] = m_scr[...]


def _mlstm(zm, gates, c0, n0, m0, *, L, l_out, kstream=None):
    Bsz, T, _ = zm.shape
    nc = T // L
    in_specs = [
        pl.BlockSpec((1, L, 4 * SEG), lambda b, c, *_: (b, c, 0)),
        pl.BlockSpec((1, L, LANES), lambda b, c, *_: (b, c, 0)),
        pl.BlockSpec((1, H_MLSTM, HEAD_DIM, HEAD_DIM), lambda b, c, *_: (b, 0, 0, 0)),
        pl.BlockSpec((1, H_MLSTM, HEAD_DIM), lambda b, c, *_: (b, 0, 0)),
        pl.BlockSpec((1, H_MLSTM, LANES), lambda b, c, *_: (b, 0, 0)),
    ]
    out_specs = [
        pl.BlockSpec((1, l_out, SEG), lambda b, c, *_: (b, c, 0)),
        pl.BlockSpec((1, H_MLSTM, HEAD_DIM, HEAD_DIM), lambda b, c, *_: (b, 0, 0, 0)),
        pl.BlockSpec((1, H_MLSTM, HEAD_DIM), lambda b, c, *_: (b, 0, 0)),
        pl.BlockSpec((1, H_MLSTM, LANES), lambda b, c, *_: (b, 0, 0)),
    ]
    out_shape = [
        jax.ShapeDtypeStruct((Bsz, nc * l_out, SEG), bf16),
        jax.ShapeDtypeStruct((Bsz, H_MLSTM, HEAD_DIM, HEAD_DIM), f32),
        jax.ShapeDtypeStruct((Bsz, H_MLSTM, HEAD_DIM), f32),
        jax.ShapeDtypeStruct((Bsz, H_MLSTM, LANES), f32),
    ]
    scratch = [
        pltpu.VMEM((H_MLSTM, HEAD_DIM, HEAD_DIM), f32),
        pltpu.VMEM((H_MLSTM, HEAD_DIM), f32),
        pltpu.VMEM((H_MLSTM, LANES), f32),
    ]
    args = [zm, gates, c0, n0, m0]
    n_stream = 0
    if kstream is not None:
        page_specs, km_spec = _kstream_specs(kstream, lambda b, c: b * nc + c)
        n_stream = len(page_specs)
        in_specs += page_specs
        out_specs.append(km_spec)
        out_shape.append(jax.ShapeDtypeStruct((Bsz * nc * kstream.bps, H_ATT, HEAD_DIM), f32))
        args = [kstream.page_table] + args + [kstream.cache_k] * n_stream
    return pl.pallas_call(
        functools.partial(_mlstm_body, L=L, l_out=l_out, n_stream=n_stream),
        grid_spec=pltpu.PrefetchScalarGridSpec(
            num_scalar_prefetch=1 if n_stream else 0,
            grid=(Bsz, nc),
            in_specs=in_specs,
            out_specs=out_specs,
            scratch_shapes=scratch,
        ),
        out_shape=out_shape,
        compiler_params=_params(("arbitrary", "arbitrary"), 16 + n_stream),
        name="mlstm",
    )(*args)


def _outproj_body(x_ref, att_ref, hm_ref, wa_ref, wm_ref, o_ref):
    o_ref[...] = x_ref[...] + _dot(att_ref[...], wa_ref[...]) + _dot(hm_ref[...], wm_ref[...])


def _outproj(x, att, hm, w_out, *, tm, tn):
    M, D = x.shape
    return pl.pallas_call(
        _outproj_body,
        grid=(M // tm, D // tn),
        in_specs=[
            pl.BlockSpec((tm, tn), lambda m, n: (m, n)),
            pl.BlockSpec((tm, SEG), lambda m, n: (m, 0)),
            pl.BlockSpec((tm, SEG), lambda m, n: (m, 0)),
            pl.BlockSpec((SEG, tn), lambda m, n: (0, n)),
            pl.BlockSpec((SEG, tn), lambda m, n: (1, n)),
        ],
        out_specs=pl.BlockSpec((tm, tn), lambda m, n: (m, n)),
        out_shape=jax.ShapeDtypeStruct((M, D), f32),
        compiler_params=_params(("arbitrary", "arbitrary"), 40),
        name="outproj",
    )(x, att, hm, w_out, w_out)


def _ffn_body(*refs, tm, seq_len, carry_mode, n_stream):
    if n_stream:
        refs = refs[1:]
        _kstream_reduce(refs[7:7 + n_stream], refs[7 + n_stream + 2])
        refs = refs[:7] + refs[7 + n_stream:7 + n_stream + 2] + refs[7 + n_stream + 3:]
    if carry_mode:
        x_ref, g_ref, gf_ref, wg_ref, wu_ref, cw_ref, wd_ref, y_ref, tail_ref, h_scr, acc_scr, carry = refs
    else:
        x_ref, g_ref, gf_ref, wg_ref, wu_ref, cw_ref, wd_ref, p1_ref, p2_ref, y_ref, a_ref, h_scr, acc_scr = refs
    m = pl.program_id(0)
    f = pl.program_id(1)
    nf = pl.num_programs(1)

    @pl.when(f == 0)
    def _():
        h_scr[...] = _rmsnorm(x_ref[...], g_ref[...]).astype(bf16)
        acc_scr[...] = jnp.zeros(acc_scr.shape, f32)

    hb = h_scr[...]
    a = _dot(hb, wg_ref[...])
    u = _dot(hb, wu_ref[...])
    row = _iota(a.shape, 0)
    a1 = pltpu.roll(a, 1, 0)
    a2 = pltpu.roll(a, 2, 0)
    if carry_mode:
        prev = carry[f]
        prev = jnp.where(m % (seq_len // tm) == 0, 0.0, prev)
        a1 = jnp.where(row == 0, prev[7:8, :], a1)
        a2 = jnp.where(row == 0, prev[6:7, :], jnp.where(row == 1, prev[7:8, :], a2))
        carry[f] = a[tm - 8:, :]
        tail_ref[0] = a[tm - 8:, :]
    else:
        lpos = row % seq_len
        a1 = jnp.where(lpos >= 1, a1, p1_ref[...])
        a2 = jnp.where(lpos >= 2, a2, p2_ref[...])
        a_ref[...] = a
    cw = cw_ref[...]
    cv = cw[0:1, :] * a2 + cw[1:2, :] * a1 + cw[2:3, :] * a
    gl = 0.5 * cv * (1.0 + lax.erf(cv * (0.5 ** 0.5)))
    acc_scr[...] += _dot((gl * u).astype(bf16), wd_ref[...])

    @pl.when(f == nf - 1)
    def _():
        y_ref[...] = _rmsnorm(x_ref[...] + acc_scr[...], gf_ref[...])


def _ffn(x1, g_ffn, g_final, w_gate, w_up, conv_w, w_down, *, tm, tf, seq_len, prev=None, kstream=None):
    M, D = x1.shape
    DF = w_gate.shape[1]
    nf = DF // tf
    carry_mode = prev is None
    in_specs = [
        pl.BlockSpec((tm, D), lambda m, f, *_: (m, 0)),
        pl.BlockSpec((1, D), lambda m, f, *_: (0, 0)),
        pl.BlockSpec((1, D), lambda m, f, *_: (0, 0)),
        pl.BlockSpec((D, tf), lambda m, f, *_: (0, f)),
        pl.BlockSpec((D, tf), lambda m, f, *_: (0, f)),
        pl.BlockSpec((CONV_W, tf), lambda m, f, *_: (0, f)),
        pl.BlockSpec((tf, D), lambda m, f, *_: (f, 0)),
    ]
    args = [x1, g_ffn, g_final, w_gate, w_up, conv_w, w_down]
    scratch = [pltpu.VMEM((tm, D), bf16), pltpu.VMEM((tm, D), f32)]
    if carry_mode:
        assert seq_len % tm == 0
        out_specs = [pl.BlockSpec((tm, D), lambda m, f, *_: (m, 0)), pl.BlockSpec((1, 8, tf), lambda m, f, *_: (m, 0, f))]
        out_shape = [jax.ShapeDtypeStruct((M, D), f32), jax.ShapeDtypeStruct((M // tm, 8, DF), f32)]
        scratch.append(pltpu.VMEM((nf, 8, tf), f32))
    else:
        assert tm % seq_len == 0
        in_specs += [pl.BlockSpec((tm, tf), lambda m, f, *_: (m, f)), pl.BlockSpec((tm, tf), lambda m, f, *_: (m, f))]
        args += list(prev)
        out_specs = [pl.BlockSpec((tm, D), lambda m, f, *_: (m, 0)), pl.BlockSpec((tm, tf), lambda m, f, *_: (m, f))]
        out_shape = [jax.ShapeDtypeStruct((M, D), f32), jax.ShapeDtypeStruct((M, DF), f32)]
    n_stream = 0
    if kstream is not None:
        assert carry_mode
        page_specs, km_spec = _kstream_specs(kstream, lambda m, f: m * nf + f)
        n_stream = len(page_specs)
        in_specs += page_specs
        out_specs.append(km_spec)
        out_shape.append(jax.ShapeDtypeStruct((M // tm * nf * kstream.bps, H_ATT, HEAD_DIM), f32))
        args = [kstream.page_table] + args + [kstream.cache_k] * n_stream
    return pl.pallas_call(
        functools.partial(_ffn_body, tm=tm, seq_len=seq_len, carry_mode=carry_mode, n_stream=n_stream),
        grid_spec=pltpu.PrefetchScalarGridSpec(
            num_scalar_prefetch=1 if n_stream else 0,
            grid=(M // tm, nf),
            in_specs=in_specs,
            out_specs=out_specs,
            scratch_shapes=scratch,
        ),
        out_shape=out_shape,
        compiler_params=_params(("arbitrary", "arbitrary"), 48 + n_stream),
        name="convffn",
    )(*args)


def kernel(x_prompt, x_sample, cache_k, cache_v, state_C, state_n, state_m, state_conv, page_table,
           norm_mix, w_in, b_if, w_out, norm_ffn, w_gate, w_up, conv_w, w_down, norm_final):
    Bsz, T, D = x_prompt.shape
    DB, Ls, _ = x_sample.shape
    depth = w_in.shape[0]
    n_pages = page_table.shape[1]
    past = n_pages * PAGE_SIZE
    assert past % MOBA_BLOCK == 0 and past // MOBA_BLOCK >= 1, "sample path assumes a block-aligned, non-empty past"
    nbp = past // MOBA_BLOCK
    n_sel_s = min(MOBA_TOP_K, nbp)
    slopes = 2.0 ** (-8.0 * jnp.arange(1, H_ATT + 1, dtype=f32) / H_ATT)
    n_main = N_SEG * SEG
    TM_P, TM_S = 512, DB * Ls
    L_P, L_S = 256, 128
    TF = 512

    xp = x_prompt.reshape(Bsz * T, D)
    xs = x_sample.reshape(DB * Ls, D)
    st_p, st_s = [], []
    for l in range(depth):
        w_main = w_in[l].astype(bf16)
        w_if = jnp.pad(w_in[l, :, n_main:], ((0, 0), (0, LANES - 2 * H_MLSTM))).astype(bf16)
        bif = jnp.pad(b_if[l], (0, LANES - 2 * H_MLSTM)).reshape(1, LANES)
        wo = w_out[l].astype(bf16)
        wg, wu, wd = w_gate[l].astype(bf16), w_up[l].astype(bf16), w_down[l].astype(bf16)
        g_mix, g_ffn = norm_mix[l].reshape(1, D), norm_ffn[l].reshape(1, D)
        g_fin = norm_final.reshape(1, D) if l == depth - 1 else jnp.ones((1, D), f32)
        assert depth == 1, "final norm is fused into the last layer's FFN call"

        q, k, v, zm, gates, kb, vb, kmean = _inproj(xp, g_mix, w_main, w_if, bif, tm=TM_P, attn_aux=True)
        att = _moba_prompt(slopes, q.reshape(Bsz, T, SEG), kb.reshape(Bsz, T, SEG), vb.reshape(Bsz, T, SEG),
                           kmean.reshape(Bsz, T // MOBA_BLOCK, SEG))
        n_blk = DB * nbp
        n_ffn = min(n_blk, (Bsz * T // TM_P) * (wg.shape[1] // TF) * FFN_STREAM_BLOCKS)
        ml_steps = Bsz * T // L_P
        ks_ffn = _KStream(page_table, cache_k, l, 0, FFN_STREAM_BLOCKS, n_blk)
        ks_ml = _KStream(page_table, cache_k, l, n_ffn, max(1, -(-(n_blk - n_ffn) // ml_steps)), n_blk)
        hm, c_p, n_p, m_p, km_ml = _mlstm(
            zm.reshape(Bsz, T, 4 * SEG), gates.reshape(Bsz, T, LANES),
            jnp.zeros((Bsz, H_MLSTM, HEAD_DIM, HEAD_DIM), f32), jnp.zeros((Bsz, H_MLSTM, HEAD_DIM), f32),
            jnp.zeros((Bsz, H_MLSTM, LANES), f32), L=L_P, l_out=L_P, kstream=ks_ml)
        x1 = _outproj(xp, att.reshape(Bsz * T, SEG), hm.reshape(Bsz * T, SEG), wo, tm=TM_P, tn=1024)
        xp, tail, km_ffn = _ffn(x1, g_ffn, g_fin, wg, wu, conv_w[l], wd, tm=TM_P, tf=TF, seq_len=T, kstream=ks_ffn)
        kmean_s = jnp.concatenate([km_ffn[:n_ffn], km_ml[:n_blk - n_ffn]], axis=0).reshape(DB, nbp, H_ATT, HEAD_DIM)
        conv_p = tail.reshape(Bsz, T // TM_P, 8, -1)[:, -1, 8 - (CONV_W - 1):, :]
        st_p.append((k.reshape(Bsz, T, H_ATT, HEAD_DIM), v.reshape(Bsz, T, H_ATT, HEAD_DIM), c_p, n_p, m_p[:, :, 0], conv_p))

        q, k, v, zm, gates = _inproj(xs, g_mix, w_main, w_if, bif, tm=TM_S, attn_aux=False)
        q8 = jnp.pad(q.reshape(DB, Ls, SEG), ((0, 0), (0, 8 - Ls), (0, 0)))
        q_rank = _qsel(xs, g_mix, w_in[l, :, :SEG])
        q_rank = jnp.pad(q_rank.reshape(DB, Ls, H_ATT, HEAD_DIM), ((0, 0), (0, 8 - Ls), (0, 0), (0, 0)))
        idx = _select_sample(kmean_s, q_rank, L=Ls, n_sel=n_sel_s)
        idx = idx[:, :Ls * n_sel_s, :, 0].reshape(DB, Ls, n_sel_s, H_ATT)
        idx_flat = jnp.transpose(idx, (0, 1, 3, 2)).reshape(-1)
        att8 = _sattn_sample(idx_flat, page_table, slopes, q8, k.reshape(DB, Ls, SEG), v.reshape(DB, Ls, SEG),
                             cache_k, cache_v, layer=l, L=Ls, n_sel=n_sel_s, past=past)
        att = att8[:, :Ls].reshape(DB * Ls, SEG)
        pad_t = ((0, 0), (0, L_S - Ls), (0, 0))
        gate_pad = jnp.where(jnp.arange(LANES) < H_MLSTM, NEG, 0.0).astype(f32)
        gates_s = jnp.concatenate(
            [gates.reshape(DB, Ls, LANES), jnp.broadcast_to(gate_pad, (DB, L_S - Ls, LANES))], axis=1)
        hm8, c_s, n_s, m_s = _mlstm(
            jnp.pad(zm.reshape(DB, Ls, 4 * SEG), pad_t), gates_s, state_C[l], state_n[l],
            jnp.broadcast_to(state_m[l][:, :, None], (DB, H_MLSTM, LANES)), L=L_S, l_out=8)
        hm = hm8[:, :Ls].reshape(DB * Ls, SEG)
        x1 = _outproj(xs, att, hm, wo, tm=TM_S, tn=1024)
        buf = state_conv[l]
        zero = jnp.zeros_like(buf[:, 0])
        p1 = jnp.stack([buf[:, 1]] + [zero] * (Ls - 1), axis=1).reshape(DB * Ls, -1)
        p2 = jnp.stack([buf[:, 0], buf[:, 1]] + [zero] * (Ls - 2), axis=1).reshape(DB * Ls, -1)
        xs, a_full = _ffn(x1, g_ffn, g_fin, wg, wu, conv_w[l], wd, tm=TM_S, tf=TF, seq_len=Ls, prev=(p1, p2))
        conv_s = a_full.reshape(DB, Ls, -1)[:, Ls - (CONV_W - 1):]
        st_s.append((k.reshape(DB, Ls, H_ATT, HEAD_DIM), v.reshape(DB, Ls, H_ATT, HEAD_DIM), c_s, n_s, m_s[:, :, 0], conv_s))

    k_p, v_p, C_p, n_p, m_p, conv_p = [jnp.stack(z) for z in zip(*st_p)]
    k_s, v_s, C_s, n_s, m_s, conv_s = [jnp.stack(z) for z in zip(*st_s)]
    return (xp.reshape(Bsz, T, D), xs.reshape(DB, Ls, D), k_p, v_p, C_p, n_p, m_p, conv_p,
            k_s, v_s, C_s, n_s, m_s, conv_s)
```

```python
import functools
from typing import NamedTuple

import jax
import jax.numpy as jnp
from jax import lax
from jax.experimental import pallas as pl
from jax.experimental.pallas import tpu as pltpu

f32, bf16, i32 = jnp.float32, jnp.bfloat16, jnp.int32

HEAD_DIM = 128
H_ATT = 8
H_MLSTM = 8
SEG = H_ATT * HEAD_DIM
N_SEG = 7
CONV_W = 3
MOBA_BLOCK = 256
MOBA_TOP_K = 3
PAGE_SIZE = 128
PAGES_PER_BLOCK = MOBA_BLOCK // PAGE_SIZE
EPS = 1e-6
NEG = -1e30
LANES = 128
MAX_BLOCKS = 32
MIB = 1 << 20


def _params(sem, vmem_mib):
    return pltpu.CompilerParams(dimension_semantics=sem, vmem_limit_bytes=vmem_mib * MIB)


def _iota(shape, dim, dtype=i32):
    return lax.broadcasted_iota(dtype, shape, dim)


def _dot(a, b):
    return jnp.dot(a, b, preferred_element_type=f32)


def _dot_nt(a, b):
    return lax.dot_general(a, b, (((1,), (1,)), ((), ())), preferred_element_type=f32)


def _dot_tn(a, b):
    return lax.dot_general(a, b, (((0,), (0,)), ((), ())), preferred_element_type=f32)


def _rmsnorm(x, g):
    return x * lax.rsqrt(jnp.mean(x * x, axis=-1, keepdims=True) + EPS) * g


def _inproj_body(x_ref, g_ref, w_ref, wif_ref, bif_ref, q_ref, k_ref, v_ref, zm_ref, gate_ref, *rest, attn_aux):
    if attn_aux:
        kb_ref, vb_ref, kmean_ref, h_scr = rest
    else:
        (h_scr,) = rest
    n = pl.program_id(1)
    scale = HEAD_DIM ** -0.5

    @pl.when(n == 0)
    def _():
        hb = _rmsnorm(x_ref[...], g_ref[...]).astype(bf16)
        h_scr[...] = hb
        zg = _dot(hb, wif_ref[...]) + bif_ref[...]
        lane = _iota(zg.shape, 1)
        log_sig = jnp.minimum(zg, 0.0) - jnp.log1p(jnp.exp(-jnp.abs(zg)))
        gate_ref[...] = jnp.where(lane < H_MLSTM, zg, jnp.where(lane < 2 * H_MLSTM, log_sig, 0.0))

    z = _dot(h_scr[...], w_ref[...])

    @pl.when(n == 0)
    def _():
        q_ref[...] = (z * scale).astype(bf16)

    @pl.when(n == 1)
    def _():
        k_ref[...] = z
        if attn_aux:
            kb_ref[...] = z.astype(bf16)
            for i in range(z.shape[0] // MOBA_BLOCK):
                blk = z[i * MOBA_BLOCK:(i + 1) * MOBA_BLOCK, :]
                kmean_ref[0, i:i + 1, :] = jnp.sum(blk, axis=0, keepdims=True) * (1.0 / MOBA_BLOCK)

    @pl.when(n == 2)
    def _():
        v_ref[...] = z
        if attn_aux:
            vb_ref[...] = z.astype(bf16)

    @pl.when(n == 4)
    def _():
        zm_ref[...] = (z * scale).astype(bf16)

    @pl.when((n == 3) | (n >= 5))
    def _():
        zm_ref[...] = z.astype(bf16)


def _inproj(x, g, w_main, w_if, b_if, *, tm, attn_aux):
    M, D = x.shape
    grid = (M // tm, N_SEG)

    def row(m, n):
        return (m, 0)

    out_specs = [
        pl.BlockSpec((tm, SEG), row),
        pl.BlockSpec((tm, SEG), row),
        pl.BlockSpec((tm, SEG), row),
        pl.BlockSpec((tm, SEG), lambda m, n: (m, jnp.maximum(n - 3, 0))),
        pl.BlockSpec((tm, LANES), row),
    ]
    out_shape = [
        jax.ShapeDtypeStruct((M, SEG), bf16),
        jax.ShapeDtypeStruct((M, SEG), f32),
        jax.ShapeDtypeStruct((M, SEG), f32),
        jax.ShapeDtypeStruct((M, 4 * SEG), bf16),
        jax.ShapeDtypeStruct((M, LANES), f32),
    ]
    if attn_aux:
        assert tm % MOBA_BLOCK == 0
        bpt = tm // MOBA_BLOCK
        out_specs += [pl.BlockSpec((tm, SEG), row), pl.BlockSpec((tm, SEG), row),
                      pl.BlockSpec((1, bpt, SEG), lambda m, n: (m, 0, 0))]
        out_shape += [jax.ShapeDtypeStruct((M, SEG), bf16), jax.ShapeDtypeStruct((M, SEG), bf16),
                      jax.ShapeDtypeStruct((M // tm, bpt, SEG), f32)]
    return pl.pallas_call(
        functools.partial(_inproj_body, attn_aux=attn_aux),
        grid=grid,
        in_specs=[
            pl.BlockSpec((tm, D), row),
            pl.BlockSpec((1, D), lambda m, n: (0, 0)),
            pl.BlockSpec((D, SEG), lambda m, n: (0, n)),
            pl.BlockSpec((D, LANES), lambda m, n: (0, 0)),
            pl.BlockSpec((1, LANES), lambda m, n: (0, 0)),
        ],
        out_specs=out_specs,
        out_shape=out_shape,
        scratch_shapes=[pltpu.VMEM((tm, D), bf16)],
        compiler_params=_params(("arbitrary", "arbitrary"), 52),
        name="inproj",
    )(x, g, w_main, w_if, b_if)


def _top_blocks_mask(scores_t, n_valid, own, n_sel):
    R = scores_t.shape[0]
    rown = _iota(scores_t.shape, 0)
    s = jnp.where(rown < n_valid, scores_t, -jnp.inf)
    mask = jnp.where(rown == own, 0.0, NEG)
    for _ in range(n_sel):
        mx = jnp.max(s, axis=0, keepdims=True)
        idx = jnp.min(jnp.where(s == mx, rown, R), axis=0, keepdims=True)
        pick = (rown == idx) & (mx > -jnp.inf)
        mask = jnp.where(pick, 0.0, mask)
        s = jnp.where(pick, -jnp.inf, s)
    return mask


ATT_HEADS = 2
ATT_CHUNK = 4


def _moba_prompt_body(slopes_ref, q_ref, kb_ref, vb_ref, kmean_ref, o_ref, kaug, vt, qaug, acc_scr, s_scr, ml_scr,
                      *, nb):
    hp = pl.program_id(1)
    qi = pl.program_id(2)
    B = MOBA_BLOCK
    CK = ATT_CHUNK * B
    heads = [slice(hh * HEAD_DIM, (hh + 1) * HEAD_DIM) for hh in range(ATT_HEADS)]

    @pl.when(qi == 0)
    def _init():
        lane = _iota((B, LANES), 1)
        s_lo = _iota((B, LANES), 0).astype(f32)
        for hh, cs in enumerate(heads):
            slope = slopes_ref[hp * ATT_HEADS + hh]
            for n in range(nb):
                c, r0 = n // ATT_CHUNK, (n % ATT_CHUNK) * B
                kaug[hh, c, r0:r0 + B, 0:HEAD_DIM] = kb_ref[0, n * B:(n + 1) * B, cs]
                ext = jnp.where(lane == n, 1.0, 0.0)
                ext = jnp.where(lane == MAX_BLOCKS, -slope, ext)
                ext = jnp.where(lane == MAX_BLOCKS + 1, slope * s_lo, ext)
                ext = jnp.where(lane == MAX_BLOCKS + 2, -slope * B, ext)
                ext = jnp.where(lane == MAX_BLOCKS + 3, slope * (B * n), ext)
                kaug[hh, c, r0:r0 + B, HEAD_DIM:2 * HEAD_DIM] = ext.astype(bf16)
                vt[hh, c, :, r0:r0 + B] = vb_ref[0, n * B:(n + 1) * B, cs].astype(f32).T.astype(bf16)

    c_own = qi // ATT_CHUNK
    n_ext = HEAD_DIM - MAX_BLOCKS
    rr = _iota((n_ext, B), 0)
    t_lo = _iota((n_ext, B), 1).astype(f32)
    q_ext = jnp.where(rr == 0, t_lo, jnp.where((rr == 1) | (rr == 3), 1.0, 0.0))
    q_ext = jnp.where(rr == 2, qi.astype(f32), q_ext).astype(bf16)
    for hh, cs in enumerate(heads):
        q_t = q_ref[0, :, cs].astype(f32).T.astype(bf16)
        scores_t = _dot(kmean_ref[0, :, cs].astype(bf16), q_t)
        mask_t = _top_blocks_mask(scores_t, qi, qi, min(MOBA_TOP_K, nb))
        if nb < MAX_BLOCKS:
            mask_t = jnp.concatenate([mask_t, jnp.full((MAX_BLOCKS - nb, B), NEG, f32)], axis=0)
        qaug[hh, 0:HEAD_DIM, :] = q_t
        qaug[hh, HEAD_DIM:HEAD_DIM + MAX_BLOCKS, :] = mask_t.astype(bf16)
        qaug[hh, HEAD_DIM + MAX_BLOCKS:, :] = q_ext
        s_scr[hh, 0] = _dot(kaug[hh, 0], qaug[hh])
        ml_scr[2 * hh:2 * hh + 1, :] = jnp.full((1, B), NEG, f32)
        ml_scr[2 * hh + 1:2 * hh + 2, :] = jnp.zeros((1, B), f32)
        acc_scr[hh] = jnp.zeros((HEAD_DIM, B), f32)

    def softmax_pv(hh, c, s):
        m = ml_scr[2 * hh:2 * hh + 1, :]
        m_new = jnp.maximum(m, jnp.max(s, axis=0, keepdims=True))
        alpha = jnp.exp(m - m_new)
        p = jnp.exp(s - m_new)
        ml_scr[2 * hh:2 * hh + 1, :] = m_new
        ml_scr[2 * hh + 1:2 * hh + 2, :] = alpha * ml_scr[2 * hh + 1:2 * hh + 2, :] + jnp.sum(p, axis=0, keepdims=True)
        acc_scr[hh] = alpha * acc_scr[hh] + _dot(vt[hh, c], p.astype(bf16))

    def stage(c, cur, nxt):
        for hh in range(ATT_HEADS):
            s_scr[hh, nxt] = _dot(kaug[hh, c + 1], qaug[hh])
        for hh in range(ATT_HEADS):
            softmax_pv(hh, c, s_scr[hh, cur])

    def body(c, carry):
        @pl.when(c % 2 == 0)
        def _():
            stage(c, 0, 1)

        @pl.when(c % 2 == 1)
        def _():
            stage(c, 1, 0)

        return carry

    lax.fori_loop(0, c_own, body, 0)
    causal = (_iota((CK, B), 0) - _iota((CK, B), 1)) <= (qi - c_own * ATT_CHUNK) * B
    for hh, cs in enumerate(heads):
        softmax_pv(hh, c_own, jnp.where(causal, s_scr[hh, c_own % 2], NEG))
        o_ref[0, :, cs] = (acc_scr[hh] / ml_scr[2 * hh + 1:2 * hh + 2, :]).T.astype(bf16)


def _moba_prompt(slopes, q, kb, vb, kmean):
    Bsz, T, _ = q.shape
    assert T % (MOBA_BLOCK * ATT_CHUNK) == 0 and H_ATT % ATT_HEADS == 0
    nb = T // MOBA_BLOCK
    assert nb <= MAX_BLOCKS
    blk = MOBA_BLOCK
    W = ATT_HEADS * HEAD_DIM
    nc = nb // ATT_CHUNK
    return pl.pallas_call(
        functools.partial(_moba_prompt_body, nb=nb),
        grid=(Bsz, H_ATT // ATT_HEADS, nb),
        in_specs=[
            pl.BlockSpec(memory_space=pltpu.SMEM),
            pl.BlockSpec((1, blk, W), lambda b, h, i: (b, i, h)),
            pl.BlockSpec((1, T, W), lambda b, h, i: (b, 0, h)),
            pl.BlockSpec((1, T, W), lambda b, h, i: (b, 0, h)),
            pl.BlockSpec((1, nb, W), lambda b, h, i: (b, 0, h)),
        ],
        out_specs=pl.BlockSpec((1, blk, W), lambda b, h, i: (b, i, h)),
        out_shape=jax.ShapeDtypeStruct((Bsz, T, H_ATT * HEAD_DIM), bf16),
        scratch_shapes=[
            pltpu.VMEM((ATT_HEADS, nc, ATT_CHUNK * blk, 2 * HEAD_DIM), bf16),
            pltpu.VMEM((ATT_HEADS, nc, HEAD_DIM, ATT_CHUNK * blk), bf16),
            pltpu.VMEM((ATT_HEADS, 2 * HEAD_DIM, blk), bf16),
            pltpu.VMEM((ATT_HEADS, HEAD_DIM, blk), f32),
            pltpu.VMEM((ATT_HEADS, 2, ATT_CHUNK * blk, blk), f32),
            pltpu.VMEM((2 * ATT_HEADS, blk), f32),
        ],
        compiler_params=_params(("arbitrary", "arbitrary", "arbitrary"), 52),
        name="moba_prompt",
    )(slopes, q, kb, vb, kmean)


FFN_STREAM_BLOCKS = 3


class _KStream(NamedTuple):
    page_table: jax.Array
    cache_k: jax.Array
    layer: int
    base: int
    bps: int
    total: int


def _kstream_specs(ks, step_of):
    nbp = ks.page_table.shape[1] // PAGES_PER_BLOCK

    def page_map(*ids, i):
        g = jnp.minimum(ks.base + step_of(*ids[:-1]) * ks.bps + i // PAGES_PER_BLOCK, ks.total - 1)
        return (ks.layer, ids[-1][g // nbp, (g % nbp) * PAGES_PER_BLOCK + i % PAGES_PER_BLOCK], 0, 0, 0)

    in_specs = [pl.BlockSpec((None, None, PAGE_SIZE, H_ATT, HEAD_DIM), functools.partial(page_map, i=i))
                for i in range(ks.bps * PAGES_PER_BLOCK)]
    out_spec = pl.BlockSpec((ks.bps, H_ATT, HEAD_DIM), lambda *ids: (step_of(*ids[:-1]), 0, 0))
    return in_specs, out_spec


def _kstream_reduce(pages, out_ref):
    for b in range(len(pages) // PAGES_PER_BLOCK):
        s = jnp.sum(pages[PAGES_PER_BLOCK * b][...], axis=0)
        for p in range(1, PAGES_PER_BLOCK):
            s = s + jnp.sum(pages[PAGES_PER_BLOCK * b + p][...], axis=0)
        out_ref[b] = s * (1.0 / MOBA_BLOCK)


def _qsel_body(x_ref, g_ref, w_ref, o_ref):
    h = _rmsnorm(x_ref[...], g_ref[...])
    o_ref[...] = jnp.dot(h, w_ref[...], preferred_element_type=f32, precision=lax.Precision.HIGHEST)


def _qsel(x, g, w_in, *, layer, tn=256):
    M, D = x.shape
    return pl.pallas_call(
        _qsel_body,
        grid=(SEG // tn,),
        in_specs=[
            pl.BlockSpec((M, D), lambda n: (0, 0)),
            pl.BlockSpec((1, D), lambda n: (0, 0)),
            pl.BlockSpec((None, D, tn), lambda n: (layer, 0, n)),
        ],
        out_specs=pl.BlockSpec((M, tn), lambda n: (0, n)),
        out_shape=jax.ShapeDtypeStruct((M, SEG), f32),
        compiler_params=_params(("arbitrary",), 32),
        name="qsel",
    )(x, g, w_in)


IDX_ROWS = 16


def _select_body(km_ref, q_ref, idx_ref, *, L, n_sel):
    nbp = km_ref.shape[0]
    idx_ref[...] = jnp.zeros(idx_ref.shape, i32)
    km = km_ref[...]
    blk = _iota((nbp, H_ATT, 1), 0)
    for l in range(L):
        sc = jnp.sum(km * q_ref[l][None], axis=-1, keepdims=True)
        for r in range(n_sel):
            mx = jnp.max(sc, axis=0, keepdims=True)
            idx = jnp.min(jnp.where(sc == mx, blk, nbp), axis=0, keepdims=True)
            idx_ref[l * n_sel + r] = jnp.broadcast_to(idx[0], (H_ATT, LANES))
            sc = jnp.where(blk == idx, -jnp.inf, sc)


def _select_sample(kmean, q8, *, L, n_sel):
    DB, nbp = kmean.shape[:2]
    assert L * n_sel <= IDX_ROWS
    return pl.pallas_call(
        functools.partial(_select_body, L=L, n_sel=n_sel),
        grid=(DB,),
        in_specs=[
            pl.BlockSpec((None, nbp, H_ATT, HEAD_DIM), lambda d: (d, 0, 0, 0)),
            pl.BlockSpec((None, 8, H_ATT, HEAD_DIM), lambda d: (d, 0, 0, 0)),
        ],
        out_specs=pl.BlockSpec((None, IDX_ROWS, H_ATT, LANES), lambda d: (d, 0, 0, 0)),
        out_shape=jax.ShapeDtypeStruct((DB, IDX_ROWS, H_ATT, LANES), i32),
        compiler_params=_params(("arbitrary",), 32),
        name="select_sample",
    )(kmean, q8)


def _sattn_body(idx_ref, pt_ref, slopes_ref, q_ref, kn_ref, vn_ref, ck_hbm, cv_hbm, o_ref, kbuf, vbuf, sem,
                *, layer, L, n_sel, past):
    nblk = L * n_sel * PAGES_PER_BLOCK
    d = pl.program_id(0)
    h = pl.program_id(1)
    step = d * H_ATT + h
    slot = step % 2

    def page_copies(dd, hh, sl):
        out = []
        for i in range(nblk):
            lj, p = i // PAGES_PER_BLOCK, i % PAGES_PER_BLOCK
            blk = idx_ref[((dd * L + lj // n_sel) * H_ATT + hh) * n_sel + lj % n_sel]
            phys = pt_ref[dd, blk * PAGES_PER_BLOCK + p]
            out.append(pltpu.make_async_copy(ck_hbm.at[layer, phys, :, hh, :], kbuf.at[sl, i], sem.at[sl]))
            out.append(pltpu.make_async_copy(cv_hbm.at[layer, phys, :, hh, :], vbuf.at[sl, i], sem.at[sl]))
        return out

    @pl.when(step == 0)
    def _():
        for cp in page_copies(d, h, slot):
            cp.start()

    @pl.when(step + 1 < pl.num_programs(0) * H_ATT)
    def _():
        nxt = step + 1
        for cp in page_copies(nxt // H_ATT, nxt % H_ATT, 1 - slot):
            cp.start()

    for cp in page_copies(d, h, slot):
        cp.wait()

    kp = [kbuf.at[slot, i] for i in range(nblk)]
    vp = [vbuf.at[slot, i] for i in range(nblk)]
    slope = slopes_ref[h]
    qf = q_ref[0].astype(f32)
    kn = kn_ref[0]
    vn = vn_ref[0]
    lpos = _iota((L, 1), 0)
    ppos = _iota((PAGE_SIZE, 1), 0)
    rows = []
    for l in range(L):
        ql = qf[l:l + 1, :]
        lo = jnp.sum(kn * ql, axis=1, keepdims=True)
        lo = jnp.where(lpos <= l, lo - slope * (l - lpos).astype(f32), NEG)
        logits = []
        for j in range(n_sel):
            blk = idx_ref[((d * L + l) * H_ATT + h) * n_sel + j]
            for p in range(PAGES_PER_BLOCK):
                kb = kp[(l * n_sel + j) * PAGES_PER_BLOCK + p][...]
                dist = (past + l - blk * MOBA_BLOCK - p * PAGE_SIZE) - ppos
                logits.append(jnp.sum(kb * ql, axis=1, keepdims=True) - slope * dist.astype(f32))
        m = jnp.max(lo, axis=0, keepdims=True)
        for lg in logits:
            m = jnp.maximum(m, jnp.max(lg, axis=0, keepdims=True))
        p_own = jnp.exp(lo - m)
        den = jnp.sum(p_own, axis=0, keepdims=True)
        num = jnp.sum(p_own * vn, axis=0, keepdims=True)
        for i, lg in enumerate(logits):
            pi = jnp.exp(lg - m)
            den = den + jnp.sum(pi, axis=0, keepdims=True)
            num = num + jnp.sum(pi * vp[l * n_sel * PAGES_PER_BLOCK + i][...], axis=0, keepdims=True)
        rows.append(num / den)
    rows.append(jnp.zeros((8 - L, HEAD_DIM), f32))
    o_ref[0] = jnp.concatenate(rows, axis=0).astype(bf16)


def _sattn_sample(idx_flat, page_table, slopes, q8, k_new, v_new, cache_k, cache_v, *, layer, L, n_sel, past):
    DB = q8.shape[0]
    assert L <= 8
    nblk = L * n_sel * PAGES_PER_BLOCK
    return pl.pallas_call(
        functools.partial(_sattn_body, layer=layer, L=L, n_sel=n_sel, past=past),
        grid_spec=pltpu.PrefetchScalarGridSpec(
            num_scalar_prefetch=2,
            grid=(DB, H_ATT),
            in_specs=[
                pl.BlockSpec(memory_space=pltpu.SMEM),
                pl.BlockSpec((1, 8, HEAD_DIM), lambda d, h, idx, pt: (d, 0, h)),
                pl.BlockSpec((1, L, HEAD_DIM), lambda d, h, idx, pt: (d, 0, h)),
                pl.BlockSpec((1, L, HEAD_DIM), lambda d, h, idx, pt: (d, 0, h)),
                pl.BlockSpec(memory_space=pl.ANY),
                pl.BlockSpec(memory_space=pl.ANY),
            ],
            out_specs=pl.BlockSpec((1, 8, HEAD_DIM), lambda d, h, idx, pt: (d, 0, h)),
            scratch_shapes=[
                pltpu.VMEM((2, nblk, PAGE_SIZE, HEAD_DIM), f32),
                pltpu.VMEM((2, nblk, PAGE_SIZE, HEAD_DIM), f32),
                pltpu.SemaphoreType.DMA((2,)),
            ],
        ),
        out_shape=jax.ShapeDtypeStruct((DB, 8, H_ATT * HEAD_DIM), bf16),
        compiler_params=_params(("arbitrary", "arbitrary"), 32),
        name="sattn_sample",
    )(idx_flat, page_table, slopes, q8, k_new, v_new, cache_k, cache_v)


def _split3_dot(a_bf16, x):
    x1 = x.astype(bf16)
    r1 = x - x1.astype(f32)
    x2 = r1.astype(bf16)
    x3 = (r1 - x2.astype(f32)).astype(bf16)
    return _dot(a_bf16, x1) + _dot(a_bf16, x2) + _dot(a_bf16, x3)


def _mlstm_body(*refs, L, l_out, n_stream):
    if n_stream:
        refs = refs[1:]
        _kstream_reduce(refs[5:5 + n_stream], refs[5 + n_stream + 4])
        refs = refs[:5] + refs[5 + n_stream:5 + n_stream + 4] + refs[5 + n_stream + 5:]
    zm_ref, gate_ref, c0_ref, n0_ref, m0_ref, hm_ref, c_out, n_out, m_out, c_scr, n_scr, m_scr = refs
    c = pl.program_id(1)
    nc = pl.num_programs(1)

    @pl.when(c == 0)
    def _():
        c_scr[...] = c0_ref[0]
        n_scr[...] = n0_ref[0]
        m_scr[...] = m0_ref[0]

    G = gate_ref[0]
    row = _iota((L, L), 0)
    col = _iota((L, L), 1)
    causal = row >= col
    csum = _split3_dot(jnp.where(causal, 1.0, 0.0).astype(bf16), G)
    lane = _iota((L, LANES), 1)
    X = jnp.where(lane < H_MLSTM, pltpu.roll(csum, LANES - H_MLSTM, 1), pltpu.roll(G, H_MLSTM, 1) - csum)
    XT = X.T
    x1 = X.astype(bf16)
    r1 = X - x1.astype(f32)
    x2 = r1.astype(bf16)
    x3 = (r1 - x2.astype(f32)).astype(bf16)
    sel_row = _iota((LANES, LANES), 0)

    def replicated(src):
        e = jnp.where(sel_row == src, 1.0, 0.0).astype(bf16)
        return _dot(x1, e) + _dot(x2, e) + _dot(x3, e)

    def over_keys(col):
        return col[:, :L] if L <= LANES else jnp.concatenate([col] * (L // LANES), axis=1)

    ones_v = jnp.ones((L, HEAD_DIM), bf16)
    for h in range(H_MLSTM):
        sl = slice(h * HEAD_DIM, (h + 1) * HEAD_DIM)
        q = zm_ref[0, :, sl]
        k = zm_ref[0, :, SEG + h * HEAD_DIM:SEG + (h + 1) * HEAD_DIM]
        v = zm_ref[0, :, 2 * SEG + h * HEAD_DIM:2 * SEG + (h + 1) * HEAD_DIM]
        o = zm_ref[0, :, 3 * SEG + h * HEAD_DIM:3 * SEG + (h + 1) * HEAD_DIM]
        b_rep = replicated(h)
        g_rep = replicated(H_MLSTM + h)
        g_row = XT[H_MLSTM + h:H_MLSTM + h + 1, :]
        m_prev = m_scr[h:h + 1, :]
        g_max = jnp.max(jnp.where(causal, g_row, -jnp.inf), axis=1, keepdims=True)
        m_t = b_rep + jnp.maximum(m_prev, g_max)
        decay_d = jnp.exp(jnp.where(causal, g_row + over_keys(b_rep - m_t), -jnp.inf))
        s = _dot_nt(q, k) * decay_d
        a_inter = jnp.exp(b_rep + m_prev - m_t)
        c_h = c_scr[h]
        n_h = n_scr[h:h + 1, :]
        sv = _dot(s.astype(bf16), jnp.concatenate([v, ones_v], axis=1))
        n_rows = jnp.broadcast_to(n_h, (HEAD_DIM, HEAD_DIM)).astype(bf16)
        cq = _dot_nt(q, jnp.concatenate([c_h.astype(bf16), n_rows], axis=0))
        num = sv[:, :HEAD_DIM] + a_inter * cq[:, :HEAD_DIM]
        den = sv[:, HEAD_DIM:] + a_inter * cq[:, HEAD_DIM:]
        hh = num / jnp.maximum(jnp.abs(den), jnp.exp(-m_t))
        out = (jax.nn.sigmoid(o.astype(f32)) * hh).astype(bf16)
        hm_ref[0, :, sl] = out[:l_out]
        m_new = m_t[L - 1:L, :]
        b_last = b_rep[L - 1:L, :]
        decay = jnp.exp(b_last + m_prev - m_new)
        wk = jnp.exp(b_last + g_rep - m_new) * k.astype(f32)
        c_scr[h] = decay * c_h + _dot_tn(v, wk.astype(bf16))
        n_scr[h:h + 1, :] = decay * n_h + jnp.sum(wk, axis=0, keepdims=True)
        m_scr[h:h + 1, :] = m_new

    @pl.when(c == nc - 1)
    def _():
        c_out[0] = c_scr[...]
        n_out[0] = n_scr[...]
        m_out[0] = m_scr[...]


def _mlstm(zm, gates, c0, n0, m0, *, L, l_out, kstream=None):
    Bsz, T, _ = zm.shape
    nc = T // L
    in_specs = [
        pl.BlockSpec((1, L, 4 * SEG), lambda b, c, *_: (b, c, 0)),
        pl.BlockSpec((1, L, LANES), lambda b, c, *_: (b, c, 0)),
        pl.BlockSpec((1, H_MLSTM, HEAD_DIM, HEAD_DIM), lambda b, c, *_: (b, 0, 0, 0)),
        pl.BlockSpec((1, H_MLSTM, HEAD_DIM), lambda b, c, *_: (b, 0, 0)),
        pl.BlockSpec((1, H_MLSTM, LANES), lambda b, c, *_: (b, 0, 0)),
    ]
    out_specs = [
        pl.BlockSpec((1, l_out, SEG), lambda b, c, *_: (b, c, 0)),
        pl.BlockSpec((1, H_MLSTM, HEAD_DIM, HEAD_DIM), lambda b, c, *_: (b, 0, 0, 0)),
        pl.BlockSpec((1, H_MLSTM, HEAD_DIM), lambda b, c, *_: (b, 0, 0)),
        pl.BlockSpec((1, H_MLSTM, LANES), lambda b, c, *_: (b, 0, 0)),
    ]
    out_shape = [
        jax.ShapeDtypeStruct((Bsz, nc * l_out, SEG), bf16),
        jax.ShapeDtypeStruct((Bsz, H_MLSTM, HEAD_DIM, HEAD_DIM), f32),
        jax.ShapeDtypeStruct((Bsz, H_MLSTM, HEAD_DIM), f32),
        jax.ShapeDtypeStruct((Bsz, H_MLSTM, LANES), f32),
    ]
    scratch = [
        pltpu.VMEM((H_MLSTM, HEAD_DIM, HEAD_DIM), f32),
        pltpu.VMEM((H_MLSTM, HEAD_DIM), f32),
        pltpu.VMEM((H_MLSTM, LANES), f32),
    ]
    args = [zm, gates, c0, n0, m0]
    n_stream = 0
    if kstream is not None:
        page_specs, km_spec = _kstream_specs(kstream, lambda b, c: b * nc + c)
        n_stream = len(page_specs)
        in_specs += page_specs
        out_specs.append(km_spec)
        out_shape.append(jax.ShapeDtypeStruct((Bsz * nc * kstream.bps, H_ATT, HEAD_DIM), f32))
        args = [kstream.page_table] + args + [kstream.cache_k] * n_stream
    return pl.pallas_call(
        functools.partial(_mlstm_body, L=L, l_out=l_out, n_stream=n_stream),
        grid_spec=pltpu.PrefetchScalarGridSpec(
            num_scalar_prefetch=1 if n_stream else 0,
            grid=(Bsz, nc),
            in_specs=in_specs,
            out_specs=out_specs,
            scratch_shapes=scratch,
        ),
        out_shape=out_shape,
        compiler_params=_params(("arbitrary", "arbitrary"), 16 + n_stream),
        name="mlstm",
    )(*args)


def _outproj_body(x_ref, att_ref, hm_ref, wa_ref, wm_ref, o_ref):
    o_ref[...] = x_ref[...] + _dot(att_ref[...], wa_ref[...]) + _dot(hm_ref[...], wm_ref[...])


def _outproj(x, att, hm, w_out, *, tm, tn):
    M, D = x.shape
    return pl.pallas_call(
        _outproj_body,
        grid=(M // tm, D // tn),
        in_specs=[
            pl.BlockSpec((tm, tn), lambda m, n: (m, n)),
            pl.BlockSpec((tm, SEG), lambda m, n: (m, 0)),
            pl.BlockSpec((tm, SEG), lambda m, n: (m, 0)),
            pl.BlockSpec((SEG, tn), lambda m, n: (0, n)),
            pl.BlockSpec((SEG, tn), lambda m, n: (1, n)),
        ],
        out_specs=pl.BlockSpec((tm, tn), lambda m, n: (m, n)),
        out_shape=jax.ShapeDtypeStruct((M, D), f32),
        compiler_params=_params(("arbitrary", "arbitrary"), 40),
        name="outproj",
    )(x, att, hm, w_out, w_out)


def _ffn_body(*refs, tm, seq_len, carry_mode, n_stream):
    if n_stream:
        refs = refs[1:]
        _kstream_reduce(refs[7:7 + n_stream], refs[7 + n_stream + 2])
        refs = refs[:7] + refs[7 + n_stream:7 + n_stream + 2] + refs[7 + n_stream + 3:]
    if carry_mode:
        x_ref, g_ref, gf_ref, wg_ref, wu_ref, cw_ref, wd_ref, y_ref, tail_ref, h_scr, acc_scr, carry = refs
    else:
        x_ref, g_ref, gf_ref, wg_ref, wu_ref, cw_ref, wd_ref, p1_ref, p2_ref, y_ref, a_ref, h_scr, acc_scr = refs
    m = pl.program_id(0)
    f = pl.program_id(1)
    nf = pl.num_programs(1)

    @pl.when(f == 0)
    def _():
        h_scr[...] = _rmsnorm(x_ref[...], g_ref[...]).astype(bf16)
        acc_scr[...] = jnp.zeros(acc_scr.shape, f32)

    hb = h_scr[...]
    a = _dot(hb, wg_ref[...])
    u = _dot(hb, wu_ref[...])
    row = _iota(a.shape, 0)
    a1 = pltpu.roll(a, 1, 0)
    a2 = pltpu.roll(a, 2, 0)
    if carry_mode:
        prev = carry[f]
        prev = jnp.where(m % (seq_len // tm) == 0, 0.0, prev)
        a1 = jnp.where(row == 0, prev[7:8, :], a1)
        a2 = jnp.where(row == 0, prev[6:7, :], jnp.where(row == 1, prev[7:8, :], a2))
        carry[f] = a[tm - 8:, :]
        tail_ref[0] = a[tm - 8:, :]
    else:
        lpos = row % seq_len
        a1 = jnp.where(lpos >= 1, a1, p1_ref[...])
        a2 = jnp.where(lpos >= 2, a2, p2_ref[...])
        a_ref[...] = a
    cw = cw_ref[...]
    cv = cw[0:1, :] * a2 + cw[1:2, :] * a1 + cw[2:3, :] * a
    gl = 0.5 * cv * (1.0 + lax.erf(cv * (0.5 ** 0.5)))
    acc_scr[...] += _dot((gl * u).astype(bf16), wd_ref[...])

    @pl.when(f == nf - 1)
    def _():
        y_ref[...] = _rmsnorm(x_ref[...] + acc_scr[...], gf_ref[...])


def _ffn(x1, g_ffn, g_final, w_gate, w_up, conv_w, w_down, *, tm, tf, seq_len, prev=None, kstream=None):
    M, D = x1.shape
    DF = w_gate.shape[1]
    nf = DF // tf
    carry_mode = prev is None
    in_specs = [
        pl.BlockSpec((tm, D), lambda m, f, *_: (m, 0)),
        pl.BlockSpec((1, D), lambda m, f, *_: (0, 0)),
        pl.BlockSpec((1, D), lambda m, f, *_: (0, 0)),
        pl.BlockSpec((D, tf), lambda m, f, *_: (0, f)),
        pl.BlockSpec((D, tf), lambda m, f, *_: (0, f)),
        pl.BlockSpec((CONV_W, tf), lambda m, f, *_: (0, f)),
        pl.BlockSpec((tf, D), lambda m, f, *_: (f, 0)),
    ]
    args = [x1, g_ffn, g_final, w_gate, w_up, conv_w, w_down]
    scratch = [pltpu.VMEM((tm, D), bf16), pltpu.VMEM((tm, D), f32)]
    if carry_mode:
        assert seq_len % tm == 0
        out_specs = [pl.BlockSpec((tm, D), lambda m, f, *_: (m, 0)), pl.BlockSpec((1, 8, tf), lambda m, f, *_: (m, 0, f))]
        out_shape = [jax.ShapeDtypeStruct((M, D), f32), jax.ShapeDtypeStruct((M // tm, 8, DF), f32)]
        scratch.append(pltpu.VMEM((nf, 8, tf), f32))
    else:
        assert tm % seq_len == 0
        in_specs += [pl.BlockSpec((tm, tf), lambda m, f, *_: (m, f)), pl.BlockSpec((tm, tf), lambda m, f, *_: (m, f))]
        args += list(prev)
        out_specs = [pl.BlockSpec((tm, D), lambda m, f, *_: (m, 0)), pl.BlockSpec((tm, tf), lambda m, f, *_: (m, f))]
        out_shape = [jax.ShapeDtypeStruct((M, D), f32), jax.ShapeDtypeStruct((M, DF), f32)]
    n_stream = 0
    if kstream is not None:
        assert carry_mode
        page_specs, km_spec = _kstream_specs(kstream, lambda m, f: m * nf + f)
        n_stream = len(page_specs)
        in_specs += page_specs
        out_specs.append(km_spec)
        out_shape.append(jax.ShapeDtypeStruct((M // tm * nf * kstream.bps, H_ATT, HEAD_DIM), f32))
        args = [kstream.page_table] + args + [kstream.cache_k] * n_stream
    return pl.pallas_call(
        functools.partial(_ffn_body, tm=tm, seq_len=seq_len, carry_mode=carry_mode, n_stream=n_stream),
        grid_spec=pltpu.PrefetchScalarGridSpec(
            num_scalar_prefetch=1 if n_stream else 0,
            grid=(M // tm, nf),
            in_specs=in_specs,
            out_specs=out_specs,
            scratch_shapes=scratch,
        ),
        out_shape=out_shape,
        compiler_params=_params(("arbitrary", "arbitrary"), 48 + n_stream),
        name="convffn",
    )(*args)


def kernel(x_prompt, x_sample, cache_k, cache_v, state_C, state_n, state_m, state_conv, page_table,
           norm_mix, w_in, b_if, w_out, norm_ffn, w_gate, w_up, conv_w, w_down, norm_final):
    Bsz, T, D = x_prompt.shape
    DB, Ls, _ = x_sample.shape
    depth = w_in.shape[0]
    n_pages = page_table.shape[1]
    past = n_pages * PAGE_SIZE
    assert past % MOBA_BLOCK == 0 and past // MOBA_BLOCK >= 1, "sample path assumes a block-aligned, non-empty past"
    nbp = past // MOBA_BLOCK
    n_sel_s = min(MOBA_TOP_K, nbp)
    slopes = 2.0 ** (-8.0 * jnp.arange(1, H_ATT + 1, dtype=f32) / H_ATT)
    n_main = N_SEG * SEG
    TM_P, TM_S = 512, DB * Ls
    L_P, L_S = 256, 8
    TF = 512

    xp = x_prompt.reshape(Bsz * T, D)
    xs = x_sample.reshape(DB * Ls, D)
    st_p, st_s = [], []
    for l in range(depth):
        w_main = w_in[l].astype(bf16)
        w_if = jnp.pad(w_main[:, n_main:], ((0, 0), (0, LANES - 2 * H_MLSTM)))
        bif = jnp.pad(b_if[l], (0, LANES - 2 * H_MLSTM)).reshape(1, LANES)
        wo = w_out[l].astype(bf16)
        wg, wu, wd = w_gate[l].astype(bf16), w_up[l].astype(bf16), w_down[l].astype(bf16)
        g_mix, g_ffn = norm_mix[l].reshape(1, D), norm_ffn[l].reshape(1, D)
        g_fin = norm_final.reshape(1, D) if l == depth - 1 else jnp.ones((1, D), f32)
        assert depth == 1, "final norm is fused into the last layer's FFN call"

        q, k, v, zm, gates, kb, vb, kmean = _inproj(xp, g_mix, w_main, w_if, bif, tm=TM_P, attn_aux=True)
        att = _moba_prompt(slopes, q.reshape(Bsz, T, SEG), kb.reshape(Bsz, T, SEG), vb.reshape(Bsz, T, SEG),
                           kmean.reshape(Bsz, T // MOBA_BLOCK, SEG))
        n_blk = DB * nbp
        n_ffn = min(n_blk, (Bsz * T // TM_P) * (wg.shape[1] // TF) * FFN_STREAM_BLOCKS)
        ml_steps = Bsz * T // L_P
        ks_ffn = _KStream(page_table, cache_k, l, 0, FFN_STREAM_BLOCKS, n_blk)
        ks_ml = _KStream(page_table, cache_k, l, n_ffn, max(1, -(-(n_blk - n_ffn) // ml_steps)), n_blk)
        hm, c_p, n_p, m_p, km_ml = _mlstm(
            zm.reshape(Bsz, T, 4 * SEG), gates.reshape(Bsz, T, LANES),
            jnp.zeros((Bsz, H_MLSTM, HEAD_DIM, HEAD_DIM), f32), jnp.zeros((Bsz, H_MLSTM, HEAD_DIM), f32),
            jnp.zeros((Bsz, H_MLSTM, LANES), f32), L=L_P, l_out=L_P, kstream=ks_ml)
        x1 = _outproj(xp, att.reshape(Bsz * T, SEG), hm.reshape(Bsz * T, SEG), wo, tm=TM_P, tn=1024)
        xp, tail, km_ffn = _ffn(x1, g_ffn, g_fin, wg, wu, conv_w[l], wd, tm=TM_P, tf=TF, seq_len=T, kstream=ks_ffn)
        kmean_s = jnp.concatenate([km_ffn[:n_ffn], km_ml[:n_blk - n_ffn]], axis=0).reshape(DB, nbp, H_ATT, HEAD_DIM)
        conv_p = tail.reshape(Bsz, T // TM_P, 8, -1)[:, -1, 8 - (CONV_W - 1):, :]
        st_p.append((k.reshape(Bsz, T, H_ATT, HEAD_DIM), v.reshape(Bsz, T, H_ATT, HEAD_DIM), c_p, n_p, m_p[:, :, 0], conv_p))

        q, k, v, zm, gates = _inproj(xs, g_mix, w_main, w_if, bif, tm=TM_S, attn_aux=False)
        q8 = jnp.pad(q.reshape(DB, Ls, SEG), ((0, 0), (0, 8 - Ls), (0, 0)))
        q_rank = _qsel(xs, g_mix, w_in, layer=l)
        q_rank = jnp.pad(q_rank.reshape(DB, Ls, H_ATT, HEAD_DIM), ((0, 0), (0, 8 - Ls), (0, 0), (0, 0)))
        idx = _select_sample(kmean_s, q_rank, L=Ls, n_sel=n_sel_s)
        idx = idx[:, :Ls * n_sel_s, :, 0].reshape(DB, Ls, n_sel_s, H_ATT)
        idx_flat = jnp.transpose(idx, (0, 1, 3, 2)).reshape(-1)
        att8 = _sattn_sample(idx_flat, page_table, slopes, q8, k.reshape(DB, Ls, SEG), v.reshape(DB, Ls, SEG),
                             cache_k, cache_v, layer=l, L=Ls, n_sel=n_sel_s, past=past)
        att = att8[:, :Ls].reshape(DB * Ls, SEG)
        pad_t = ((0, 0), (0, L_S - Ls), (0, 0))
        gate_pad = jnp.where(jnp.arange(LANES) < H_MLSTM, NEG, 0.0).astype(f32)
        gates_s = jnp.concatenate(
            [gates.reshape(DB, Ls, LANES), jnp.broadcast_to(gate_pad, (DB, L_S - Ls, LANES))], axis=1)
        hm8, c_s, n_s, m_s = _mlstm(
            jnp.pad(zm.reshape(DB, Ls, 4 * SEG), pad_t), gates_s, state_C[l], state_n[l],
            jnp.broadcast_to(state_m[l][:, :, None], (DB, H_MLSTM, LANES)), L=L_S, l_out=8)
        hm = hm8[:, :Ls].reshape(DB * Ls, SEG)
        x1 = _outproj(xs, att, hm, wo, tm=TM_S, tn=1024)
        buf = state_conv[l]
        zero = jnp.zeros_like(buf[:, 0])
        p1 = jnp.stack([buf[:, 1]] + [zero] * (Ls - 1), axis=1).reshape(DB * Ls, -1)
        p2 = jnp.stack([buf[:, 0], buf[:, 1]] + [zero] * (Ls - 2), axis=1).reshape(DB * Ls, -1)
        xs, a_full = _ffn(x1, g_ffn, g_fin, wg, wu, conv_w[l], wd, tm=TM_S, tf=TF, seq_len=Ls, prev=(p1, p2))
        conv_s = a_full.reshape(DB, Ls, -1)[:, Ls - (CONV_W - 1):]
        st_s.append((k.reshape(DB, Ls, H_ATT, HEAD_DIM), v.reshape(DB, Ls, H_ATT, HEAD_DIM), c_s, n_s, m_s[:, :, 0], conv_s))

    k_p, v_p, C_p, n_p, m_p, conv_p = [jnp.stack(z) for z in zip(*st_p)]
    k_s, v_s, C_s, n_s, m_s, conv_s = [jnp.stack(z) for z in zip(*st_s)]
    return (xp.reshape(Bsz, T, D), xs.reshape(DB, Ls, D), k_p, v_p, C_p, n_p, m_p, conv_p,
            k_s, v_s, C_s, n_s, m_s, conv_s)
```

```python
import functools
from typing import NamedTuple

import jax
import jax.numpy as jnp
from jax import lax
from jax.experimental import pallas as pl
from jax.experimental.pallas import tpu as pltpu

f32, bf16, i32 = jnp.float32, jnp.bfloat16, jnp.int32

HEAD_DIM = 128
H_ATT = 8
H_MLSTM = 8
SEG = H_ATT * HEAD_DIM
N_SEG = 7
CONV_W = 3
MOBA_BLOCK = 256
MOBA_TOP_K = 3
PAGE_SIZE = 128
PAGES_PER_BLOCK = MOBA_BLOCK // PAGE_SIZE
EPS = 1e-6
NEG = -1e30
LANES = 128
MAX_BLOCKS = 32
MIB = 1 << 20


def _params(sem, vmem_mib):
    return pltpu.CompilerParams(dimension_semantics=sem, vmem_limit_bytes=vmem_mib * MIB)


def _iota(shape, dim, dtype=i32):
    return lax.broadcasted_iota(dtype, shape, dim)


def _dot(a, b):
    return jnp.dot(a, b, preferred_element_type=f32)


def _dot_nt(a, b):
    return lax.dot_general(a, b, (((1,), (1,)), ((), ())), preferred_element_type=f32)


def _dot_tn(a, b):
    return lax.dot_general(a, b, (((0,), (0,)), ((), ())), preferred_element_type=f32)


def _rmsnorm(x, g):
    return x * lax.rsqrt(jnp.mean(x * x, axis=-1, keepdims=True) + EPS) * g


def _cast_w_in_body(w_ref, main_ref, wif_ref):
    n = pl.program_id(0)

    @pl.when(n < N_SEG)
    def _():
        main_ref[...] = w_ref[...].astype(bf16)

    @pl.when(n == N_SEG)
    def _():
        g = w_ref[:, 0:2 * H_MLSTM].astype(bf16)
        wif_ref[...] = jnp.concatenate([g, jnp.zeros((g.shape[0], LANES - 2 * H_MLSTM), bf16)], axis=1)


def _cast_w_in(w_in, *, layer):
    D = w_in.shape[1]
    assert w_in.shape[2] == N_SEG * SEG + 2 * H_MLSTM
    return pl.pallas_call(
        _cast_w_in_body,
        grid=(N_SEG + 1,),
        in_specs=[pl.BlockSpec((None, D, SEG), lambda n: (layer, 0, n))],
        out_specs=[pl.BlockSpec((D, SEG), lambda n: (0, jnp.minimum(n, N_SEG - 1))),
                   pl.BlockSpec((D, LANES), lambda n: (0, 0))],
        out_shape=[jax.ShapeDtypeStruct((D, N_SEG * SEG), bf16), jax.ShapeDtypeStruct((D, LANES), bf16)],
        compiler_params=_params(("arbitrary",), 40),
        name="cast_w_in",
    )(w_in)


def _inproj_body(x_ref, g_ref, w_ref, wif_ref, bif_ref, q_ref, k_ref, v_ref, zm_ref, gate_ref, *rest, attn_aux):
    if attn_aux:
        kb_ref, vb_ref, kmean_ref, h_scr = rest
    else:
        (h_scr,) = rest
    n = pl.program_id(1)
    scale = HEAD_DIM ** -0.5

    @pl.when(n == 0)
    def _():
        hb = _rmsnorm(x_ref[...], g_ref[...]).astype(bf16)
        h_scr[...] = hb
        zg = _dot(hb, wif_ref[...]) + bif_ref[...]
        lane = _iota(zg.shape, 1)
        log_sig = jnp.minimum(zg, 0.0) - jnp.log1p(jnp.exp(-jnp.abs(zg)))
        gate_ref[...] = jnp.where(lane < H_MLSTM, zg, jnp.where(lane < 2 * H_MLSTM, log_sig, 0.0))
        q_ref[...] = (_dot(hb, w_ref[...]) * scale).astype(bf16)

    def segment():
        return _dot(h_scr[...], w_ref[...])

    @pl.when(n == 1)
    def _():
        z = segment()
        k_ref[...] = z
        if attn_aux:
            kb_ref[...] = z.astype(bf16)
            for i in range(z.shape[0] // MOBA_BLOCK):
                blk = z[i * MOBA_BLOCK:(i + 1) * MOBA_BLOCK, :]
                kmean_ref[0, i:i + 1, :] = jnp.sum(blk, axis=0, keepdims=True) * (1.0 / MOBA_BLOCK)

    @pl.when(n == 2)
    def _():
        z = segment()
        v_ref[...] = z
        if attn_aux:
            vb_ref[...] = z.astype(bf16)

    @pl.when(n == 4)
    def _():
        zm_ref[...] = (segment() * scale).astype(bf16)

    @pl.when((n == 3) | (n >= 5))
    def _():
        zm_ref[...] = segment().astype(bf16)


def _inproj(x, g, w_main, w_if, b_if, *, tm, attn_aux):
    M, D = x.shape
    grid = (M // tm, N_SEG)

    def row(m, n):
        return (m, 0)

    out_specs = [
        pl.BlockSpec((tm, SEG), row),
        pl.BlockSpec((tm, SEG), row),
        pl.BlockSpec((tm, SEG), row),
        pl.BlockSpec((tm, SEG), lambda m, n: (m, jnp.maximum(n - 3, 0))),
        pl.BlockSpec((tm, LANES), row),
    ]
    out_shape = [
        jax.ShapeDtypeStruct((M, SEG), bf16),
        jax.ShapeDtypeStruct((M, SEG), f32),
        jax.ShapeDtypeStruct((M, SEG), f32),
        jax.ShapeDtypeStruct((M, 4 * SEG), bf16),
        jax.ShapeDtypeStruct((M, LANES), f32),
    ]
    if attn_aux:
        assert tm % MOBA_BLOCK == 0
        bpt = tm // MOBA_BLOCK
        out_specs += [pl.BlockSpec((tm, SEG), row), pl.BlockSpec((tm, SEG), row),
                      pl.BlockSpec((1, bpt, SEG), lambda m, n: (m, 0, 0))]
        out_shape += [jax.ShapeDtypeStruct((M, SEG), bf16), jax.ShapeDtypeStruct((M, SEG), bf16),
                      jax.ShapeDtypeStruct((M // tm, bpt, SEG), f32)]
    return pl.pallas_call(
        functools.partial(_inproj_body, attn_aux=attn_aux),
        grid=grid,
        in_specs=[
            pl.BlockSpec((tm, D), row),
            pl.BlockSpec((1, D), lambda m, n: (0, 0)),
            pl.BlockSpec((D, SEG), lambda m, n: (0, n)),
            pl.BlockSpec((D, LANES), lambda m, n: (0, 0)),
            pl.BlockSpec((1, LANES), lambda m, n: (0, 0)),
        ],
        out_specs=out_specs,
        out_shape=out_shape,
        scratch_shapes=[pltpu.VMEM((tm, D), bf16)],
        compiler_params=_params(("arbitrary", "arbitrary"), 52),
        name="inproj",
    )(x, g, w_main, w_if, b_if)


def _top_blocks_mask(scores_t, n_valid, own, n_sel):
    R = scores_t.shape[0]
    rown = _iota(scores_t.shape, 0)
    s = jnp.where(rown < n_valid, scores_t, -jnp.inf)
    mask = jnp.where(rown == own, 0.0, NEG)
    for _ in range(n_sel):
        mx = jnp.max(s, axis=0, keepdims=True)
        idx = jnp.min(jnp.where(s == mx, rown, R), axis=0, keepdims=True)
        pick = (rown == idx) & (mx > -jnp.inf)
        mask = jnp.where(pick, 0.0, mask)
        s = jnp.where(pick, -jnp.inf, s)
    return mask


ATT_HEADS = 2
ATT_CHUNK = 4


def _moba_prompt_body(slopes_ref, q_ref, kb_ref, vb_ref, kmean_ref, o_ref, kaug, vt, qaug, acc_scr, s_scr, ml_scr,
                      *, nb):
    hp = pl.program_id(1)
    qi = pl.program_id(2)
    B = MOBA_BLOCK
    CK = ATT_CHUNK * B
    heads = [slice(hh * HEAD_DIM, (hh + 1) * HEAD_DIM) for hh in range(ATT_HEADS)]

    @pl.when(qi == 0)
    def _init():
        lane = _iota((B, LANES), 1)
        s_lo = _iota((B, LANES), 0).astype(f32)
        for hh, cs in enumerate(heads):
            slope = slopes_ref[hp * ATT_HEADS + hh]
            for n in range(nb):
                c, r0 = n // ATT_CHUNK, (n % ATT_CHUNK) * B
                kaug[hh, c, r0:r0 + B, 0:HEAD_DIM] = kb_ref[0, n * B:(n + 1) * B, cs]
                ext = jnp.where(lane == n, 1.0, 0.0)
                ext = jnp.where(lane == MAX_BLOCKS, -slope, ext)
                ext = jnp.where(lane == MAX_BLOCKS + 1, slope * s_lo, ext)
                ext = jnp.where(lane == MAX_BLOCKS + 2, -slope * B, ext)
                ext = jnp.where(lane == MAX_BLOCKS + 3, slope * (B * n), ext)
                kaug[hh, c, r0:r0 + B, HEAD_DIM:2 * HEAD_DIM] = ext.astype(bf16)
                vt[hh, c, :, r0:r0 + B] = vb_ref[0, n * B:(n + 1) * B, cs].astype(f32).T.astype(bf16)

    c_own = qi // ATT_CHUNK
    n_ext = HEAD_DIM - MAX_BLOCKS
    rr = _iota((n_ext, B), 0)
    t_lo = _iota((n_ext, B), 1).astype(f32)
    q_ext = jnp.where(rr == 0, t_lo, jnp.where((rr == 1) | (rr == 3), 1.0, 0.0))
    q_ext = jnp.where(rr == 2, qi.astype(f32), q_ext).astype(bf16)
    for hh, cs in enumerate(heads):
        q_t = q_ref[0, :, cs].astype(f32).T.astype(bf16)
        scores_t = _dot(kmean_ref[0, :, cs].astype(bf16), q_t)
        mask_t = _top_blocks_mask(scores_t, qi, qi, min(MOBA_TOP_K, nb))
        if nb < MAX_BLOCKS:
            mask_t = jnp.concatenate([mask_t, jnp.full((MAX_BLOCKS - nb, B), NEG, f32)], axis=0)
        qaug[hh, 0:HEAD_DIM, :] = q_t
        qaug[hh, HEAD_DIM:HEAD_DIM + MAX_BLOCKS, :] = mask_t.astype(bf16)
        qaug[hh, HEAD_DIM + MAX_BLOCKS:, :] = q_ext
        s_scr[hh, 0] = _dot(kaug[hh, 0], qaug[hh])
        ml_scr[2 * hh:2 * hh + 1, :] = jnp.full((1, B), NEG, f32)
        ml_scr[2 * hh + 1:2 * hh + 2, :] = jnp.zeros((1, B), f32)
        acc_scr[hh] = jnp.zeros((HEAD_DIM, B), f32)

    def softmax_pv(hh, c, s):
        m = ml_scr[2 * hh:2 * hh + 1, :]
        m_new = jnp.maximum(m, jnp.max(s, axis=0, keepdims=True))
        alpha = jnp.exp(m - m_new)
        p = jnp.exp(s - m_new)
        ml_scr[2 * hh:2 * hh + 1, :] = m_new
        ml_scr[2 * hh + 1:2 * hh + 2, :] = alpha * ml_scr[2 * hh + 1:2 * hh + 2, :] + jnp.sum(p, axis=0, keepdims=True)
        acc_scr[hh] = alpha * acc_scr[hh] + _dot(vt[hh, c], p.astype(bf16))

    def stage(c, cur, nxt):
        for hh in range(ATT_HEADS):
            s_scr[hh, nxt] = _dot(kaug[hh, c + 1], qaug[hh])
        for hh in range(ATT_HEADS):
            softmax_pv(hh, c, s_scr[hh, cur])

    def body(c, carry):
        @pl.when(c % 2 == 0)
        def _():
            stage(c, 0, 1)

        @pl.when(c % 2 == 1)
        def _():
            stage(c, 1, 0)

        return carry

    lax.fori_loop(0, c_own, body, 0)
    causal = (_iota((CK, B), 0) - _iota((CK, B), 1)) <= (qi - c_own * ATT_CHUNK) * B
    for hh, cs in enumerate(heads):
        softmax_pv(hh, c_own, jnp.where(causal, s_scr[hh, c_own % 2], NEG))
        o_ref[0, :, cs] = (acc_scr[hh] / ml_scr[2 * hh + 1:2 * hh + 2, :]).T.astype(bf16)


def _moba_prompt(slopes, q, kb, vb, kmean):
    Bsz, T, _ = q.shape
    assert T % (MOBA_BLOCK * ATT_CHUNK) == 0 and H_ATT % ATT_HEADS == 0
    nb = T // MOBA_BLOCK
    assert nb <= MAX_BLOCKS
    blk = MOBA_BLOCK
    W = ATT_HEADS * HEAD_DIM
    nc = nb // ATT_CHUNK
    return pl.pallas_call(
        functools.partial(_moba_prompt_body, nb=nb),
        grid=(Bsz, H_ATT // ATT_HEADS, nb),
        in_specs=[
            pl.BlockSpec(memory_space=pltpu.SMEM),
            pl.BlockSpec((1, blk, W), lambda b, h, i: (b, i, h)),
            pl.BlockSpec((1, T, W), lambda b, h, i: (b, 0, h)),
            pl.BlockSpec((1, T, W), lambda b, h, i: (b, 0, h)),
            pl.BlockSpec((1, nb, W), lambda b, h, i: (b, 0, h)),
        ],
        out_specs=pl.BlockSpec((1, blk, W), lambda b, h, i: (b, i, h)),
        out_shape=jax.ShapeDtypeStruct((Bsz, T, H_ATT * HEAD_DIM), bf16),
        scratch_shapes=[
            pltpu.VMEM((ATT_HEADS, nc, ATT_CHUNK * blk, 2 * HEAD_DIM), bf16),
            pltpu.VMEM((ATT_HEADS, nc, HEAD_DIM, ATT_CHUNK * blk), bf16),
            pltpu.VMEM((ATT_HEADS, 2 * HEAD_DIM, blk), bf16),
            pltpu.VMEM((ATT_HEADS, HEAD_DIM, blk), f32),
            pltpu.VMEM((ATT_HEADS, 2, ATT_CHUNK * blk, blk), f32),
            pltpu.VMEM((2 * ATT_HEADS, blk), f32),
        ],
        compiler_params=_params(("arbitrary", "arbitrary", "arbitrary"), 52),
        name="moba_prompt",
    )(slopes, q, kb, vb, kmean)


FFN_STREAM_BLOCKS = 3


class _KStream(NamedTuple):
    page_table: jax.Array
    cache_k: jax.Array
    layer: int
    base: int
    bps: int
    total: int


def _kstream_specs(ks, step_of):
    nbp = ks.page_table.shape[1] // PAGES_PER_BLOCK

    def page_map(*ids, i):
        g = jnp.minimum(ks.base + step_of(*ids[:-1]) * ks.bps + i // PAGES_PER_BLOCK, ks.total - 1)
        return (ks.layer, ids[-1][g // nbp, (g % nbp) * PAGES_PER_BLOCK + i % PAGES_PER_BLOCK], 0, 0, 0)

    in_specs = [pl.BlockSpec((None, None, PAGE_SIZE, H_ATT, HEAD_DIM), functools.partial(page_map, i=i))
                for i in range(ks.bps * PAGES_PER_BLOCK)]
    out_spec = pl.BlockSpec((ks.bps, H_ATT, HEAD_DIM), lambda *ids: (step_of(*ids[:-1]), 0, 0))
    return in_specs, out_spec


def _kstream_reduce(pages, out_ref):
    for b in range(len(pages) // PAGES_PER_BLOCK):
        s = jnp.sum(pages[PAGES_PER_BLOCK * b][...], axis=0)
        for p in range(1, PAGES_PER_BLOCK):
            s = s + jnp.sum(pages[PAGES_PER_BLOCK * b + p][...], axis=0)
        out_ref[b] = s * (1.0 / MOBA_BLOCK)


def _qsel_body(x_ref, g_ref, w_ref, o_ref):
    h = _rmsnorm(x_ref[...], g_ref[...])
    o_ref[...] = jnp.dot(h, w_ref[...], preferred_element_type=f32, precision=lax.Precision.HIGHEST)


def _qsel(x, g, w_in, *, layer, tn=256):
    M, D = x.shape
    return pl.pallas_call(
        _qsel_body,
        grid=(SEG // tn,),
        in_specs=[
            pl.BlockSpec((M, D), lambda n: (0, 0)),
            pl.BlockSpec((1, D), lambda n: (0, 0)),
            pl.BlockSpec((None, D, tn), lambda n: (layer, 0, n)),
        ],
        out_specs=pl.BlockSpec((M, tn), lambda n: (0, n)),
        out_shape=jax.ShapeDtypeStruct((M, SEG), f32),
        compiler_params=_params(("arbitrary",), 32),
        name="qsel",
    )(x, g, w_in)


IDX_ROWS = 16


def _select_body(km_ref, q_ref, idx_ref, *, L, n_sel):
    nbp = km_ref.shape[0]
    idx_ref[...] = jnp.zeros(idx_ref.shape, i32)
    km = km_ref[...]
    blk = _iota((nbp, H_ATT, 1), 0)
    for l in range(L):
        sc = jnp.sum(km * q_ref[l][None], axis=-1, keepdims=True)
        for r in range(n_sel):
            mx = jnp.max(sc, axis=0, keepdims=True)
            idx = jnp.min(jnp.where(sc == mx, blk, nbp), axis=0, keepdims=True)
            idx_ref[l * n_sel + r] = jnp.broadcast_to(idx[0], (H_ATT, LANES))
            sc = jnp.where(blk == idx, -jnp.inf, sc)


def _select_sample(kmean, q8, *, L, n_sel):
    DB, nbp = kmean.shape[:2]
    assert L * n_sel <= IDX_ROWS
    return pl.pallas_call(
        functools.partial(_select_body, L=L, n_sel=n_sel),
        grid=(DB,),
        in_specs=[
            pl.BlockSpec((None, nbp, H_ATT, HEAD_DIM), lambda d: (d, 0, 0, 0)),
            pl.BlockSpec((None, 8, H_ATT, HEAD_DIM), lambda d: (d, 0, 0, 0)),
        ],
        out_specs=pl.BlockSpec((None, IDX_ROWS, H_ATT, LANES), lambda d: (d, 0, 0, 0)),
        out_shape=jax.ShapeDtypeStruct((DB, IDX_ROWS, H_ATT, LANES), i32),
        compiler_params=_params(("arbitrary",), 32),
        name="select_sample",
    )(kmean, q8)


def _sattn_body(idx_ref, pt_ref, slopes_ref, q_ref, kn_ref, vn_ref, ck_hbm, cv_hbm, o_ref, kbuf, vbuf, sem,
                *, layer, L, n_sel, past):
    nblk = L * n_sel * PAGES_PER_BLOCK
    d = pl.program_id(0)
    h = pl.program_id(1)
    step = d * H_ATT + h
    slot = step % 2

    def page_copies(dd, hh, sl):
        out = []
        for i in range(nblk):
            lj, p = i // PAGES_PER_BLOCK, i % PAGES_PER_BLOCK
            blk = idx_ref[((dd * L + lj // n_sel) * H_ATT + hh) * n_sel + lj % n_sel]
            phys = pt_ref[dd, blk * PAGES_PER_BLOCK + p]
            out.append(pltpu.make_async_copy(ck_hbm.at[layer, phys, :, hh, :], kbuf.at[sl, i], sem.at[sl]))
            out.append(pltpu.make_async_copy(cv_hbm.at[layer, phys, :, hh, :], vbuf.at[sl, i], sem.at[sl]))
        return out

    @pl.when(step == 0)
    def _():
        for cp in page_copies(d, h, slot):
            cp.start()

    @pl.when(step + 1 < pl.num_programs(0) * H_ATT)
    def _():
        nxt = step + 1
        for cp in page_copies(nxt // H_ATT, nxt % H_ATT, 1 - slot):
            cp.start()

    for cp in page_copies(d, h, slot):
        cp.wait()

    kp = [kbuf.at[slot, i] for i in range(nblk)]
    vp = [vbuf.at[slot, i] for i in range(nblk)]
    slope = slopes_ref[h]
    qf = q_ref[0].astype(f32)
    kn = kn_ref[0]
    vn = vn_ref[0]
    lpos = _iota((L, 1), 0)
    slope_pos = slope * _iota((PAGE_SIZE, 1), 0).astype(f32)
    rows = []
    for l in range(L):
        ql = qf[l:l + 1, :]
        lo = jnp.sum(kn * ql, axis=1, keepdims=True)
        lo = jnp.where(lpos <= l, lo - slope * (l - lpos).astype(f32), NEG)
        logits = []
        for j in range(n_sel):
            blk = idx_ref[((d * L + l) * H_ATT + h) * n_sel + j]
            for p in range(PAGES_PER_BLOCK):
                kb = kp[(l * n_sel + j) * PAGES_PER_BLOCK + p][...]
                page_dist = (past + l - blk * MOBA_BLOCK - p * PAGE_SIZE).astype(f32)
                logits.append(jnp.sum(kb * ql, axis=1, keepdims=True) + (slope_pos - slope * page_dist))
        m = jnp.max(lo, axis=0, keepdims=True)
        for lg in logits:
            m = jnp.maximum(m, jnp.max(lg, axis=0, keepdims=True))
        p_own = jnp.exp(lo - m)
        den = jnp.sum(p_own, axis=0, keepdims=True)
        num = jnp.sum(p_own * vn, axis=0, keepdims=True)
        for i, lg in enumerate(logits):
            pi = jnp.exp(lg - m)
            den = den + jnp.sum(pi, axis=0, keepdims=True)
            num = num + jnp.sum(pi * vp[l * n_sel * PAGES_PER_BLOCK + i][...], axis=0, keepdims=True)
        rows.append(num / den)
    rows.append(jnp.zeros((8 - L, HEAD_DIM), f32))
    o_ref[0] = jnp.concatenate(rows, axis=0).astype(bf16)


def _sattn_sample(idx_flat, page_table, slopes, q8, k_new, v_new, cache_k, cache_v, *, layer, L, n_sel, past):
    DB = q8.shape[0]
    assert L <= 8
    nblk = L * n_sel * PAGES_PER_BLOCK
    return pl.pallas_call(
        functools.partial(_sattn_body, layer=layer, L=L, n_sel=n_sel, past=past),
        grid_spec=pltpu.PrefetchScalarGridSpec(
            num_scalar_prefetch=2,
            grid=(DB, H_ATT),
            in_specs=[
                pl.BlockSpec(memory_space=pltpu.SMEM),
                pl.BlockSpec((1, 8, HEAD_DIM), lambda d, h, idx, pt: (d, 0, h)),
                pl.BlockSpec((1, L, HEAD_DIM), lambda d, h, idx, pt: (d, 0, h)),
                pl.BlockSpec((1, L, HEAD_DIM), lambda d, h, idx, pt: (d, 0, h)),
                pl.BlockSpec(memory_space=pl.ANY),
                pl.BlockSpec(memory_space=pl.ANY),
            ],
            out_specs=pl.BlockSpec((1, 8, HEAD_DIM), lambda d, h, idx, pt: (d, 0, h)),
            scratch_shapes=[
                pltpu.VMEM((2, nblk, PAGE_SIZE, HEAD_DIM), f32),
                pltpu.VMEM((2, nblk, PAGE_SIZE, HEAD_DIM), f32),
                pltpu.SemaphoreType.DMA((2,)),
            ],
        ),
        out_shape=jax.ShapeDtypeStruct((DB, 8, H_ATT * HEAD_DIM), bf16),
        compiler_params=_params(("arbitrary", "arbitrary"), 32),
        name="sattn_sample",
    )(idx_flat, page_table, slopes, q8, k_new, v_new, cache_k, cache_v)


def _split3_dot(a_bf16, x):
    x1 = x.astype(bf16)
    r1 = x - x1.astype(f32)
    x2 = r1.astype(bf16)
    x3 = (r1 - x2.astype(f32)).astype(bf16)
    return _dot(a_bf16, x1) + _dot(a_bf16, x2) + _dot(a_bf16, x3)


def _mlstm_body(*refs, L, l_out, n_stream):
    if n_stream:
        refs = refs[1:]
        _kstream_reduce(refs[5:5 + n_stream], refs[5 + n_stream + 4])
        refs = refs[:5] + refs[5 + n_stream:5 + n_stream + 4] + refs[5 + n_stream + 5:]
    zm_ref, gate_ref, c0_ref, n0_ref, m0_ref, hm_ref, c_out, n_out, m_out, c_scr, n_scr, m_scr = refs
    c = pl.program_id(1)
    nc = pl.num_programs(1)

    @pl.when(c == 0)
    def _():
        c_scr[...] = c0_ref[0]
        n_scr[...] = n0_ref[0]
        m_scr[...] = m0_ref[0]

    G = gate_ref[0]
    row = _iota((L, L), 0)
    col = _iota((L, L), 1)
    causal = row >= col
    csum = _split3_dot(jnp.where(causal, 1.0, 0.0).astype(bf16), G)
    lane = _iota((L, LANES), 1)
    X = jnp.where(lane < H_MLSTM, pltpu.roll(csum, LANES - H_MLSTM, 1), pltpu.roll(G, H_MLSTM, 1) - csum)
    XT = X.T
    x1 = X.astype(bf16)
    r1 = X - x1.astype(f32)
    x2 = r1.astype(bf16)
    x3 = (r1 - x2.astype(f32)).astype(bf16)
    sel_row = _iota((LANES, LANES), 0)

    def replicated(src):
        e = jnp.where(sel_row == src, 1.0, 0.0).astype(bf16)
        return _dot(x1, e) + _dot(x2, e) + _dot(x3, e)

    def over_keys(col):
        return col[:, :L] if L <= LANES else jnp.concatenate([col] * (L // LANES), axis=1)

    ones_v = jnp.ones((L, HEAD_DIM), bf16)
    for h in range(H_MLSTM):
        sl = slice(h * HEAD_DIM, (h + 1) * HEAD_DIM)
        q = zm_ref[0, :, sl]
        k = zm_ref[0, :, SEG + h * HEAD_DIM:SEG + (h + 1) * HEAD_DIM]
        v = zm_ref[0, :, 2 * SEG + h * HEAD_DIM:2 * SEG + (h + 1) * HEAD_DIM]
        o = zm_ref[0, :, 3 * SEG + h * HEAD_DIM:3 * SEG + (h + 1) * HEAD_DIM]
        b_rep = replicated(h)
        g_rep = replicated(H_MLSTM + h)
        g_row = XT[H_MLSTM + h:H_MLSTM + h + 1, :]
        m_prev = m_scr[h:h + 1, :]
        g_max = jnp.max(jnp.where(causal, g_row, -jnp.inf), axis=1, keepdims=True)
        m_t = b_rep + jnp.maximum(m_prev, g_max)
        decay_d = jnp.exp(jnp.where(causal, g_row + over_keys(b_rep - m_t), -jnp.inf))
        s = _dot_nt(q, k) * decay_d
        a_inter = jnp.exp(b_rep + m_prev - m_t)
        c_h = c_scr[h]
        n_h = n_scr[h:h + 1, :]
        sv = _dot(s.astype(bf16), jnp.concatenate([v, ones_v], axis=1))
        n_rows = jnp.broadcast_to(n_h, (HEAD_DIM, HEAD_DIM)).astype(bf16)
        cq = _dot_nt(q, jnp.concatenate([c_h.astype(bf16), n_rows], axis=0))
        num = sv[:, :HEAD_DIM] + a_inter * cq[:, :HEAD_DIM]
        den = sv[:, HEAD_DIM:] + a_inter * cq[:, HEAD_DIM:]
        hh = num / jnp.maximum(jnp.abs(den), jnp.exp(-m_t))
        out = (jax.nn.sigmoid(o.astype(f32)) * hh).astype(bf16)
        hm_ref[0, :, sl] = out[:l_out]
        m_new = m_t[L - 1:L, :]
        b_last = b_rep[L - 1:L, :]
        decay = jnp.exp(b_last + m_prev - m_new)
        wk = jnp.exp(b_last + g_rep - m_new) * k.astype(f32)
        c_scr[h] = decay * c_h + _dot_tn(v, wk.astype(bf16))
        n_scr[h:h + 1, :] = decay * n_h + jnp.sum(wk, axis=0, keepdims=True)
        m_scr[h:h + 1, :] = m_new

    @pl.when(c == nc - 1)
    def _():
        c_out[0] = c_scr[...]
        n_out[0] = n_scr[...]
        m_out[0] = m_scr[...]


def _mlstm(zm, gates, c0, n0, m0, *, L, l_out, kstream=None):
    Bsz, T, _ = zm.shape
    nc = T // L
    in_specs = [
        pl.BlockSpec((1, L, 4 * SEG), lambda b, c, *_: (b, c, 0)),
        pl.BlockSpec((1, L, LANES), lambda b, c, *_: (b, c, 0)),
        pl.BlockSpec((1, H_MLSTM, HEAD_DIM, HEAD_DIM), lambda b, c, *_: (b, 0, 0, 0)),
        pl.BlockSpec((1, H_MLSTM, HEAD_DIM), lambda b, c, *_: (b, 0, 0)),
        pl.BlockSpec((1, H_MLSTM, LANES), lambda b, c, *_: (b, 0, 0)),
    ]
    out_specs = [
        pl.BlockSpec((1, l_out, SEG), lambda b, c, *_: (b, c, 0)),
        pl.BlockSpec((1, H_MLSTM, HEAD_DIM, HEAD_DIM), lambda b, c, *_: (b, 0, 0, 0)),
        pl.BlockSpec((1, H_MLSTM, HEAD_DIM), lambda b, c, *_: (b, 0, 0)),
        pl.BlockSpec((1, H_MLSTM, LANES), lambda b, c, *_: (b, 0, 0)),
    ]
    out_shape = [
        jax.ShapeDtypeStruct((Bsz, nc * l_out, SEG), bf16),
        jax.ShapeDtypeStruct((Bsz, H_MLSTM, HEAD_DIM, HEAD_DIM), f32),
        jax.ShapeDtypeStruct((Bsz, H_MLSTM, HEAD_DIM), f32),
        jax.ShapeDtypeStruct((Bsz, H_MLSTM, LANES), f32),
    ]
    scratch = [
        pltpu.VMEM((H_MLSTM, HEAD_DIM, HEAD_DIM), f32),
        pltpu.VMEM((H_MLSTM, HEAD_DIM), f32),
        pltpu.VMEM((H_MLSTM, LANES), f32),
    ]
    args = [zm, gates, c0, n0, m0]
    n_stream = 0
    if kstream is not None:
        page_specs, km_spec = _kstream_specs(kstream, lambda b, c: b * nc + c)
        n_stream = len(page_specs)
        in_specs += page_specs
        out_specs.append(km_spec)
        out_shape.append(jax.ShapeDtypeStruct((Bsz * nc * kstream.bps, H_ATT, HEAD_DIM), f32))
        args = [kstream.page_table] + args + [kstream.cache_k] * n_stream
    return pl.pallas_call(
        functools.partial(_mlstm_body, L=L, l_out=l_out, n_stream=n_stream),
        grid_spec=pltpu.PrefetchScalarGridSpec(
            num_scalar_prefetch=1 if n_stream else 0,
            grid=(Bsz, nc),
            in_specs=in_specs,
            out_specs=out_specs,
            scratch_shapes=scratch,
        ),
        out_shape=out_shape,
        compiler_params=_params(("arbitrary", "arbitrary"), 16 + n_stream),
        name="mlstm",
    )(*args)


def _outproj_body(x_ref, att_ref, hm_ref, wa_ref, wm_ref, o_ref):
    o_ref[...] = x_ref[...] + _dot(att_ref[...], wa_ref[...]) + _dot(hm_ref[...], wm_ref[...])


def _outproj(x, att, hm, w_out, *, tm, tn):
    M, D = x.shape
    return pl.pallas_call(
        _outproj_body,
        grid=(M // tm, D // tn),
        in_specs=[
            pl.BlockSpec((tm, tn), lambda m, n: (m, n)),
            pl.BlockSpec((tm, SEG), lambda m, n: (m, 0)),
            pl.BlockSpec((tm, SEG), lambda m, n: (m, 0)),
            pl.BlockSpec((SEG, tn), lambda m, n: (0, n)),
            pl.BlockSpec((SEG, tn), lambda m, n: (1, n)),
        ],
        out_specs=pl.BlockSpec((tm, tn), lambda m, n: (m, n)),
        out_shape=jax.ShapeDtypeStruct((M, D), f32),
        compiler_params=_params(("arbitrary", "arbitrary"), 40),
        name="outproj",
    )(x, att, hm, w_out, w_out)


def _ffn_body(*refs, tm, seq_len, carry_mode, n_stream):
    if n_stream:
        refs = refs[1:]
        _kstream_reduce(refs[7:7 + n_stream], refs[7 + n_stream + 2])
        refs = refs[:7] + refs[7 + n_stream:7 + n_stream + 2] + refs[7 + n_stream + 3:]
    if carry_mode:
        x_ref, g_ref, gf_ref, wg_ref, wu_ref, cw_ref, wd_ref, y_ref, tail_ref, h_scr, acc_scr, carry = refs
    else:
        x_ref, g_ref, gf_ref, wg_ref, wu_ref, cw_ref, wd_ref, p1_ref, p2_ref, y_ref, a_ref, h_scr, acc_scr = refs
    m = pl.program_id(0)
    f = pl.program_id(1)
    nf = pl.num_programs(1)

    @pl.when(f == 0)
    def _():
        h_scr[...] = _rmsnorm(x_ref[...], g_ref[...]).astype(bf16)
        acc_scr[...] = jnp.zeros(acc_scr.shape, f32)

    hb = h_scr[...]
    a = _dot(hb, wg_ref[...])
    u = _dot(hb, wu_ref[...])
    row = _iota(a.shape, 0)
    a1 = pltpu.roll(a, 1, 0)
    a2 = pltpu.roll(a, 2, 0)
    if carry_mode:
        prev = carry[f]
        prev = jnp.where(m % (seq_len // tm) == 0, 0.0, prev)
        a1 = jnp.where(row == 0, prev[7:8, :], a1)
        a2 = jnp.where(row == 0, prev[6:7, :], jnp.where(row == 1, prev[7:8, :], a2))
        carry[f] = a[tm - 8:, :]
        tail_ref[0] = a[tm - 8:, :]
    else:
        lpos = row % seq_len
        a1 = jnp.where(lpos >= 1, a1, p1_ref[...])
        a2 = jnp.where(lpos >= 2, a2, p2_ref[...])
        a_ref[...] = a
    cw = cw_ref[...]
    cv = cw[0:1, :] * a2 + cw[1:2, :] * a1 + cw[2:3, :] * a
    gl = 0.5 * cv * (1.0 + lax.erf(cv * (0.5 ** 0.5)))
    acc_scr[...] += _dot((gl * u).astype(bf16), wd_ref[...])

    @pl.when(f == nf - 1)
    def _():
        y_ref[...] = _rmsnorm(x_ref[...] + acc_scr[...], gf_ref[...])


def _ffn(x1, g_ffn, g_final, w_gate, w_up, conv_w, w_down, *, tm, tf, seq_len, prev=None, kstream=None):
    M, D = x1.shape
    DF = w_gate.shape[1]
    nf = DF // tf
    carry_mode = prev is None
    in_specs = [
        pl.BlockSpec((tm, D), lambda m, f, *_: (m, 0)),
        pl.BlockSpec((1, D), lambda m, f, *_: (0, 0)),
        pl.BlockSpec((1, D), lambda m, f, *_: (0, 0)),
        pl.BlockSpec((D, tf), lambda m, f, *_: (0, f)),
        pl.BlockSpec((D, tf), lambda m, f, *_: (0, f)),
        pl.BlockSpec((CONV_W, tf), lambda m, f, *_: (0, f)),
        pl.BlockSpec((tf, D), lambda m, f, *_: (f, 0)),
    ]
    args = [x1, g_ffn, g_final, w_gate, w_up, conv_w, w_down]
    scratch = [pltpu.VMEM((tm, D), bf16), pltpu.VMEM((tm, D), f32)]
    if carry_mode:
        assert seq_len % tm == 0
        out_specs = [pl.BlockSpec((tm, D), lambda m, f, *_: (m, 0)), pl.BlockSpec((1, 8, tf), lambda m, f, *_: (m, 0, f))]
        out_shape = [jax.ShapeDtypeStruct((M, D), f32), jax.ShapeDtypeStruct((M // tm, 8, DF), f32)]
        scratch.append(pltpu.VMEM((nf, 8, tf), f32))
    else:
        assert tm % seq_len == 0
        in_specs += [pl.BlockSpec((tm, tf), lambda m, f, *_: (m, f)), pl.BlockSpec((tm, tf), lambda m, f, *_: (m, f))]
        args += list(prev)
        out_specs = [pl.BlockSpec((tm, D), lambda m, f, *_: (m, 0)), pl.BlockSpec((tm, tf), lambda m, f, *_: (m, f))]
        out_shape = [jax.ShapeDtypeStruct((M, D), f32), jax.ShapeDtypeStruct((M, DF), f32)]
    n_stream = 0
    if kstream is not None:
        assert carry_mode
        page_specs, km_spec = _kstream_specs(kstream, lambda m, f: m * nf + f)
        n_stream = len(page_specs)
        in_specs += page_specs
        out_specs.append(km_spec)
        out_shape.append(jax.ShapeDtypeStruct((M // tm * nf * kstream.bps, H_ATT, HEAD_DIM), f32))
        args = [kstream.page_table] + args + [kstream.cache_k] * n_stream
    return pl.pallas_call(
        functools.partial(_ffn_body, tm=tm, seq_len=seq_len, carry_mode=carry_mode, n_stream=n_stream),
        grid_spec=pltpu.PrefetchScalarGridSpec(
            num_scalar_prefetch=1 if n_stream else 0,
            grid=(M // tm, nf),
            in_specs=in_specs,
            out_specs=out_specs,
            scratch_shapes=scratch,
        ),
        out_shape=out_shape,
        compiler_params=_params(("arbitrary", "arbitrary"), 48 + n_stream),
        name="convffn",
    )(*args)


def kernel(x_prompt, x_sample, cache_k, cache_v, state_C, state_n, state_m, state_conv, page_table,
           norm_mix, w_in, b_if, w_out, norm_ffn, w_gate, w_up, conv_w, w_down, norm_final):
    Bsz, T, D = x_prompt.shape
    DB, Ls, _ = x_sample.shape
    depth = w_in.shape[0]
    n_pages = page_table.shape[1]
    past = n_pages * PAGE_SIZE
    assert past % MOBA_BLOCK == 0 and past // MOBA_BLOCK >= 1, "sample path assumes a block-aligned, non-empty past"
    nbp = past // MOBA_BLOCK
    n_sel_s = min(MOBA_TOP_K, nbp)
    slopes = 2.0 ** (-8.0 * jnp.arange(1, H_ATT + 1, dtype=f32) / H_ATT)
    n_main = N_SEG * SEG
    TM_P, TM_S = 512, DB * Ls
    L_P, L_S = 256, 8
    TF = 512

    xp = x_prompt.reshape(Bsz * T, D)
    xs = x_sample.reshape(DB * Ls, D)
    st_p, st_s = [], []
    for l in range(depth):
        w_main, w_if = _cast_w_in(w_in, layer=l)
        bif = jnp.pad(b_if[l], (0, LANES - 2 * H_MLSTM)).reshape(1, LANES)
        wo = w_out[l].astype(bf16)
        wg, wu, wd = w_gate[l].astype(bf16), w_up[l].astype(bf16), w_down[l].astype(bf16)
        g_mix, g_ffn = norm_mix[l].reshape(1, D), norm_ffn[l].reshape(1, D)
        g_fin = norm_final.reshape(1, D) if l == depth - 1 else jnp.ones((1, D), f32)
        assert depth == 1, "final norm is fused into the last layer's FFN call"

        q, k, v, zm, gates, kb, vb, kmean = _inproj(xp, g_mix, w_main, w_if, bif, tm=TM_P, attn_aux=True)
        att = _moba_prompt(slopes, q.reshape(Bsz, T, SEG), kb.reshape(Bsz, T, SEG), vb.reshape(Bsz, T, SEG),
                           kmean.reshape(Bsz, T // MOBA_BLOCK, SEG))
        n_blk = DB * nbp
        n_ffn = min(n_blk, (Bsz * T // TM_P) * (wg.shape[1] // TF) * FFN_STREAM_BLOCKS)
        ml_steps = Bsz * T // L_P
        ks_ffn = _KStream(page_table, cache_k, l, 0, FFN_STREAM_BLOCKS, n_blk)
        ks_ml = _KStream(page_table, cache_k, l, n_ffn, max(1, -(-(n_blk - n_ffn) // ml_steps)), n_blk)
        hm, c_p, n_p, m_p, km_ml = _mlstm(
            zm.reshape(Bsz, T, 4 * SEG), gates.reshape(Bsz, T, LANES),
            jnp.zeros((Bsz, H_MLSTM, HEAD_DIM, HEAD_DIM), f32), jnp.zeros((Bsz, H_MLSTM, HEAD_DIM), f32),
            jnp.zeros((Bsz, H_MLSTM, LANES), f32), L=L_P, l_out=L_P, kstream=ks_ml)
        x1 = _outproj(xp, att.reshape(Bsz * T, SEG), hm.reshape(Bsz * T, SEG), wo, tm=TM_P, tn=1024)
        xp, tail, km_ffn = _ffn(x1, g_ffn, g_fin, wg, wu, conv_w[l], wd, tm=TM_P, tf=TF, seq_len=T, kstream=ks_ffn)
        kmean_s = jnp.concatenate([km_ffn[:n_ffn], km_ml[:n_blk - n_ffn]], axis=0).reshape(DB, nbp, H_ATT, HEAD_DIM)
        conv_p = tail.reshape(Bsz, T // TM_P, 8, -1)[:, -1, 8 - (CONV_W - 1):, :]
        st_p.append((k.reshape(Bsz, T, H_ATT, HEAD_DIM), v.reshape(Bsz, T, H_ATT, HEAD_DIM), c_p, n_p, m_p[:, :, 0], conv_p))

        q, k, v, zm, gates = _inproj(xs, g_mix, w_main, w_if, bif, tm=TM_S, attn_aux=False)
        q8 = jnp.pad(q.reshape(DB, Ls, SEG), ((0, 0), (0, 8 - Ls), (0, 0)))
        q_rank = _qsel(xs, g_mix, w_in, layer=l)
        q_rank = jnp.pad(q_rank.reshape(DB, Ls, H_ATT, HEAD_DIM), ((0, 0), (0, 8 - Ls), (0, 0), (0, 0)))
        idx = _select_sample(kmean_s, q_rank, L=Ls, n_sel=n_sel_s)
        idx = idx[:, :Ls * n_sel_s, :, 0].reshape(DB, Ls, n_sel_s, H_ATT)
        idx_flat = jnp.transpose(idx, (0, 1, 3, 2)).reshape(-1)
        att8 = _sattn_sample(idx_flat, page_table, slopes, q8, k.reshape(DB, Ls, SEG), v.reshape(DB, Ls, SEG),
                             cache_k, cache_v, layer=l, L=Ls, n_sel=n_sel_s, past=past)
        att = att8[:, :Ls].reshape(DB * Ls, SEG)
        pad_t = ((0, 0), (0, L_S - Ls), (0, 0))
        gate_pad = jnp.where(jnp.arange(LANES) < H_MLSTM, NEG, 0.0).astype(f32)
        gates_s = jnp.concatenate(
            [gates.reshape(DB, Ls, LANES), jnp.broadcast_to(gate_pad, (DB, L_S - Ls, LANES))], axis=1)
        hm8, c_s, n_s, m_s = _mlstm(
            jnp.pad(zm.reshape(DB, Ls, 4 * SEG), pad_t), gates_s, state_C[l], state_n[l],
            jnp.broadcast_to(state_m[l][:, :, None], (DB, H_MLSTM, LANES)), L=L_S, l_out=8)
        hm = hm8[:, :Ls].reshape(DB * Ls, SEG)
        x1 = _outproj(xs, att, hm, wo, tm=TM_S, tn=1024)
        buf = state_conv[l]
        zero = jnp.zeros_like(buf[:, 0])
        p1 = jnp.stack([buf[:, 1]] + [zero] * (Ls - 1), axis=1).reshape(DB * Ls, -1)
        p2 = jnp.stack([buf[:, 0], buf[:, 1]] + [zero] * (Ls - 2), axis=1).reshape(DB * Ls, -1)
        xs, a_full = _ffn(x1, g_ffn, g_fin, wg, wu, conv_w[l], wd, tm=TM_S, tf=TF, seq_len=Ls, prev=(p1, p2))
        conv_s = a_full.reshape(DB, Ls, -1)[:, Ls - (CONV_W - 1):]
        st_s.append((k.reshape(DB, Ls, H_ATT, HEAD_DIM), v.reshape(DB, Ls, H_ATT, HEAD_DIM), c_s, n_s, m_s[:, :, 0], conv_s))

    k_p, v_p, C_p, n_p, m_p, conv_p = [jnp.stack(z) for z in zip(*st_p)]
    k_s, v_s, C_s, n_s, m_s, conv_s = [jnp.stack(z) for z in zip(*st_s)]
    return (xp.reshape(Bsz, T, D), xs.reshape(DB, Ls, D), k_p, v_p, C_p, n_p, m_p, conv_p,
            k_s, v_s, C_s, n_s, m_s, conv_s)
```

```python
import functools
from typing import NamedTuple

import jax
import jax.numpy as jnp
from jax import lax
from jax.experimental import pallas as pl
from jax.experimental.pallas import tpu as pltpu

f32, bf16, i32 = jnp.float32, jnp.bfloat16, jnp.int32

HEAD_DIM = 128
H_ATT = 8
H_MLSTM = 8
SEG = H_ATT * HEAD_DIM
N_SEG = 7
CONV_W = 3
MOBA_BLOCK = 256
MOBA_TOP_K = 3
PAGE_SIZE = 128
PAGES_PER_BLOCK = MOBA_BLOCK // PAGE_SIZE
EPS = 1e-6
NEG = -1e30
LANES = 128
MAX_BLOCKS = 32
MIB = 1 << 20


def _params(sem, vmem_mib):
    return pltpu.CompilerParams(dimension_semantics=sem, vmem_limit_bytes=vmem_mib * MIB)


def _iota(shape, dim, dtype=i32):
    return lax.broadcasted_iota(dtype, shape, dim)


def _dot(a, b):
    return jnp.dot(a, b, preferred_element_type=f32)


def _dot_nt(a, b):
    return lax.dot_general(a, b, (((1,), (1,)), ((), ())), preferred_element_type=f32)


def _dot_tn(a, b):
    return lax.dot_general(a, b, (((0,), (0,)), ((), ())), preferred_element_type=f32)


def _rmsnorm(x, g):
    return x * lax.rsqrt(jnp.mean(x * x, axis=-1, keepdims=True) + EPS) * g


def _cast_w_in_body(w_ref, main_ref, wif_ref):
    n = pl.program_id(0)

    @pl.when(n < N_SEG)
    def _():
        main_ref[...] = w_ref[...].astype(bf16)

    @pl.when(n == N_SEG)
    def _():
        g = w_ref[:, 0:2 * H_MLSTM].astype(bf16)
        wif_ref[...] = jnp.concatenate([g, jnp.zeros((g.shape[0], LANES - 2 * H_MLSTM), bf16)], axis=1)


def _cast_w_in(w_in, *, layer):
    D = w_in.shape[1]
    assert w_in.shape[2] == N_SEG * SEG + 2 * H_MLSTM
    return pl.pallas_call(
        _cast_w_in_body,
        grid=(N_SEG + 1,),
        in_specs=[pl.BlockSpec((None, D, SEG), lambda n: (layer, 0, n))],
        out_specs=[pl.BlockSpec((D, SEG), lambda n: (0, jnp.minimum(n, N_SEG - 1))),
                   pl.BlockSpec((D, LANES), lambda n: (0, 0))],
        out_shape=[jax.ShapeDtypeStruct((D, N_SEG * SEG), bf16), jax.ShapeDtypeStruct((D, LANES), bf16)],
        compiler_params=_params(("arbitrary",), 40),
        name="cast_w_in",
    )(w_in)


def _inproj_body(x_ref, g_ref, w_ref, wif_ref, bif_ref, q_ref, k_ref, v_ref, zm_ref, gate_ref, *rest, attn_aux):
    if attn_aux:
        kb_ref, vb_ref, kmean_ref, h_scr = rest
    else:
        (h_scr,) = rest
    n = pl.program_id(1)
    scale = HEAD_DIM ** -0.5

    @pl.when(n == 0)
    def _():
        hb = _rmsnorm(x_ref[...], g_ref[...]).astype(bf16)
        h_scr[...] = hb
        zg = _dot(hb, wif_ref[...]) + bif_ref[...]
        lane = _iota(zg.shape, 1)
        log_sig = jnp.minimum(zg, 0.0) - jnp.log1p(jnp.exp(-jnp.abs(zg)))
        gate_ref[...] = jnp.where(lane < H_MLSTM, zg, jnp.where(lane < 2 * H_MLSTM, log_sig, 0.0))
        q_ref[...] = (_dot(hb, w_ref[...]) * scale).astype(bf16)

    def segment():
        return _dot(h_scr[...], w_ref[...])

    @pl.when(n == 1)
    def _():
        z = segment()
        k_ref[...] = z
        if attn_aux:
            kb_ref[...] = z.astype(bf16)
            for i in range(z.shape[0] // MOBA_BLOCK):
                blk = z[i * MOBA_BLOCK:(i + 1) * MOBA_BLOCK, :]
                kmean_ref[0, i:i + 1, :] = jnp.sum(blk, axis=0, keepdims=True) * (1.0 / MOBA_BLOCK)

    @pl.when(n == 2)
    def _():
        z = segment()
        v_ref[...] = z
        if attn_aux:
            vb_ref[...] = z.astype(bf16)

    @pl.when(n == 4)
    def _():
        zm_ref[...] = (segment() * scale).astype(bf16)

    @pl.when((n == 3) | (n >= 5))
    def _():
        zm_ref[...] = segment().astype(bf16)


def _inproj(x, g, w_main, w_if, b_if, *, tm, attn_aux):
    M, D = x.shape
    grid = (M // tm, N_SEG)

    def row(m, n):
        return (m, 0)

    out_specs = [
        pl.BlockSpec((tm, SEG), row),
        pl.BlockSpec((tm, SEG), row),
        pl.BlockSpec((tm, SEG), row),
        pl.BlockSpec((tm, SEG), lambda m, n: (m, jnp.maximum(n - 3, 0))),
        pl.BlockSpec((tm, LANES), row),
    ]
    out_shape = [
        jax.ShapeDtypeStruct((M, SEG), bf16),
        jax.ShapeDtypeStruct((M, SEG), f32),
        jax.ShapeDtypeStruct((M, SEG), f32),
        jax.ShapeDtypeStruct((M, 4 * SEG), bf16),
        jax.ShapeDtypeStruct((M, LANES), f32),
    ]
    if attn_aux:
        assert tm % MOBA_BLOCK == 0
        bpt = tm // MOBA_BLOCK
        out_specs += [pl.BlockSpec((tm, SEG), row), pl.BlockSpec((tm, SEG), row),
                      pl.BlockSpec((1, bpt, SEG), lambda m, n: (m, 0, 0))]
        out_shape += [jax.ShapeDtypeStruct((M, SEG), bf16), jax.ShapeDtypeStruct((M, SEG), bf16),
                      jax.ShapeDtypeStruct((M // tm, bpt, SEG), f32)]
    return pl.pallas_call(
        functools.partial(_inproj_body, attn_aux=attn_aux),
        grid=grid,
        in_specs=[
            pl.BlockSpec((tm, D), row),
            pl.BlockSpec((1, D), lambda m, n: (0, 0)),
            pl.BlockSpec((D, SEG), lambda m, n: (0, n)),
            pl.BlockSpec((D, LANES), lambda m, n: (0, 0)),
            pl.BlockSpec((1, LANES), lambda m, n: (0, 0)),
        ],
        out_specs=out_specs,
        out_shape=out_shape,
        scratch_shapes=[pltpu.VMEM((tm, D), bf16)],
        compiler_params=_params(("arbitrary", "arbitrary"), 52),
        name="inproj",
    )(x, g, w_main, w_if, b_if)


def _top_blocks_mask(scores_t, n_valid, own, n_sel):
    R = scores_t.shape[0]
    rown = _iota(scores_t.shape, 0)
    s = jnp.where(rown < n_valid, scores_t, -jnp.inf)
    mask = jnp.where(rown == own, 0.0, NEG)
    for _ in range(n_sel):
        mx = jnp.max(s, axis=0, keepdims=True)
        idx = jnp.min(jnp.where(s == mx, rown, R), axis=0, keepdims=True)
        pick = (rown == idx) & (mx > -jnp.inf)
        mask = jnp.where(pick, 0.0, mask)
        s = jnp.where(pick, -jnp.inf, s)
    return mask


ATT_HEADS = 2
ATT_CHUNK = 4


def _moba_prompt_body(slopes_ref, q_ref, kb_ref, vb_ref, kmean_ref, o_ref, kaug, vt, qaug, acc_scr, s_scr, ml_scr,
                      *, nb):
    hp = pl.program_id(1)
    qi = pl.program_id(2)
    B = MOBA_BLOCK
    CK = ATT_CHUNK * B
    heads = [slice(hh * HEAD_DIM, (hh + 1) * HEAD_DIM) for hh in range(ATT_HEADS)]

    @pl.when(qi == 0)
    def _init():
        lane = _iota((B, LANES), 1)
        s_lo = _iota((B, LANES), 0).astype(f32)
        for hh, cs in enumerate(heads):
            slope = slopes_ref[hp * ATT_HEADS + hh]
            for n in range(nb):
                c, r0 = n // ATT_CHUNK, (n % ATT_CHUNK) * B
                kaug[hh, c, r0:r0 + B, 0:HEAD_DIM] = kb_ref[0, n * B:(n + 1) * B, cs]
                ext = jnp.where(lane == n, 1.0, 0.0)
                ext = jnp.where(lane == MAX_BLOCKS, -slope, ext)
                ext = jnp.where(lane == MAX_BLOCKS + 1, slope * s_lo, ext)
                ext = jnp.where(lane == MAX_BLOCKS + 2, -slope * B, ext)
                ext = jnp.where(lane == MAX_BLOCKS + 3, slope * (B * n), ext)
                kaug[hh, c, r0:r0 + B, HEAD_DIM:2 * HEAD_DIM] = ext.astype(bf16)
                vt[hh, c, :, r0:r0 + B] = vb_ref[0, n * B:(n + 1) * B, cs].astype(f32).T.astype(bf16)

    c_own = qi // ATT_CHUNK
    n_ext = HEAD_DIM - MAX_BLOCKS
    rr = _iota((n_ext, B), 0)
    t_lo = _iota((n_ext, B), 1).astype(f32)
    q_ext = jnp.where(rr == 0, t_lo, jnp.where((rr == 1) | (rr == 3), 1.0, 0.0))
    q_ext = jnp.where(rr == 2, qi.astype(f32), q_ext).astype(bf16)
    for hh, cs in enumerate(heads):
        q_t = q_ref[0, :, cs].astype(f32).T.astype(bf16)
        scores_t = _dot(kmean_ref[0, :, cs].astype(bf16), q_t)
        mask_t = _top_blocks_mask(scores_t, qi, qi, min(MOBA_TOP_K, nb))
        if nb < MAX_BLOCKS:
            mask_t = jnp.concatenate([mask_t, jnp.full((MAX_BLOCKS - nb, B), NEG, f32)], axis=0)
        qaug[hh, 0:HEAD_DIM, :] = q_t
        qaug[hh, HEAD_DIM:HEAD_DIM + MAX_BLOCKS, :] = mask_t.astype(bf16)
        qaug[hh, HEAD_DIM + MAX_BLOCKS:, :] = q_ext
        s_scr[hh, 0] = _dot(kaug[hh, 0], qaug[hh])
        ml_scr[2 * hh:2 * hh + 1, :] = jnp.full((1, B), NEG, f32)
        ml_scr[2 * hh + 1:2 * hh + 2, :] = jnp.zeros((1, B), f32)
        acc_scr[hh] = jnp.zeros((HEAD_DIM, B), f32)

    def softmax_pv(hh, c, s):
        m = ml_scr[2 * hh:2 * hh + 1, :]
        m_new = jnp.maximum(m, jnp.max(s, axis=0, keepdims=True))
        alpha = jnp.exp(m - m_new)
        p = jnp.exp(s - m_new)
        ml_scr[2 * hh:2 * hh + 1, :] = m_new
        ml_scr[2 * hh + 1:2 * hh + 2, :] = alpha * ml_scr[2 * hh + 1:2 * hh + 2, :] + jnp.sum(p, axis=0, keepdims=True)
        acc_scr[hh] = alpha * acc_scr[hh] + _dot(vt[hh, c], p.astype(bf16))

    def stage(c, cur, nxt):
        for hh in range(ATT_HEADS):
            s_scr[hh, nxt] = _dot(kaug[hh, c + 1], qaug[hh])
        for hh in range(ATT_HEADS):
            softmax_pv(hh, c, s_scr[hh, cur])

    def body(c, carry):
        @pl.when(c % 2 == 0)
        def _():
            stage(c, 0, 1)

        @pl.when(c % 2 == 1)
        def _():
            stage(c, 1, 0)

        return carry

    lax.fori_loop(0, c_own, body, 0)
    own = qi - c_own * ATT_CHUNK
    for r in range(ATT_CHUNK):
        @pl.when(own == r)
        def _(r=r):
            nk = (r + 1) * B
            causal = (_iota((nk, B), 0) - _iota((nk, B), 1)) <= r * B
            for hh in range(ATT_HEADS):
                s = jnp.where(causal, s_scr[hh, c_own % 2, 0:nk, :], NEG)
                m = ml_scr[2 * hh:2 * hh + 1, :]
                m_new = jnp.maximum(m, jnp.max(s, axis=0, keepdims=True))
                alpha = jnp.exp(m - m_new)
                p = jnp.exp(s - m_new)
                ml_scr[2 * hh + 1:2 * hh + 2, :] = (alpha * ml_scr[2 * hh + 1:2 * hh + 2, :]
                                                    + jnp.sum(p, axis=0, keepdims=True))
                acc_scr[hh] = alpha * acc_scr[hh] + _dot(vt[hh, c_own, :, 0:nk], p.astype(bf16))

    for hh, cs in enumerate(heads):
        o_ref[0, :, cs] = (acc_scr[hh] / ml_scr[2 * hh + 1:2 * hh + 2, :]).T.astype(bf16)


def _moba_prompt(slopes, q, kb, vb, kmean):
    Bsz, T, _ = q.shape
    assert T % (MOBA_BLOCK * ATT_CHUNK) == 0 and H_ATT % ATT_HEADS == 0
    nb = T // MOBA_BLOCK
    assert nb <= MAX_BLOCKS
    blk = MOBA_BLOCK
    W = ATT_HEADS * HEAD_DIM
    nc = nb // ATT_CHUNK
    return pl.pallas_call(
        functools.partial(_moba_prompt_body, nb=nb),
        grid=(Bsz, H_ATT // ATT_HEADS, nb),
        in_specs=[
            pl.BlockSpec(memory_space=pltpu.SMEM),
            pl.BlockSpec((1, blk, W), lambda b, h, i: (b, i, h)),
            pl.BlockSpec((1, T, W), lambda b, h, i: (b, 0, h)),
            pl.BlockSpec((1, T, W), lambda b, h, i: (b, 0, h)),
            pl.BlockSpec((1, nb, W), lambda b, h, i: (b, 0, h)),
        ],
        out_specs=pl.BlockSpec((1, blk, W), lambda b, h, i: (b, i, h)),
        out_shape=jax.ShapeDtypeStruct((Bsz, T, H_ATT * HEAD_DIM), bf16),
        scratch_shapes=[
            pltpu.VMEM((ATT_HEADS, nc, ATT_CHUNK * blk, 2 * HEAD_DIM), bf16),
            pltpu.VMEM((ATT_HEADS, nc, HEAD_DIM, ATT_CHUNK * blk), bf16),
            pltpu.VMEM((ATT_HEADS, 2 * HEAD_DIM, blk), bf16),
            pltpu.VMEM((ATT_HEADS, HEAD_DIM, blk), f32),
            pltpu.VMEM((ATT_HEADS, 2, ATT_CHUNK * blk, blk), f32),
            pltpu.VMEM((2 * ATT_HEADS, blk), f32),
        ],
        compiler_params=_params(("arbitrary", "arbitrary", "arbitrary"), 52),
        name="moba_prompt",
    )(slopes, q, kb, vb, kmean)


FFN_STREAM_BLOCKS = 3


class _KStream(NamedTuple):
    page_table: jax.Array
    cache_k: jax.Array
    layer: int
    base: int
    bps: int
    total: int


def _kstream_specs(ks, step_of):
    nbp = ks.page_table.shape[1] // PAGES_PER_BLOCK

    def page_map(*ids, i):
        g = jnp.minimum(ks.base + step_of(*ids[:-1]) * ks.bps + i // PAGES_PER_BLOCK, ks.total - 1)
        return (ks.layer, ids[-1][g // nbp, (g % nbp) * PAGES_PER_BLOCK + i % PAGES_PER_BLOCK], 0, 0, 0)

    in_specs = [pl.BlockSpec((None, None, PAGE_SIZE, H_ATT, HEAD_DIM), functools.partial(page_map, i=i))
                for i in range(ks.bps * PAGES_PER_BLOCK)]
    out_spec = pl.BlockSpec((ks.bps, H_ATT, HEAD_DIM), lambda *ids: (step_of(*ids[:-1]), 0, 0))
    return in_specs, out_spec


def _kstream_reduce(pages, out_ref):
    for b in range(len(pages) // PAGES_PER_BLOCK):
        s = jnp.sum(pages[PAGES_PER_BLOCK * b][...], axis=0)
        for p in range(1, PAGES_PER_BLOCK):
            s = s + jnp.sum(pages[PAGES_PER_BLOCK * b + p][...], axis=0)
        out_ref[b] = s * (1.0 / MOBA_BLOCK)


def _qsel_body(x_ref, g_ref, w_ref, o_ref):
    h = _rmsnorm(x_ref[...], g_ref[...])
    o_ref[...] = jnp.dot(h, w_ref[...], preferred_element_type=f32, precision=lax.Precision.HIGHEST)


def _qsel(x, g, w_in, *, layer, tn=256):
    M, D = x.shape
    return pl.pallas_call(
        _qsel_body,
        grid=(SEG // tn,),
        in_specs=[
            pl.BlockSpec((M, D), lambda n: (0, 0)),
            pl.BlockSpec((1, D), lambda n: (0, 0)),
            pl.BlockSpec((None, D, tn), lambda n: (layer, 0, n)),
        ],
        out_specs=pl.BlockSpec((M, tn), lambda n: (0, n)),
        out_shape=jax.ShapeDtypeStruct((M, SEG), f32),
        compiler_params=_params(("arbitrary",), 32),
        name="qsel",
    )(x, g, w_in)


IDX_ROWS = 16


def _select_body(km_ref, q_ref, idx_ref, *, L, n_sel):
    nbp = km_ref.shape[0]
    idx_ref[...] = jnp.zeros(idx_ref.shape, i32)
    km = km_ref[...]
    blk = _iota((nbp, H_ATT, 1), 0)
    for l in range(L):
        sc = jnp.sum(km * q_ref[l][None], axis=-1, keepdims=True)
        for r in range(n_sel):
            mx = jnp.max(sc, axis=0, keepdims=True)
            idx = jnp.min(jnp.where(sc == mx, blk, nbp), axis=0, keepdims=True)
            idx_ref[l * n_sel + r] = jnp.broadcast_to(idx[0], (H_ATT, LANES))
            sc = jnp.where(blk == idx, -jnp.inf, sc)


def _select_sample(kmean, q8, *, L, n_sel):
    DB, nbp = kmean.shape[:2]
    assert L * n_sel <= IDX_ROWS
    return pl.pallas_call(
        functools.partial(_select_body, L=L, n_sel=n_sel),
        grid=(DB,),
        in_specs=[
            pl.BlockSpec((None, nbp, H_ATT, HEAD_DIM), lambda d: (d, 0, 0, 0)),
            pl.BlockSpec((None, 8, H_ATT, HEAD_DIM), lambda d: (d, 0, 0, 0)),
        ],
        out_specs=pl.BlockSpec((None, IDX_ROWS, H_ATT, LANES), lambda d: (d, 0, 0, 0)),
        out_shape=jax.ShapeDtypeStruct((DB, IDX_ROWS, H_ATT, LANES), i32),
        compiler_params=_params(("arbitrary",), 32),
        name="select_sample",
    )(kmean, q8)


def _sattn_body(idx_ref, pt_ref, slopes_ref, q_ref, kn_ref, vn_ref, ck_hbm, cv_hbm, o_ref, kbuf, vbuf, sem,
                *, layer, L, n_sel, past):
    nblk = L * n_sel * PAGES_PER_BLOCK
    d = pl.program_id(0)
    h = pl.program_id(1)
    step = d * H_ATT + h
    slot = step % 2

    def page_copies(dd, hh, sl):
        out = []
        for i in range(nblk):
            lj, p = i // PAGES_PER_BLOCK, i % PAGES_PER_BLOCK
            blk = idx_ref[((dd * L + lj // n_sel) * H_ATT + hh) * n_sel + lj % n_sel]
            phys = pt_ref[dd, blk * PAGES_PER_BLOCK + p]
            out.append(pltpu.make_async_copy(ck_hbm.at[layer, phys, :, hh, :], kbuf.at[sl, i], sem.at[sl]))
            out.append(pltpu.make_async_copy(cv_hbm.at[layer, phys, :, hh, :], vbuf.at[sl, i], sem.at[sl]))
        return out

    @pl.when(step == 0)
    def _():
        for cp in page_copies(d, h, slot):
            cp.start()

    @pl.when(step + 1 < pl.num_programs(0) * H_ATT)
    def _():
        nxt = step + 1
        for cp in page_copies(nxt // H_ATT, nxt % H_ATT, 1 - slot):
            cp.start()

    for cp in page_copies(d, h, slot):
        cp.wait()

    kp = [kbuf.at[slot, i] for i in range(nblk)]
    vp = [vbuf.at[slot, i] for i in range(nblk)]
    slope = slopes_ref[h]
    qf = q_ref[0].astype(f32)
    kn = kn_ref[0]
    vn = vn_ref[0]
    lpos = _iota((L, 1), 0)
    slope_pos = slope * _iota((PAGE_SIZE, 1), 0).astype(f32)
    rows = []
    for l in range(L):
        ql = qf[l:l + 1, :]
        lo = jnp.sum(kn * ql, axis=1, keepdims=True)
        lo = jnp.where(lpos <= l, lo - slope * (l - lpos).astype(f32), NEG)
        logits = []
        for j in range(n_sel):
            blk = idx_ref[((d * L + l) * H_ATT + h) * n_sel + j]
            for p in range(PAGES_PER_BLOCK):
                kb = kp[(l * n_sel + j) * PAGES_PER_BLOCK + p][...]
                page_dist = (past + l - blk * MOBA_BLOCK - p * PAGE_SIZE).astype(f32)
                logits.append(jnp.sum(kb * ql, axis=1, keepdims=True) + (slope_pos - slope * page_dist))
        m = jnp.max(lo, axis=0, keepdims=True)
        for lg in logits:
            m = jnp.maximum(m, jnp.max(lg, axis=0, keepdims=True))
        p_own = jnp.exp(lo - m)
        den = jnp.sum(p_own, axis=0, keepdims=True)
        num = jnp.sum(p_own * vn, axis=0, keepdims=True)
        for i, lg in enumerate(logits):
            pi = jnp.exp(lg - m)
            den = den + jnp.sum(pi, axis=0, keepdims=True)
            num = num + jnp.sum(pi * vp[l * n_sel * PAGES_PER_BLOCK + i][...], axis=0, keepdims=True)
        rows.append(num / den)
    rows.append(jnp.zeros((8 - L, HEAD_DIM), f32))
    o_ref[0] = jnp.concatenate(rows, axis=0).astype(bf16)


def _sattn_sample(idx_flat, page_table, slopes, q8, k_new, v_new, cache_k, cache_v, *, layer, L, n_sel, past):
    DB = q8.shape[0]
    assert L <= 8
    nblk = L * n_sel * PAGES_PER_BLOCK
    return pl.pallas_call(
        functools.partial(_sattn_body, layer=layer, L=L, n_sel=n_sel, past=past),
        grid_spec=pltpu.PrefetchScalarGridSpec(
            num_scalar_prefetch=2,
            grid=(DB, H_ATT),
            in_specs=[
                pl.BlockSpec(memory_space=pltpu.SMEM),
                pl.BlockSpec((1, 8, HEAD_DIM), lambda d, h, idx, pt: (d, 0, h)),
                pl.BlockSpec((1, L, HEAD_DIM), lambda d, h, idx, pt: (d, 0, h)),
                pl.BlockSpec((1, L, HEAD_DIM), lambda d, h, idx, pt: (d, 0, h)),
                pl.BlockSpec(memory_space=pl.ANY),
                pl.BlockSpec(memory_space=pl.ANY),
            ],
            out_specs=pl.BlockSpec((1, 8, HEAD_DIM), lambda d, h, idx, pt: (d, 0, h)),
            scratch_shapes=[
                pltpu.VMEM((2, nblk, PAGE_SIZE, HEAD_DIM), f32),
                pltpu.VMEM((2, nblk, PAGE_SIZE, HEAD_DIM), f32),
                pltpu.SemaphoreType.DMA((2,)),
            ],
        ),
        out_shape=jax.ShapeDtypeStruct((DB, 8, H_ATT * HEAD_DIM), bf16),
        compiler_params=_params(("arbitrary", "arbitrary"), 32),
        name="sattn_sample",
    )(idx_flat, page_table, slopes, q8, k_new, v_new, cache_k, cache_v)


def _split3_dot(a_bf16, x):
    x1 = x.astype(bf16)
    r1 = x - x1.astype(f32)
    x2 = r1.astype(bf16)
    x3 = (r1 - x2.astype(f32)).astype(bf16)
    return _dot(a_bf16, x1) + _dot(a_bf16, x2) + _dot(a_bf16, x3)


def _mlstm_body(*refs, L, l_out, n_stream):
    if n_stream:
        refs = refs[1:]
        _kstream_reduce(refs[5:5 + n_stream], refs[5 + n_stream + 4])
        refs = refs[:5] + refs[5 + n_stream:5 + n_stream + 4] + refs[5 + n_stream + 5:]
    zm_ref, gate_ref, c0_ref, n0_ref, m0_ref, hm_ref, c_out, n_out, m_out, c_scr, n_scr, m_scr = refs
    c = pl.program_id(1)
    nc = pl.num_programs(1)

    @pl.when(c == 0)
    def _():
        c_scr[...] = c0_ref[0]
        n_scr[...] = n0_ref[0]
        m_scr[...] = m0_ref[0]

    G = gate_ref[0]
    row = _iota((L, L), 0)
    col = _iota((L, L), 1)
    causal = row >= col
    csum = _split3_dot(jnp.where(causal, 1.0, 0.0).astype(bf16), G)
    lane = _iota((L, LANES), 1)
    X = jnp.where(lane < H_MLSTM, pltpu.roll(csum, LANES - H_MLSTM, 1), pltpu.roll(G, H_MLSTM, 1) - csum)
    XT = X.T
    x1 = X.astype(bf16)
    r1 = X - x1.astype(f32)
    x2 = r1.astype(bf16)
    x3 = (r1 - x2.astype(f32)).astype(bf16)
    sel_row = _iota((LANES, LANES), 0)

    def replicated(src):
        e = jnp.where(sel_row == src, 1.0, 0.0).astype(bf16)
        return _dot(x1, e) + _dot(x2, e) + _dot(x3, e)

    def over_keys(col):
        return col[:, :L] if L <= LANES else jnp.concatenate([col] * (L // LANES), axis=1)

    ones_v = jnp.ones((L, HEAD_DIM), bf16)
    for h in range(H_MLSTM):
        sl = slice(h * HEAD_DIM, (h + 1) * HEAD_DIM)
        q = zm_ref[0, :, sl]
        k = zm_ref[0, :, SEG + h * HEAD_DIM:SEG + (h + 1) * HEAD_DIM]
        v = zm_ref[0, :, 2 * SEG + h * HEAD_DIM:2 * SEG + (h + 1) * HEAD_DIM]
        o = zm_ref[0, :, 3 * SEG + h * HEAD_DIM:3 * SEG + (h + 1) * HEAD_DIM]
        b_rep = replicated(h)
        g_rep = replicated(H_MLSTM + h)
        g_row = XT[H_MLSTM + h:H_MLSTM + h + 1, :]
        m_prev = m_scr[h:h + 1, :]
        g_max = jnp.max(jnp.where(causal, g_row, -jnp.inf), axis=1, keepdims=True)
        m_t = b_rep + jnp.maximum(m_prev, g_max)
        decay_d = jnp.exp(jnp.where(causal, g_row + over_keys(b_rep - m_t), -jnp.inf))
        s = _dot_nt(q, k) * decay_d
        a_inter = jnp.exp(b_rep + m_prev - m_t)
        c_h = c_scr[h]
        n_h = n_scr[h:h + 1, :]
        sv = _dot(s.astype(bf16), jnp.concatenate([v, ones_v], axis=1))
        n_rows = jnp.broadcast_to(n_h, (HEAD_DIM, HEAD_DIM)).astype(bf16)
        cq = _dot_nt(q, jnp.concatenate([c_h.astype(bf16), n_rows], axis=0))
        num = sv[:, :HEAD_DIM] + a_inter * cq[:, :HEAD_DIM]
        den = sv[:, HEAD_DIM:] + a_inter * cq[:, HEAD_DIM:]
        hh = num / jnp.maximum(jnp.abs(den), jnp.exp(-m_t))
        out = (jax.nn.sigmoid(o.astype(f32)) * hh).astype(bf16)
        hm_ref[0, :, sl] = out[:l_out]
        m_new = m_t[L - 1:L, :]
        b_last = b_rep[L - 1:L, :]
        decay = jnp.exp(b_last + m_prev - m_new)
        wk = jnp.exp(b_last + g_rep - m_new) * k.astype(f32)
        c_scr[h] = decay * c_h + _dot_tn(v, wk.astype(bf16))
        n_scr[h:h + 1, :] = decay * n_h + jnp.sum(wk, axis=0, keepdims=True)
        m_scr[h:h + 1, :] = m_new

    @pl.when(c == nc - 1)
    def _():
        c_out[0] = c_scr[...]
        n_out[0] = n_scr[...]
        m_out[0] = m_scr[...]


def _mlstm(zm, gates, c0, n0, m0, *, L, l_out, kstream=None):
    Bsz, T, _ = zm.shape
    nc = T // L
    in_specs = [
        pl.BlockSpec((1, L, 4 * SEG), lambda b, c, *_: (b, c, 0)),
        pl.BlockSpec((1, L, LANES), lambda b, c, *_: (b, c, 0)),
        pl.BlockSpec((1, H_MLSTM, HEAD_DIM, HEAD_DIM), lambda b, c, *_: (b, 0, 0, 0)),
        pl.BlockSpec((1, H_MLSTM, HEAD_DIM), lambda b, c, *_: (b, 0, 0)),
        pl.BlockSpec((1, H_MLSTM, LANES), lambda b, c, *_: (b, 0, 0)),
    ]
    out_specs = [
        pl.BlockSpec((1, l_out, SEG), lambda b, c, *_: (b, c, 0)),
        pl.BlockSpec((1, H_MLSTM, HEAD_DIM, HEAD_DIM), lambda b, c, *_: (b, 0, 0, 0)),
        pl.BlockSpec((1, H_MLSTM, HEAD_DIM), lambda b, c, *_: (b, 0, 0)),
        pl.BlockSpec((1, H_MLSTM, LANES), lambda b, c, *_: (b, 0, 0)),
    ]
    out_shape = [
        jax.ShapeDtypeStruct((Bsz, nc * l_out, SEG), bf16),
        jax.ShapeDtypeStruct((Bsz, H_MLSTM, HEAD_DIM, HEAD_DIM), f32),
        jax.ShapeDtypeStruct((Bsz, H_MLSTM, HEAD_DIM), f32),
        jax.ShapeDtypeStruct((Bsz, H_MLSTM, LANES), f32),
    ]
    scratch = [
        pltpu.VMEM((H_MLSTM, HEAD_DIM, HEAD_DIM), f32),
        pltpu.VMEM((H_MLSTM, HEAD_DIM), f32),
        pltpu.VMEM((H_MLSTM, LANES), f32),
    ]
    args = [zm, gates, c0, n0, m0]
    n_stream = 0
    if kstream is not None:
        page_specs, km_spec = _kstream_specs(kstream, lambda b, c: b * nc + c)
        n_stream = len(page_specs)
        in_specs += page_specs
        out_specs.append(km_spec)
        out_shape.append(jax.ShapeDtypeStruct((Bsz * nc * kstream.bps, H_ATT, HEAD_DIM), f32))
        args = [kstream.page_table] + args + [kstream.cache_k] * n_stream
    return pl.pallas_call(
        functools.partial(_mlstm_body, L=L, l_out=l_out, n_stream=n_stream),
        grid_spec=pltpu.PrefetchScalarGridSpec(
            num_scalar_prefetch=1 if n_stream else 0,
            grid=(Bsz, nc),
            in_specs=in_specs,
            out_specs=out_specs,
            scratch_shapes=scratch,
        ),
        out_shape=out_shape,
        compiler_params=_params(("arbitrary", "arbitrary"), 16 + n_stream),
        name="mlstm",
    )(*args)


def _outproj_body(x_ref, att_ref, hm_ref, wa_ref, wm_ref, o_ref):
    o_ref[...] = x_ref[...] + _dot(att_ref[...], wa_ref[...]) + _dot(hm_ref[...], wm_ref[...])


def _outproj(x, att, hm, w_out, *, tm, tn):
    M, D = x.shape
    return pl.pallas_call(
        _outproj_body,
        grid=(M // tm, D // tn),
        in_specs=[
            pl.BlockSpec((tm, tn), lambda m, n: (m, n)),
            pl.BlockSpec((tm, SEG), lambda m, n: (m, 0)),
            pl.BlockSpec((tm, SEG), lambda m, n: (m, 0)),
            pl.BlockSpec((SEG, tn), lambda m, n: (0, n)),
            pl.BlockSpec((SEG, tn), lambda m, n: (1, n)),
        ],
        out_specs=pl.BlockSpec((tm, tn), lambda m, n: (m, n)),
        out_shape=jax.ShapeDtypeStruct((M, D), f32),
        compiler_params=_params(("arbitrary", "arbitrary"), 40),
        name="outproj",
    )(x, att, hm, w_out, w_out)


def _ffn_body(*refs, tm, seq_len, carry_mode, n_stream):
    if n_stream:
        refs = refs[1:]
        _kstream_reduce(refs[7:7 + n_stream], refs[7 + n_stream + 2])
        refs = refs[:7] + refs[7 + n_stream:7 + n_stream + 2] + refs[7 + n_stream + 3:]
    if carry_mode:
        x_ref, g_ref, gf_ref, wg_ref, wu_ref, cw_ref, wd_ref, y_ref, tail_ref, h_scr, acc_scr, carry = refs
    else:
        x_ref, g_ref, gf_ref, wg_ref, wu_ref, cw_ref, wd_ref, p1_ref, p2_ref, y_ref, a_ref, h_scr, acc_scr = refs
    m = pl.program_id(0)
    f = pl.program_id(1)
    nf = pl.num_programs(1)

    @pl.when(f == 0)
    def _():
        h_scr[...] = _rmsnorm(x_ref[...], g_ref[...]).astype(bf16)
        acc_scr[...] = jnp.zeros(acc_scr.shape, f32)

    hb = h_scr[...]
    a = _dot(hb, wg_ref[...])
    u = _dot(hb, wu_ref[...])
    row = _iota(a.shape, 0)
    a1 = pltpu.roll(a, 1, 0)
    a2 = pltpu.roll(a, 2, 0)
    if carry_mode:
        prev = carry[f]
        prev = jnp.where(m % (seq_len // tm) == 0, 0.0, prev)
        a1 = jnp.where(row == 0, prev[7:8, :], a1)
        a2 = jnp.where(row == 0, prev[6:7, :], jnp.where(row == 1, prev[7:8, :], a2))
        carry[f] = a[tm - 8:, :]
        tail_ref[0] = a[tm - 8:, :]
    else:
        lpos = row % seq_len
        a1 = jnp.where(lpos >= 1, a1, p1_ref[...])
        a2 = jnp.where(lpos >= 2, a2, p2_ref[...])
        a_ref[...] = a
    cw = cw_ref[...]
    cv = cw[0:1, :] * a2 + cw[1:2, :] * a1 + cw[2:3, :] * a
    gl = 0.5 * cv * (1.0 + lax.erf(cv * (0.5 ** 0.5)))
    acc_scr[...] += _dot((gl * u).astype(bf16), wd_ref[...])

    @pl.when(f == nf - 1)
    def _():
        y_ref[...] = _rmsnorm(x_ref[...] + acc_scr[...], gf_ref[...])


def _ffn(x1, g_ffn, g_final, w_gate, w_up, conv_w, w_down, *, tm, tf, seq_len, prev=None, kstream=None):
    M, D = x1.shape
    DF = w_gate.shape[1]
    nf = DF // tf
    carry_mode = prev is None
    in_specs = [
        pl.BlockSpec((tm, D), lambda m, f, *_: (m, 0)),
        pl.BlockSpec((1, D), lambda m, f, *_: (0, 0)),
        pl.BlockSpec((1, D), lambda m, f, *_: (0, 0)),
        pl.BlockSpec((D, tf), lambda m, f, *_: (0, f)),
        pl.BlockSpec((D, tf), lambda m, f, *_: (0, f)),
        pl.BlockSpec((CONV_W, tf), lambda m, f, *_: (0, f)),
        pl.BlockSpec((tf, D), lambda m, f, *_: (f, 0)),
    ]
    args = [x1, g_ffn, g_final, w_gate, w_up, conv_w, w_down]
    scratch = [pltpu.VMEM((tm, D), bf16), pltpu.VMEM((tm, D), f32)]
    if carry_mode:
        assert seq_len % tm == 0
        out_specs = [pl.BlockSpec((tm, D), lambda m, f, *_: (m, 0)), pl.BlockSpec((1, 8, tf), lambda m, f, *_: (m, 0, f))]
        out_shape = [jax.ShapeDtypeStruct((M, D), f32), jax.ShapeDtypeStruct((M // tm, 8, DF), f32)]
        scratch.append(pltpu.VMEM((nf, 8, tf), f32))
    else:
        assert tm % seq_len == 0
        in_specs += [pl.BlockSpec((tm, tf), lambda m, f, *_: (m, f)), pl.BlockSpec((tm, tf), lambda m, f, *_: (m, f))]
        args += list(prev)
        out_specs = [pl.BlockSpec((tm, D), lambda m, f, *_: (m, 0)), pl.BlockSpec((tm, tf), lambda m, f, *_: (m, f))]
        out_shape = [jax.ShapeDtypeStruct((M, D), f32), jax.ShapeDtypeStruct((M, DF), f32)]
    n_stream = 0
    if kstream is not None:
        assert carry_mode
        page_specs, km_spec = _kstream_specs(kstream, lambda m, f: m * nf + f)
        n_stream = len(page_specs)
        in_specs += page_specs
        out_specs.append(km_spec)
        out_shape.append(jax.ShapeDtypeStruct((M // tm * nf * kstream.bps, H_ATT, HEAD_DIM), f32))
        args = [kstream.page_table] + args + [kstream.cache_k] * n_stream
    return pl.pallas_call(
        functools.partial(_ffn_body, tm=tm, seq_len=seq_len, carry_mode=carry_mode, n_stream=n_stream),
        grid_spec=pltpu.PrefetchScalarGridSpec(
            num_scalar_prefetch=1 if n_stream else 0,
            grid=(M // tm, nf),
            in_specs=in_specs,
            out_specs=out_specs,
            scratch_shapes=scratch,
        ),
        out_shape=out_shape,
        compiler_params=_params(("arbitrary", "arbitrary"), 48 + n_stream),
        name="convffn",
    )(*args)


def kernel(x_prompt, x_sample, cache_k, cache_v, state_C, state_n, state_m, state_conv, page_table,
           norm_mix, w_in, b_if, w_out, norm_ffn, w_gate, w_up, conv_w, w_down, norm_final):
    Bsz, T, D = x_prompt.shape
    DB, Ls, _ = x_sample.shape
    depth = w_in.shape[0]
    n_pages = page_table.shape[1]
    past = n_pages * PAGE_SIZE
    assert past % MOBA_BLOCK == 0 and past // MOBA_BLOCK >= 1, "sample path assumes a block-aligned, non-empty past"
    nbp = past // MOBA_BLOCK
    n_sel_s = min(MOBA_TOP_K, nbp)
    slopes = 2.0 ** (-8.0 * jnp.arange(1, H_ATT + 1, dtype=f32) / H_ATT)
    n_main = N_SEG * SEG
    TM_P, TM_S = 512, DB * Ls
    L_P, L_S = 256, 8
    TF = 512

    xp = x_prompt.reshape(Bsz * T, D)
    xs = x_sample.reshape(DB * Ls, D)
    st_p, st_s = [], []
    for l in range(depth):
        w_main, w_if = _cast_w_in(w_in, layer=l)
        bif = jnp.pad(b_if[l], (0, LANES - 2 * H_MLSTM)).reshape(1, LANES)
        wo = w_out[l].astype(bf16)
        wg, wu, wd = w_gate[l].astype(bf16), w_up[l].astype(bf16), w_down[l].astype(bf16)
        g_mix, g_ffn = norm_mix[l].reshape(1, D), norm_ffn[l].reshape(1, D)
        g_fin = norm_final.reshape(1, D) if l == depth - 1 else jnp.ones((1, D), f32)
        assert depth == 1, "final norm is fused into the last layer's FFN call"

        q, k, v, zm, gates, kb, vb, kmean = _inproj(xp, g_mix, w_main, w_if, bif, tm=TM_P, attn_aux=True)
        att = _moba_prompt(slopes, q.reshape(Bsz, T, SEG), kb.reshape(Bsz, T, SEG), vb.reshape(Bsz, T, SEG),
                           kmean.reshape(Bsz, T // MOBA_BLOCK, SEG))
        n_blk = DB * nbp
        n_ffn = min(n_blk, (Bsz * T // TM_P) * (wg.shape[1] // TF) * FFN_STREAM_BLOCKS)
        ml_steps = Bsz * T // L_P
        ks_ffn = _KStream(page_table, cache_k, l, 0, FFN_STREAM_BLOCKS, n_blk)
        ks_ml = _KStream(page_table, cache_k, l, n_ffn, max(1, -(-(n_blk - n_ffn) // ml_steps)), n_blk)
        hm, c_p, n_p, m_p, km_ml = _mlstm(
            zm.reshape(Bsz, T, 4 * SEG), gates.reshape(Bsz, T, LANES),
            jnp.zeros((Bsz, H_MLSTM, HEAD_DIM, HEAD_DIM), f32), jnp.zeros((Bsz, H_MLSTM, HEAD_DIM), f32),
            jnp.zeros((Bsz, H_MLSTM, LANES), f32), L=L_P, l_out=L_P, kstream=ks_ml)
        x1 = _outproj(xp, att.reshape(Bsz * T, SEG), hm.reshape(Bsz * T, SEG), wo, tm=TM_P, tn=1024)
        xp, tail, km_ffn = _ffn(x1, g_ffn, g_fin, wg, wu, conv_w[l], wd, tm=TM_P, tf=TF, seq_len=T, kstream=ks_ffn)
        kmean_s = jnp.concatenate([km_ffn[:n_ffn], km_ml[:n_blk - n_ffn]], axis=0).reshape(DB, nbp, H_ATT, HEAD_DIM)
        conv_p = tail.reshape(Bsz, T // TM_P, 8, -1)[:, -1, 8 - (CONV_W - 1):, :]
        st_p.append((k.reshape(Bsz, T, H_ATT, HEAD_DIM), v.reshape(Bsz, T, H_ATT, HEAD_DIM), c_p, n_p, m_p[:, :, 0], conv_p))

        q, k, v, zm, gates = _inproj(xs, g_mix, w_main, w_if, bif, tm=TM_S, attn_aux=False)
        q8 = jnp.pad(q.reshape(DB, Ls, SEG), ((0, 0), (0, 8 - Ls), (0, 0)))
        q_rank = _qsel(xs, g_mix, w_in, layer=l)
        q_rank = jnp.pad(q_rank.reshape(DB, Ls, H_ATT, HEAD_DIM), ((0, 0), (0, 8 - Ls), (0, 0), (0, 0)))
        idx = _select_sample(kmean_s, q_rank, L=Ls, n_sel=n_sel_s)
        idx = idx[:, :Ls * n_sel_s, :, 0].reshape(DB, Ls, n_sel_s, H_ATT)
        idx_flat = jnp.transpose(idx, (0, 1, 3, 2)).reshape(-1)
        att8 = _sattn_sample(idx_flat, page_table, slopes, q8, k.reshape(DB, Ls, SEG), v.reshape(DB, Ls, SEG),
                             cache_k, cache_v, layer=l, L=Ls, n_sel=n_sel_s, past=past)
        att = att8[:, :Ls].reshape(DB * Ls, SEG)
        pad_t = ((0, 0), (0, L_S - Ls), (0, 0))
        gate_pad = jnp.where(jnp.arange(LANES) < H_MLSTM, NEG, 0.0).astype(f32)
        gates_s = jnp.concatenate(
            [gates.reshape(DB, Ls, LANES), jnp.broadcast_to(gate_pad, (DB, L_S - Ls, LANES))], axis=1)
        hm8, c_s, n_s, m_s = _mlstm(
            jnp.pad(zm.reshape(DB, Ls, 4 * SEG), pad_t), gates_s, state_C[l], state_n[l],
            jnp.broadcast_to(state_m[l][:, :, None], (DB, H_MLSTM, LANES)), L=L_S, l_out=8)
        hm = hm8[:, :Ls].reshape(DB * Ls, SEG)
        x1 = _outproj(xs, att, hm, wo, tm=TM_S, tn=1024)
        buf = state_conv[l]
        zero = jnp.zeros_like(buf[:, 0])
        p1 = jnp.stack([buf[:, 1]] + [zero] * (Ls - 1), axis=1).reshape(DB * Ls, -1)
        p2 = jnp.stack([buf[:, 0], buf[:, 1]] + [zero] * (Ls - 2), axis=1).reshape(DB * Ls, -1)
        xs, a_full = _ffn(x1, g_ffn, g_fin, wg, wu, conv_w[l], wd, tm=TM_S, tf=TF, seq_len=Ls, prev=(p1, p2))
        conv_s = a_full.reshape(DB, Ls, -1)[:, Ls - (CONV_W - 1):]
        st_s.append((k.reshape(DB, Ls, H_ATT, HEAD_DIM), v.reshape(DB, Ls, H_ATT, HEAD_DIM), c_s, n_s, m_s[:, :, 0], conv_s))

    k_p, v_p, C_p, n_p, m_p, conv_p = [jnp.stack(z) for z in zip(*st_p)]
    k_s, v_s, C_s, n_s, m_s, conv_s = [jnp.stack(z) for z in zip(*st_s)]
    return (xp.reshape(Bsz, T, D), xs.reshape(DB, Ls, D), k_p, v_p, C_p, n_p, m_p, conv_p,
            k_s, v_s, C_s, n_s, m_s, conv_s)
```

```python
import functools
from typing import NamedTuple

import jax
import jax.numpy as jnp
from jax import lax
from jax.experimental import pallas as pl
from jax.experimental.pallas import tpu as pltpu

f32, bf16, i32 = jnp.float32, jnp.bfloat16, jnp.int32

HEAD_DIM = 128
H_ATT = 8
H_MLSTM = 8
SEG = H_ATT * HEAD_DIM
N_SEG = 7
CONV_W = 3
MOBA_BLOCK = 256
MOBA_TOP_K = 3
PAGE_SIZE = 128
PAGES_PER_BLOCK = MOBA_BLOCK // PAGE_SIZE
EPS = 1e-6
NEG = -1e30
LANES = 128
MAX_BLOCKS = 32
MIB = 1 << 20


def _params(sem, vmem_mib):
    return pltpu.CompilerParams(dimension_semantics=sem, vmem_limit_bytes=vmem_mib * MIB)


def _iota(shape, dim, dtype=i32):
    return lax.broadcasted_iota(dtype, shape, dim)


def _dot(a, b):
    return jnp.dot(a, b, preferred_element_type=f32)


def _dot_nt(a, b):
    return lax.dot_general(a, b, (((1,), (1,)), ((), ())), preferred_element_type=f32)


def _dot_tn(a, b):
    return lax.dot_general(a, b, (((0,), (0,)), ((), ())), preferred_element_type=f32)


def _rmsnorm(x, g):
    return x * lax.rsqrt(jnp.mean(x * x, axis=-1, keepdims=True) + EPS) * g


def _cast_w_in_body(w_ref, wg_ref, main_ref, wif_ref):
    main_ref[...] = w_ref[...].T.astype(bf16)

    @pl.when(pl.program_id(0) == 0)
    def _():
        g = wg_ref[...]
        g = jnp.concatenate([g, jnp.zeros((LANES - g.shape[0], g.shape[1]), f32)], axis=0)
        wif_ref[...] = g.T.astype(bf16)


def _cast_w_in(w_in_t, *, layer):
    D = w_in_t.shape[2]
    n_gate = 2 * H_MLSTM
    assert w_in_t.shape[1] == N_SEG * SEG + n_gate and (N_SEG * SEG) % n_gate == 0
    return pl.pallas_call(
        _cast_w_in_body,
        grid=(N_SEG,),
        in_specs=[pl.BlockSpec((None, SEG, D), lambda n: (layer, n, 0)),
                  pl.BlockSpec((None, n_gate, D), lambda n: (layer, N_SEG * SEG // n_gate, 0))],
        out_specs=[pl.BlockSpec((D, SEG), lambda n: (0, n)),
                   pl.BlockSpec((D, LANES), lambda n: (0, 0))],
        out_shape=[jax.ShapeDtypeStruct((D, N_SEG * SEG), bf16), jax.ShapeDtypeStruct((D, LANES), bf16)],
        compiler_params=_params(("arbitrary",), 48),
        name="cast_w_in",
    )(w_in_t, w_in_t)


def _inproj_body(x_ref, g_ref, w_ref, wif_ref, bif_ref, q_ref, k_ref, v_ref, zm_ref, gate_ref, *rest, attn_aux):
    if attn_aux:
        kb_ref, vb_ref, kmean_ref, h_scr = rest
    else:
        (h_scr,) = rest
    n = pl.program_id(1)
    scale = HEAD_DIM ** -0.5

    @pl.when(n == 0)
    def _():
        hb = _rmsnorm(x_ref[...], g_ref[...]).astype(bf16)
        h_scr[...] = hb
        zg = _dot(hb, wif_ref[...]) + bif_ref[...]
        lane = _iota(zg.shape, 1)
        log_sig = jnp.minimum(zg, 0.0) - jnp.log1p(jnp.exp(-jnp.abs(zg)))
        gate_ref[...] = jnp.where(lane < H_MLSTM, zg, jnp.where(lane < 2 * H_MLSTM, log_sig, 0.0))
        q_ref[...] = (_dot(hb, w_ref[...]) * scale).astype(bf16)

    def segment():
        return _dot(h_scr[...], w_ref[...])

    @pl.when(n == 1)
    def _():
        z = segment()
        k_ref[...] = z
        if attn_aux:
            kb_ref[...] = z.astype(bf16)
            for i in range(z.shape[0] // MOBA_BLOCK):
                blk = z[i * MOBA_BLOCK:(i + 1) * MOBA_BLOCK, :]
                kmean_ref[0, i:i + 1, :] = jnp.sum(blk, axis=0, keepdims=True) * (1.0 / MOBA_BLOCK)

    @pl.when(n == 2)
    def _():
        z = segment()
        v_ref[...] = z
        if attn_aux:
            vb_ref[...] = z.astype(bf16)

    @pl.when(n == 4)
    def _():
        zm_ref[...] = (segment() * scale).astype(bf16)

    @pl.when((n == 3) | (n >= 5))
    def _():
        zm_ref[...] = segment().astype(bf16)


def _inproj(x, g, w_main, w_if, b_if, *, tm, attn_aux):
    M, D = x.shape
    grid = (M // tm, N_SEG)

    def row(m, n):
        return (m, 0)

    out_specs = [
        pl.BlockSpec((tm, SEG), row),
        pl.BlockSpec((tm, SEG), row),
        pl.BlockSpec((tm, SEG), row),
        pl.BlockSpec((tm, SEG), lambda m, n: (m, jnp.maximum(n - 3, 0))),
        pl.BlockSpec((tm, LANES), row),
    ]
    out_shape = [
        jax.ShapeDtypeStruct((M, SEG), bf16),
        jax.ShapeDtypeStruct((M, SEG), f32),
        jax.ShapeDtypeStruct((M, SEG), f32),
        jax.ShapeDtypeStruct((M, 4 * SEG), bf16),
        jax.ShapeDtypeStruct((M, LANES), f32),
    ]
    if attn_aux:
        assert tm % MOBA_BLOCK == 0
        bpt = tm // MOBA_BLOCK
        out_specs += [pl.BlockSpec((tm, SEG), row), pl.BlockSpec((tm, SEG), row),
                      pl.BlockSpec((1, bpt, SEG), lambda m, n: (m, 0, 0))]
        out_shape += [jax.ShapeDtypeStruct((M, SEG), bf16), jax.ShapeDtypeStruct((M, SEG), bf16),
                      jax.ShapeDtypeStruct((M // tm, bpt, SEG), f32)]
    return pl.pallas_call(
        functools.partial(_inproj_body, attn_aux=attn_aux),
        grid=grid,
        in_specs=[
            pl.BlockSpec((tm, D), row),
            pl.BlockSpec((1, D), lambda m, n: (0, 0)),
            pl.BlockSpec((D, SEG), lambda m, n: (0, n)),
            pl.BlockSpec((D, LANES), lambda m, n: (0, 0)),
            pl.BlockSpec((1, LANES), lambda m, n: (0, 0)),
        ],
        out_specs=out_specs,
        out_shape=out_shape,
        scratch_shapes=[pltpu.VMEM((tm, D), bf16)],
        compiler_params=_params(("arbitrary", "arbitrary"), 52),
        name="inproj",
    )(x, g, w_main, w_if, b_if)


def _top_blocks_mask(scores_t, n_valid, own, n_sel):
    R = scores_t.shape[0]
    rown = _iota(scores_t.shape, 0)
    s = jnp.where(rown < n_valid, scores_t, -jnp.inf)
    mask = jnp.where(rown == own, 0.0, NEG)
    for _ in range(n_sel):
        mx = jnp.max(s, axis=0, keepdims=True)
        idx = jnp.min(jnp.where(s == mx, rown, R), axis=0, keepdims=True)
        pick = (rown == idx) & (mx > -jnp.inf)
        mask = jnp.where(pick, 0.0, mask)
        s = jnp.where(pick, -jnp.inf, s)
    return mask


ATT_HEADS = 2
ATT_CHUNK = 4


def _moba_prompt_body(slopes_ref, q_ref, kb_ref, vb_ref, kmean_ref, o_ref, kaug, vt, qaug, acc_scr, s_scr, ml_scr,
                      *, nb):
    hp = pl.program_id(1)
    qi = pl.program_id(2)
    B = MOBA_BLOCK
    CK = ATT_CHUNK * B
    heads = [slice(hh * HEAD_DIM, (hh + 1) * HEAD_DIM) for hh in range(ATT_HEADS)]

    @pl.when(qi == 0)
    def _init():
        lane = _iota((B, LANES), 1)
        s_lo = _iota((B, LANES), 0).astype(f32)
        for hh, cs in enumerate(heads):
            slope = slopes_ref[hp * ATT_HEADS + hh]
            for n in range(nb):
                c, r0 = n // ATT_CHUNK, (n % ATT_CHUNK) * B
                kaug[hh, c, r0:r0 + B, 0:HEAD_DIM] = kb_ref[0, n * B:(n + 1) * B, cs]
                ext = jnp.where(lane == n, 1.0, 0.0)
                ext = jnp.where(lane == MAX_BLOCKS, -slope, ext)
                ext = jnp.where(lane == MAX_BLOCKS + 1, slope * s_lo, ext)
                ext = jnp.where(lane == MAX_BLOCKS + 2, -slope * B, ext)
                ext = jnp.where(lane == MAX_BLOCKS + 3, slope * (B * n), ext)
                kaug[hh, c, r0:r0 + B, HEAD_DIM:2 * HEAD_DIM] = ext.astype(bf16)
                vt[hh, c, :, r0:r0 + B] = vb_ref[0, n * B:(n + 1) * B, cs].astype(f32).T.astype(bf16)

    c_own = qi // ATT_CHUNK
    n_ext = HEAD_DIM - MAX_BLOCKS
    rr = _iota((n_ext, B), 0)
    t_lo = _iota((n_ext, B), 1).astype(f32)
    q_ext = jnp.where(rr == 0, t_lo, jnp.where((rr == 1) | (rr == 3), 1.0, 0.0))
    q_ext = jnp.where(rr == 2, qi.astype(f32), q_ext).astype(bf16)
    for hh, cs in enumerate(heads):
        q_t = q_ref[0, :, cs].astype(f32).T.astype(bf16)
        scores_t = _dot(kmean_ref[0, :, cs].astype(bf16), q_t)
        mask_t = _top_blocks_mask(scores_t, qi, qi, min(MOBA_TOP_K, nb))
        if nb < MAX_BLOCKS:
            mask_t = jnp.concatenate([mask_t, jnp.full((MAX_BLOCKS - nb, B), NEG, f32)], axis=0)
        qaug[hh, 0:HEAD_DIM, :] = q_t
        qaug[hh, HEAD_DIM:HEAD_DIM + MAX_BLOCKS, :] = mask_t.astype(bf16)
        qaug[hh, HEAD_DIM + MAX_BLOCKS:, :] = q_ext
        s_scr[hh, 0] = _dot(kaug[hh, 0], qaug[hh])
        ml_scr[2 * hh:2 * hh + 1, :] = jnp.full((1, B), NEG, f32)
        ml_scr[2 * hh + 1:2 * hh + 2, :] = jnp.zeros((1, B), f32)
        acc_scr[hh] = jnp.zeros((HEAD_DIM, B), f32)

    def softmax_pv(hh, c, s):
        m = ml_scr[2 * hh:2 * hh + 1, :]
        m_new = jnp.maximum(m, jnp.max(s, axis=0, keepdims=True))
        alpha = jnp.exp(m - m_new)
        p = jnp.exp(s - m_new)
        ml_scr[2 * hh:2 * hh + 1, :] = m_new
        ml_scr[2 * hh + 1:2 * hh + 2, :] = alpha * ml_scr[2 * hh + 1:2 * hh + 2, :] + jnp.sum(p, axis=0, keepdims=True)
        acc_scr[hh] = alpha * acc_scr[hh] + _dot(vt[hh, c], p.astype(bf16))

    def stage(c, cur, nxt):
        for hh in range(ATT_HEADS):
            s_scr[hh, nxt] = _dot(kaug[hh, c + 1], qaug[hh])
        for hh in range(ATT_HEADS):
            softmax_pv(hh, c, s_scr[hh, cur])

    def body(c, carry):
        @pl.when(c % 2 == 0)
        def _():
            stage(c, 0, 1)

        @pl.when(c % 2 == 1)
        def _():
            stage(c, 1, 0)

        return carry

    lax.fori_loop(0, c_own, body, 0)
    own = qi - c_own * ATT_CHUNK
    for r in range(ATT_CHUNK):
        @pl.when(own == r)
        def _(r=r):
            nk = (r + 1) * B
            causal = (_iota((nk, B), 0) - _iota((nk, B), 1)) <= r * B
            for hh in range(ATT_HEADS):
                s = jnp.where(causal, s_scr[hh, c_own % 2, 0:nk, :], NEG)
                m = ml_scr[2 * hh:2 * hh + 1, :]
                m_new = jnp.maximum(m, jnp.max(s, axis=0, keepdims=True))
                alpha = jnp.exp(m - m_new)
                p = jnp.exp(s - m_new)
                ml_scr[2 * hh + 1:2 * hh + 2, :] = (alpha * ml_scr[2 * hh + 1:2 * hh + 2, :]
                                                    + jnp.sum(p, axis=0, keepdims=True))
                acc_scr[hh] = alpha * acc_scr[hh] + _dot(vt[hh, c_own, :, 0:nk], p.astype(bf16))

    for hh, cs in enumerate(heads):
        o_ref[0, :, cs] = (acc_scr[hh] / ml_scr[2 * hh + 1:2 * hh + 2, :]).T.astype(bf16)


def _moba_prompt(slopes, q, kb, vb, kmean):
    Bsz, T, _ = q.shape
    assert T % (MOBA_BLOCK * ATT_CHUNK) == 0 and H_ATT % ATT_HEADS == 0
    nb = T // MOBA_BLOCK
    assert nb <= MAX_BLOCKS
    blk = MOBA_BLOCK
    W = ATT_HEADS * HEAD_DIM
    nc = nb // ATT_CHUNK
    return pl.pallas_call(
        functools.partial(_moba_prompt_body, nb=nb),
        grid=(Bsz, H_ATT // ATT_HEADS, nb),
        in_specs=[
            pl.BlockSpec(memory_space=pltpu.SMEM),
            pl.BlockSpec((1, blk, W), lambda b, h, i: (b, i, h)),
            pl.BlockSpec((1, T, W), lambda b, h, i: (b, 0, h)),
            pl.BlockSpec((1, T, W), lambda b, h, i: (b, 0, h)),
            pl.BlockSpec((1, nb, W), lambda b, h, i: (b, 0, h)),
        ],
        out_specs=pl.BlockSpec((1, blk, W), lambda b, h, i: (b, i, h)),
        out_shape=jax.ShapeDtypeStruct((Bsz, T, H_ATT * HEAD_DIM), bf16),
        scratch_shapes=[
            pltpu.VMEM((ATT_HEADS, nc, ATT_CHUNK * blk, 2 * HEAD_DIM), bf16),
            pltpu.VMEM((ATT_HEADS, nc, HEAD_DIM, ATT_CHUNK * blk), bf16),
            pltpu.VMEM((ATT_HEADS, 2 * HEAD_DIM, blk), bf16),
            pltpu.VMEM((ATT_HEADS, HEAD_DIM, blk), f32),
            pltpu.VMEM((ATT_HEADS, 2, ATT_CHUNK * blk, blk), f32),
            pltpu.VMEM((2 * ATT_HEADS, blk), f32),
        ],
        compiler_params=_params(("arbitrary", "arbitrary", "arbitrary"), 52),
        name="moba_prompt",
    )(slopes, q, kb, vb, kmean)


FFN_STREAM_BLOCKS = 3


class _KStream(NamedTuple):
    page_table: jax.Array
    cache_k: jax.Array
    layer: int
    base: int
    bps: int
    total: int


def _kstream_specs(ks, step_of):
    nbp = ks.page_table.shape[1] // PAGES_PER_BLOCK

    def page_map(*ids, i):
        g = jnp.minimum(ks.base + step_of(*ids[:-1]) * ks.bps + i // PAGES_PER_BLOCK, ks.total - 1)
        return (ks.layer, ids[-1][g // nbp, (g % nbp) * PAGES_PER_BLOCK + i % PAGES_PER_BLOCK], 0, 0, 0)

    in_specs = [pl.BlockSpec((None, None, PAGE_SIZE, H_ATT, HEAD_DIM), functools.partial(page_map, i=i))
                for i in range(ks.bps * PAGES_PER_BLOCK)]
    out_spec = pl.BlockSpec((ks.bps, H_ATT, HEAD_DIM), lambda *ids: (step_of(*ids[:-1]), 0, 0))
    return in_specs, out_spec


def _kstream_reduce(pages, out_ref):
    for b in range(len(pages) // PAGES_PER_BLOCK):
        s = jnp.sum(pages[PAGES_PER_BLOCK * b][...], axis=0)
        for p in range(1, PAGES_PER_BLOCK):
            s = s + jnp.sum(pages[PAGES_PER_BLOCK * b + p][...], axis=0)
        out_ref[b] = s * (1.0 / MOBA_BLOCK)


def _qsel_body(x_ref, g_ref, w_ref, o_ref):
    h = _rmsnorm(x_ref[...], g_ref[...])
    o_ref[...] = lax.dot_general(h, w_ref[...], (((1,), (1,)), ((), ())), preferred_element_type=f32,
                                 precision=lax.Precision.HIGHEST)


def _qsel(x, g, w_in_t, *, layer, tn=256):
    M, D = x.shape
    return pl.pallas_call(
        _qsel_body,
        grid=(SEG // tn,),
        in_specs=[
            pl.BlockSpec((M, D), lambda n: (0, 0)),
            pl.BlockSpec((1, D), lambda n: (0, 0)),
            pl.BlockSpec((None, tn, D), lambda n: (layer, n, 0)),
        ],
        out_specs=pl.BlockSpec((M, tn), lambda n: (0, n)),
        out_shape=jax.ShapeDtypeStruct((M, SEG), f32),
        compiler_params=_params(("arbitrary",), 32),
        name="qsel",
    )(x, g, w_in_t)


IDX_ROWS = 16


def _select_body(km_ref, q_ref, idx_ref, *, L, n_sel):
    nbp = km_ref.shape[0]
    idx_ref[...] = jnp.zeros(idx_ref.shape, i32)
    km = km_ref[...]
    blk = _iota((nbp, H_ATT, 1), 0)
    for l in range(L):
        sc = jnp.sum(km * q_ref[l][None], axis=-1, keepdims=True)
        for r in range(n_sel):
            mx = jnp.max(sc, axis=0, keepdims=True)
            idx = jnp.min(jnp.where(sc == mx, blk, nbp), axis=0, keepdims=True)
            idx_ref[l * n_sel + r] = jnp.broadcast_to(idx[0], (H_ATT, LANES))
            sc = jnp.where(blk == idx, -jnp.inf, sc)


def _select_sample(kmean, q8, *, L, n_sel):
    DB, nbp = kmean.shape[:2]
    assert L * n_sel <= IDX_ROWS
    return pl.pallas_call(
        functools.partial(_select_body, L=L, n_sel=n_sel),
        grid=(DB,),
        in_specs=[
            pl.BlockSpec((None, nbp, H_ATT, HEAD_DIM), lambda d: (d, 0, 0, 0)),
            pl.BlockSpec((None, 8, H_ATT, HEAD_DIM), lambda d: (d, 0, 0, 0)),
        ],
        out_specs=pl.BlockSpec((None, IDX_ROWS, H_ATT, LANES), lambda d: (d, 0, 0, 0)),
        out_shape=jax.ShapeDtypeStruct((DB, IDX_ROWS, H_ATT, LANES), i32),
        compiler_params=_params(("arbitrary",), 32),
        name="select_sample",
    )(kmean, q8)


def _sattn_body(idx_ref, pt_ref, slopes_ref, q_ref, kn_ref, vn_ref, ck_hbm, cv_hbm, o_ref, kbuf, vbuf, sem,
                *, layer, L, n_sel, past):
    nblk = L * n_sel * PAGES_PER_BLOCK
    d = pl.program_id(0)
    h = pl.program_id(1)
    step = d * H_ATT + h
    slot = step % 2

    def page_copies(dd, hh, sl):
        out = []
        for i in range(nblk):
            lj, p = i // PAGES_PER_BLOCK, i % PAGES_PER_BLOCK
            blk = idx_ref[((dd * L + lj // n_sel) * H_ATT + hh) * n_sel + lj % n_sel]
            phys = pt_ref[dd, blk * PAGES_PER_BLOCK + p]
            out.append(pltpu.make_async_copy(ck_hbm.at[layer, phys, :, hh, :], kbuf.at[sl, i], sem.at[sl]))
            out.append(pltpu.make_async_copy(cv_hbm.at[layer, phys, :, hh, :], vbuf.at[sl, i], sem.at[sl]))
        return out

    @pl.when(step == 0)
    def _():
        for cp in page_copies(d, h, slot):
            cp.start()

    @pl.when(step + 1 < pl.num_programs(0) * H_ATT)
    def _():
        nxt = step + 1
        for cp in page_copies(nxt // H_ATT, nxt % H_ATT, 1 - slot):
            cp.start()

    for cp in page_copies(d, h, slot):
        cp.wait()

    kp = [kbuf.at[slot, i] for i in range(nblk)]
    vp = [vbuf.at[slot, i] for i in range(nblk)]
    slope = slopes_ref[h]
    qf = q_ref[0].astype(f32)
    kn = kn_ref[0]
    vn = vn_ref[0]
    lpos = _iota((L, 1), 0)
    slope_pos = slope * _iota((PAGE_SIZE, 1), 0).astype(f32)
    rows = []
    for l in range(L):
        ql = qf[l:l + 1, :]
        lo = jnp.sum(kn * ql, axis=1, keepdims=True)
        lo = jnp.where(lpos <= l, lo - slope * (l - lpos).astype(f32), NEG)
        logits = []
        for j in range(n_sel):
            blk = idx_ref[((d * L + l) * H_ATT + h) * n_sel + j]
            for p in range(PAGES_PER_BLOCK):
                kb = kp[(l * n_sel + j) * PAGES_PER_BLOCK + p][...]
                page_dist = (past + l - blk * MOBA_BLOCK - p * PAGE_SIZE).astype(f32)
                logits.append(jnp.sum(kb * ql, axis=1, keepdims=True) + (slope_pos - slope * page_dist))
        m = jnp.max(lo, axis=0, keepdims=True)
        for lg in logits:
            m = jnp.maximum(m, jnp.max(lg, axis=0, keepdims=True))
        p_own = jnp.exp(lo - m)
        den = jnp.sum(p_own, axis=0, keepdims=True)
        num = jnp.sum(p_own * vn, axis=0, keepdims=True)
        for i, lg in enumerate(logits):
            pi = jnp.exp(lg - m)
            den = den + jnp.sum(pi, axis=0, keepdims=True)
            num = num + jnp.sum(pi * vp[l * n_sel * PAGES_PER_BLOCK + i][...], axis=0, keepdims=True)
        rows.append(num / den)
    rows.append(jnp.zeros((8 - L, HEAD_DIM), f32))
    o_ref[0] = jnp.concatenate(rows, axis=0).astype(bf16)


def _sattn_sample(idx_flat, page_table, slopes, q8, k_new, v_new, cache_k, cache_v, *, layer, L, n_sel, past):
    DB = q8.shape[0]
    assert L <= 8
    nblk = L * n_sel * PAGES_PER_BLOCK
    return pl.pallas_call(
        functools.partial(_sattn_body, layer=layer, L=L, n_sel=n_sel, past=past),
        grid_spec=pltpu.PrefetchScalarGridSpec(
            num_scalar_prefetch=2,
            grid=(DB, H_ATT),
            in_specs=[
                pl.BlockSpec(memory_space=pltpu.SMEM),
                pl.BlockSpec((1, 8, HEAD_DIM), lambda d, h, idx, pt: (d, 0, h)),
                pl.BlockSpec((1, L, HEAD_DIM), lambda d, h, idx, pt: (d, 0, h)),
                pl.BlockSpec((1, L, HEAD_DIM), lambda d, h, idx, pt: (d, 0, h)),
                pl.BlockSpec(memory_space=pl.ANY),
                pl.BlockSpec(memory_space=pl.ANY),
            ],
            out_specs=pl.BlockSpec((1, 8, HEAD_DIM), lambda d, h, idx, pt: (d, 0, h)),
            scratch_shapes=[
                pltpu.VMEM((2, nblk, PAGE_SIZE, HEAD_DIM), f32),
                pltpu.VMEM((2, nblk, PAGE_SIZE, HEAD_DIM), f32),
                pltpu.SemaphoreType.DMA((2,)),
            ],
        ),
        out_shape=jax.ShapeDtypeStruct((DB, 8, H_ATT * HEAD_DIM), bf16),
        compiler_params=_params(("arbitrary", "arbitrary"), 32),
        name="sattn_sample",
    )(idx_flat, page_table, slopes, q8, k_new, v_new, cache_k, cache_v)


def _split3_dot(a_bf16, x):
    x1 = x.astype(bf16)
    r1 = x - x1.astype(f32)
    x2 = r1.astype(bf16)
    x3 = (r1 - x2.astype(f32)).astype(bf16)
    return _dot(a_bf16, x1) + _dot(a_bf16, x2) + _dot(a_bf16, x3)


def _mlstm_body(*refs, L, l_out, n_stream):
    if n_stream:
        refs = refs[1:]
        _kstream_reduce(refs[5:5 + n_stream], refs[5 + n_stream + 4])
        refs = refs[:5] + refs[5 + n_stream:5 + n_stream + 4] + refs[5 + n_stream + 5:]
    zm_ref, gate_ref, c0_ref, n0_ref, m0_ref, hm_ref, c_out, n_out, m_out, c_scr, n_scr, m_scr = refs
    c = pl.program_id(1)
    nc = pl.num_programs(1)

    @pl.when(c == 0)
    def _():
        c_scr[...] = c0_ref[0]
        n_scr[...] = n0_ref[0]
        m_scr[...] = m0_ref[0]

    G = gate_ref[0]
    row = _iota((L, L), 0)
    col = _iota((L, L), 1)
    causal = row >= col
    csum = _split3_dot(jnp.where(causal, 1.0, 0.0).astype(bf16), G)
    lane = _iota((L, LANES), 1)
    X = jnp.where(lane < H_MLSTM, pltpu.roll(csum, LANES - H_MLSTM, 1), pltpu.roll(G, H_MLSTM, 1) - csum)
    XT = X.T
    x1 = X.astype(bf16)
    r1 = X - x1.astype(f32)
    x2 = r1.astype(bf16)
    x3 = (r1 - x2.astype(f32)).astype(bf16)
    sel_row = _iota((LANES, LANES), 0)

    def replicated(src):
        e = jnp.where(sel_row == src, 1.0, 0.0).astype(bf16)
        return _dot(x1, e) + _dot(x2, e) + _dot(x3, e)

    def over_keys(col):
        return col[:, :L] if L <= LANES else jnp.concatenate([col] * (L // LANES), axis=1)

    ones_v = jnp.ones((L, HEAD_DIM), bf16)
    for h in range(H_MLSTM):
        sl = slice(h * HEAD_DIM, (h + 1) * HEAD_DIM)
        q = zm_ref[0, :, sl]
        k = zm_ref[0, :, SEG + h * HEAD_DIM:SEG + (h + 1) * HEAD_DIM]
        v = zm_ref[0, :, 2 * SEG + h * HEAD_DIM:2 * SEG + (h + 1) * HEAD_DIM]
        o = zm_ref[0, :, 3 * SEG + h * HEAD_DIM:3 * SEG + (h + 1) * HEAD_DIM]
        b_rep = replicated(h)
        g_rep = replicated(H_MLSTM + h)
        g_row = XT[H_MLSTM + h:H_MLSTM + h + 1, :]
        m_prev = m_scr[h:h + 1, :]
        g_max = jnp.max(jnp.where(causal, g_row, -jnp.inf), axis=1, keepdims=True)
        m_t = b_rep + jnp.maximum(m_prev, g_max)
        decay_d = jnp.exp(jnp.where(causal, g_row + over_keys(b_rep - m_t), -jnp.inf))
        s = _dot_nt(q, k) * decay_d
        a_inter = jnp.exp(b_rep + m_prev - m_t)
        c_h = c_scr[h]
        n_h = n_scr[h:h + 1, :]
        sv = _dot(s.astype(bf16), jnp.concatenate([v, ones_v], axis=1))
        n_rows = jnp.broadcast_to(n_h, (HEAD_DIM, HEAD_DIM)).astype(bf16)
        cq = _dot_nt(q, jnp.concatenate([c_h.astype(bf16), n_rows], axis=0))
        num = sv[:, :HEAD_DIM] + a_inter * cq[:, :HEAD_DIM]
        den = sv[:, HEAD_DIM:] + a_inter * cq[:, HEAD_DIM:]
        hh = num / jnp.maximum(jnp.abs(den), jnp.exp(-m_t))
        out = (jax.nn.sigmoid(o.astype(f32)) * hh).astype(bf16)
        hm_ref[0, :, sl] = out[:l_out]
        m_new = m_t[L - 1:L, :]
        b_last = b_rep[L - 1:L, :]
        decay = jnp.exp(b_last + m_prev - m_new)
        wk = jnp.exp(b_last + g_rep - m_new) * k.astype(f32)
        c_scr[h] = decay * c_h + _dot_tn(v, wk.astype(bf16))
        n_scr[h:h + 1, :] = decay * n_h + jnp.sum(wk, axis=0, keepdims=True)
        m_scr[h:h + 1, :] = m_new

    @pl.when(c == nc - 1)
    def _():
        c_out[0] = c_scr[...]
        n_out[0] = n_scr[...]
        m_out[0] = m_scr[...]


def _mlstm(zm, gates, c0, n0, m0, *, L, l_out, kstream=None):
    Bsz, T, _ = zm.shape
    nc = T // L
    in_specs = [
        pl.BlockSpec((1, L, 4 * SEG), lambda b, c, *_: (b, c, 0)),
        pl.BlockSpec((1, L, LANES), lambda b, c, *_: (b, c, 0)),
        pl.BlockSpec((1, H_MLSTM, HEAD_DIM, HEAD_DIM), lambda b, c, *_: (b, 0, 0, 0)),
        pl.BlockSpec((1, H_MLSTM, HEAD_DIM), lambda b, c, *_: (b, 0, 0)),
        pl.BlockSpec((1, H_MLSTM, LANES), lambda b, c, *_: (b, 0, 0)),
    ]
    out_specs = [
        pl.BlockSpec((1, l_out, SEG), lambda b, c, *_: (b, c, 0)),
        pl.BlockSpec((1, H_MLSTM, HEAD_DIM, HEAD_DIM), lambda b, c, *_: (b, 0, 0, 0)),
        pl.BlockSpec((1, H_MLSTM, HEAD_DIM), lambda b, c, *_: (b, 0, 0)),
        pl.BlockSpec((1, H_MLSTM, LANES), lambda b, c, *_: (b, 0, 0)),
    ]
    out_shape = [
        jax.ShapeDtypeStruct((Bsz, nc * l_out, SEG), bf16),
        jax.ShapeDtypeStruct((Bsz, H_MLSTM, HEAD_DIM, HEAD_DIM), f32),
        jax.ShapeDtypeStruct((Bsz, H_MLSTM, HEAD_DIM), f32),
        jax.ShapeDtypeStruct((Bsz, H_MLSTM, LANES), f32),
    ]
    scratch = [
        pltpu.VMEM((H_MLSTM, HEAD_DIM, HEAD_DIM), f32),
        pltpu.VMEM((H_MLSTM, HEAD_DIM), f32),
        pltpu.VMEM((H_MLSTM, LANES), f32),
    ]
    args = [zm, gates, c0, n0, m0]
    n_stream = 0
    if kstream is not None:
        page_specs, km_spec = _kstream_specs(kstream, lambda b, c: b * nc + c)
        n_stream = len(page_specs)
        in_specs += page_specs
        out_specs.append(km_spec)
        out_shape.append(jax.ShapeDtypeStruct((Bsz * nc * kstream.bps, H_ATT, HEAD_DIM), f32))
        args = [kstream.page_table] + args + [kstream.cache_k] * n_stream
    return pl.pallas_call(
        functools.partial(_mlstm_body, L=L, l_out=l_out, n_stream=n_stream),
        grid_spec=pltpu.PrefetchScalarGridSpec(
            num_scalar_prefetch=1 if n_stream else 0,
            grid=(Bsz, nc),
            in_specs=in_specs,
            out_specs=out_specs,
            scratch_shapes=scratch,
        ),
        out_shape=out_shape,
        compiler_params=_params(("arbitrary", "arbitrary"), 16 + n_stream),
        name="mlstm",
    )(*args)


def _outproj_body(x_ref, att_ref, hm_ref, wa_ref, wm_ref, o_ref):
    o_ref[...] = x_ref[...] + _dot(att_ref[...], wa_ref[...]) + _dot(hm_ref[...], wm_ref[...])


def _outproj(x, att, hm, w_out, *, tm, tn):
    M, D = x.shape
    return pl.pallas_call(
        _outproj_body,
        grid=(M // tm, D // tn),
        in_specs=[
            pl.BlockSpec((tm, tn), lambda m, n: (m, n)),
            pl.BlockSpec((tm, SEG), lambda m, n: (m, 0)),
            pl.BlockSpec((tm, SEG), lambda m, n: (m, 0)),
            pl.BlockSpec((SEG, tn), lambda m, n: (0, n)),
            pl.BlockSpec((SEG, tn), lambda m, n: (1, n)),
        ],
        out_specs=pl.BlockSpec((tm, tn), lambda m, n: (m, n)),
        out_shape=jax.ShapeDtypeStruct((M, D), f32),
        compiler_params=_params(("arbitrary", "arbitrary"), 40),
        name="outproj",
    )(x, att, hm, w_out, w_out)


def _ffn_body(*refs, tm, seq_len, carry_mode, n_stream):
    if n_stream:
        refs = refs[1:]
        _kstream_reduce(refs[7:7 + n_stream], refs[7 + n_stream + 2])
        refs = refs[:7] + refs[7 + n_stream:7 + n_stream + 2] + refs[7 + n_stream + 3:]
    if carry_mode:
        x_ref, g_ref, gf_ref, wg_ref, wu_ref, cw_ref, wd_ref, y_ref, tail_ref, h_scr, acc_scr, carry = refs
    else:
        x_ref, g_ref, gf_ref, wg_ref, wu_ref, cw_ref, wd_ref, p1_ref, p2_ref, y_ref, a_ref, h_scr, acc_scr = refs
    m = pl.program_id(0)
    f = pl.program_id(1)
    nf = pl.num_programs(1)

    @pl.when(f == 0)
    def _():
        h_scr[...] = _rmsnorm(x_ref[...], g_ref[...]).astype(bf16)
        acc_scr[...] = jnp.zeros(acc_scr.shape, f32)

    hb = h_scr[...]
    a = _dot(hb, wg_ref[...])
    u = _dot(hb, wu_ref[...])
    row = _iota(a.shape, 0)
    a1 = pltpu.roll(a, 1, 0)
    a2 = pltpu.roll(a, 2, 0)
    if carry_mode:
        prev = carry[f]
        prev = jnp.where(m % (seq_len // tm) == 0, 0.0, prev)
        a1 = jnp.where(row == 0, prev[7:8, :], a1)
        a2 = jnp.where(row == 0, prev[6:7, :], jnp.where(row == 1, prev[7:8, :], a2))
        carry[f] = a[tm - 8:, :]
        tail_ref[0] = a[tm - 8:, :]
    else:
        lpos = row % seq_len
        a1 = jnp.where(lpos >= 1, a1, p1_ref[...])
        a2 = jnp.where(lpos >= 2, a2, p2_ref[...])
        a_ref[...] = a
    cw = cw_ref[...]
    cv = cw[0:1, :] * a2 + cw[1:2, :] * a1 + cw[2:3, :] * a
    gl = 0.5 * cv * (1.0 + lax.erf(cv * (0.5 ** 0.5)))
    acc_scr[...] += _dot((gl * u).astype(bf16), wd_ref[...])

    @pl.when(f == nf - 1)
    def _():
        y_ref[...] = _rmsnorm(x_ref[...] + acc_scr[...], gf_ref[...])


def _ffn(x1, g_ffn, g_final, w_gate, w_up, conv_w, w_down, *, tm, tf, seq_len, prev=None, kstream=None):
    M, D = x1.shape
    DF = w_gate.shape[1]
    nf = DF // tf
    carry_mode = prev is None
    in_specs = [
        pl.BlockSpec((tm, D), lambda m, f, *_: (m, 0)),
        pl.BlockSpec((1, D), lambda m, f, *_: (0, 0)),
        pl.BlockSpec((1, D), lambda m, f, *_: (0, 0)),
        pl.BlockSpec((D, tf), lambda m, f, *_: (0, f)),
        pl.BlockSpec((D, tf), lambda m, f, *_: (0, f)),
        pl.BlockSpec((CONV_W, tf), lambda m, f, *_: (0, f)),
        pl.BlockSpec((tf, D), lambda m, f, *_: (f, 0)),
    ]
    args = [x1, g_ffn, g_final, w_gate, w_up, conv_w, w_down]
    scratch = [pltpu.VMEM((tm, D), bf16), pltpu.VMEM((tm, D), f32)]
    if carry_mode:
        assert seq_len % tm == 0
        out_specs = [pl.BlockSpec((tm, D), lambda m, f, *_: (m, 0)), pl.BlockSpec((1, 8, tf), lambda m, f, *_: (m, 0, f))]
        out_shape = [jax.ShapeDtypeStruct((M, D), f32), jax.ShapeDtypeStruct((M // tm, 8, DF), f32)]
        scratch.append(pltpu.VMEM((nf, 8, tf), f32))
    else:
        assert tm % seq_len == 0
        in_specs += [pl.BlockSpec((tm, tf), lambda m, f, *_: (m, f)), pl.BlockSpec((tm, tf), lambda m, f, *_: (m, f))]
        args += list(prev)
        out_specs = [pl.BlockSpec((tm, D), lambda m, f, *_: (m, 0)), pl.BlockSpec((tm, tf), lambda m, f, *_: (m, f))]
        out_shape = [jax.ShapeDtypeStruct((M, D), f32), jax.ShapeDtypeStruct((M, DF), f32)]
    n_stream = 0
    if kstream is not None:
        assert carry_mode
        page_specs, km_spec = _kstream_specs(kstream, lambda m, f: m * nf + f)
        n_stream = len(page_specs)
        in_specs += page_specs
        out_specs.append(km_spec)
        out_shape.append(jax.ShapeDtypeStruct((M // tm * nf * kstream.bps, H_ATT, HEAD_DIM), f32))
        args = [kstream.page_table] + args + [kstream.cache_k] * n_stream
    return pl.pallas_call(
        functools.partial(_ffn_body, tm=tm, seq_len=seq_len, carry_mode=carry_mode, n_stream=n_stream),
        grid_spec=pltpu.PrefetchScalarGridSpec(
            num_scalar_prefetch=1 if n_stream else 0,
            grid=(M // tm, nf),
            in_specs=in_specs,
            out_specs=out_specs,
            scratch_shapes=scratch,
        ),
        out_shape=out_shape,
        compiler_params=_params(("arbitrary", "arbitrary"), 48 + n_stream),
        name="convffn",
    )(*args)


def kernel(x_prompt, x_sample, cache_k, cache_v, state_C, state_n, state_m, state_conv, page_table,
           norm_mix, w_in, b_if, w_out, norm_ffn, w_gate, w_up, conv_w, w_down, norm_final):
    Bsz, T, D = x_prompt.shape
    DB, Ls, _ = x_sample.shape
    depth = w_in.shape[0]
    n_pages = page_table.shape[1]
    past = n_pages * PAGE_SIZE
    assert past % MOBA_BLOCK == 0 and past // MOBA_BLOCK >= 1, "sample path assumes a block-aligned, non-empty past"
    nbp = past // MOBA_BLOCK
    n_sel_s = min(MOBA_TOP_K, nbp)
    slopes = 2.0 ** (-8.0 * jnp.arange(1, H_ATT + 1, dtype=f32) / H_ATT)
    w_in_t = jnp.swapaxes(w_in, 1, 2)
    TM_P, TM_S = 512, DB * Ls
    L_P, L_S = 256, 8
    TF = 512

    xp = x_prompt.reshape(Bsz * T, D)
    xs = x_sample.reshape(DB * Ls, D)
    st_p, st_s = [], []
    for l in range(depth):
        w_main, w_if = _cast_w_in(w_in_t, layer=l)
        bif = jnp.pad(b_if[l], (0, LANES - 2 * H_MLSTM)).reshape(1, LANES)
        wo = w_out[l].astype(bf16)
        wg, wu, wd = w_gate[l].astype(bf16), w_up[l].astype(bf16), w_down[l].astype(bf16)
        g_mix, g_ffn = norm_mix[l].reshape(1, D), norm_ffn[l].reshape(1, D)
        g_fin = norm_final.reshape(1, D) if l == depth - 1 else jnp.ones((1, D), f32)
        assert depth == 1, "final norm is fused into the last layer's FFN call"

        q, k, v, zm, gates, kb, vb, kmean = _inproj(xp, g_mix, w_main, w_if, bif, tm=TM_P, attn_aux=True)
        att = _moba_prompt(slopes, q.reshape(Bsz, T, SEG), kb.reshape(Bsz, T, SEG), vb.reshape(Bsz, T, SEG),
                           kmean.reshape(Bsz, T // MOBA_BLOCK, SEG))
        n_blk = DB * nbp
        n_ffn = min(n_blk, (Bsz * T // TM_P) * (wg.shape[1] // TF) * FFN_STREAM_BLOCKS)
        ml_steps = Bsz * T // L_P
        ks_ffn = _KStream(page_table, cache_k, l, 0, FFN_STREAM_BLOCKS, n_blk)
        ks_ml = _KStream(page_table, cache_k, l, n_ffn, max(1, -(-(n_blk - n_ffn) // ml_steps)), n_blk)
        hm, c_p, n_p, m_p, km_ml = _mlstm(
            zm.reshape(Bsz, T, 4 * SEG), gates.reshape(Bsz, T, LANES),
            jnp.zeros((Bsz, H_MLSTM, HEAD_DIM, HEAD_DIM), f32), jnp.zeros((Bsz, H_MLSTM, HEAD_DIM), f32),
            jnp.zeros((Bsz, H_MLSTM, LANES), f32), L=L_P, l_out=L_P, kstream=ks_ml)
        x1 = _outproj(xp, att.reshape(Bsz * T, SEG), hm.reshape(Bsz * T, SEG), wo, tm=TM_P, tn=D)
        xp, tail, km_ffn = _ffn(x1, g_ffn, g_fin, wg, wu, conv_w[l], wd, tm=TM_P, tf=TF, seq_len=T, kstream=ks_ffn)
        kmean_s = jnp.concatenate([km_ffn[:n_ffn], km_ml[:n_blk - n_ffn]], axis=0).reshape(DB, nbp, H_ATT, HEAD_DIM)
        conv_p = tail.reshape(Bsz, T // TM_P, 8, -1)[:, -1, 8 - (CONV_W - 1):, :]
        st_p.append((k.reshape(Bsz, T, H_ATT, HEAD_DIM), v.reshape(Bsz, T, H_ATT, HEAD_DIM), c_p, n_p, m_p[:, :, 0], conv_p))

        q, k, v, zm, gates = _inproj(xs, g_mix, w_main, w_if, bif, tm=TM_S, attn_aux=False)
        q8 = jnp.pad(q.reshape(DB, Ls, SEG), ((0, 0), (0, 8 - Ls), (0, 0)))
        q_rank = _qsel(xs, g_mix, w_in_t, layer=l)
        q_rank = jnp.pad(q_rank.reshape(DB, Ls, H_ATT, HEAD_DIM), ((0, 0), (0, 8 - Ls), (0, 0), (0, 0)))
        idx = _select_sample(kmean_s, q_rank, L=Ls, n_sel=n_sel_s)
        idx = idx[:, :Ls * n_sel_s, :, 0].reshape(DB, Ls, n_sel_s, H_ATT)
        idx_flat = jnp.transpose(idx, (0, 1, 3, 2)).reshape(-1)
        att8 = _sattn_sample(idx_flat, page_table, slopes, q8, k.reshape(DB, Ls, SEG), v.reshape(DB, Ls, SEG),
                             cache_k, cache_v, layer=l, L=Ls, n_sel=n_sel_s, past=past)
        att = att8[:, :Ls].reshape(DB * Ls, SEG)
        pad_t = ((0, 0), (0, L_S - Ls), (0, 0))
        gate_pad = jnp.where(jnp.arange(LANES) < H_MLSTM, NEG, 0.0).astype(f32)
        gates_s = jnp.concatenate(
            [gates.reshape(DB, Ls, LANES), jnp.broadcast_to(gate_pad, (DB, L_S - Ls, LANES))], axis=1)
        hm8, c_s, n_s, m_s = _mlstm(
            jnp.pad(zm.reshape(DB, Ls, 4 * SEG), pad_t), gates_s, state_C[l], state_n[l],
            jnp.broadcast_to(state_m[l][:, :, None], (DB, H_MLSTM, LANES)), L=L_S, l_out=8)
        hm = hm8[:, :Ls].reshape(DB * Ls, SEG)
        x1 = _outproj(xs, att, hm, wo, tm=TM_S, tn=D)
        buf = state_conv[l]
        zero = jnp.zeros_like(buf[:, 0])
        p1 = jnp.stack([buf[:, 1]] + [zero] * (Ls - 1), axis=1).reshape(DB * Ls, -1)
        p2 = jnp.stack([buf[:, 0], buf[:, 1]] + [zero] * (Ls - 2), axis=1).reshape(DB * Ls, -1)
        xs, a_full = _ffn(x1, g_ffn, g_fin, wg, wu, conv_w[l], wd, tm=TM_S, tf=TF, seq_len=Ls, prev=(p1, p2))
        conv_s = a_full.reshape(DB, Ls, -1)[:, Ls - (CONV_W - 1):]
        st_s.append((k.reshape(DB, Ls, H_ATT, HEAD_DIM), v.reshape(DB, Ls, H_ATT, HEAD_DIM), c_s, n_s, m_s[:, :, 0], conv_s))

    k_p, v_p, C_p, n_p, m_p, conv_p = [jnp.stack(z) for z in zip(*st_p)]
    k_s, v_s, C_s, n_s, m_s, conv_s = [jnp.stack(z) for z in zip(*st_s)]
    return (xp.reshape(Bsz, T, D), xs.reshape(DB, Ls, D), k_p, v_p, C_p, n_p, m_p, conv_p,
            k_s, v_s, C_s, n_s, m_s, conv_s)
```

```python
import functools
from typing import NamedTuple

import jax
import jax.numpy as jnp
from jax import lax
from jax.experimental import pallas as pl
from jax.experimental.pallas import tpu as pltpu

f32, bf16, i32 = jnp.float32, jnp.bfloat16, jnp.int32

HEAD_DIM = 128
H_ATT = 8
H_MLSTM = 8
SEG = H_ATT * HEAD_DIM
N_SEG = 7
CONV_W = 3
MOBA_BLOCK = 256
MOBA_TOP_K = 3
PAGE_SIZE = 128
PAGES_PER_BLOCK = MOBA_BLOCK // PAGE_SIZE
EPS = 1e-6
NEG = -1e30
LANES = 128
MAX_BLOCKS = 32
MIB = 1 << 20


def _params(sem, vmem_mib):
    return pltpu.CompilerParams(dimension_semantics=sem, vmem_limit_bytes=vmem_mib * MIB)


def _iota(shape, dim, dtype=i32):
    return lax.broadcasted_iota(dtype, shape, dim)


def _dot(a, b):
    return jnp.dot(a, b, preferred_element_type=f32)


def _dot_nt(a, b):
    return lax.dot_general(a, b, (((1,), (1,)), ((), ())), preferred_element_type=f32)


def _dot_tn(a, b):
    return lax.dot_general(a, b, (((0,), (0,)), ((), ())), preferred_element_type=f32)


def _rmsnorm(x, g):
    return x * lax.rsqrt(jnp.mean(x * x, axis=-1, keepdims=True) + EPS) * g


def _cast_w_in_body(w_ref, wg_ref, main_ref, wif_ref):
    main_ref[...] = w_ref[...].T.astype(bf16)

    @pl.when(pl.program_id(0) == 0)
    def _():
        g = wg_ref[...]
        g = jnp.concatenate([g, jnp.zeros((LANES - g.shape[0], g.shape[1]), f32)], axis=0)
        wif_ref[...] = g.T.astype(bf16)


def _cast_w_in(w_in_t, *, layer):
    D = w_in_t.shape[2]
    n_gate = 2 * H_MLSTM
    assert w_in_t.shape[1] == N_SEG * SEG + n_gate and (N_SEG * SEG) % n_gate == 0
    return pl.pallas_call(
        _cast_w_in_body,
        grid=(N_SEG,),
        in_specs=[pl.BlockSpec((None, SEG, D), lambda n: (layer, n, 0)),
                  pl.BlockSpec((None, n_gate, D), lambda n: (layer, N_SEG * SEG // n_gate, 0))],
        out_specs=[pl.BlockSpec((D, SEG), lambda n: (0, n)),
                   pl.BlockSpec((D, LANES), lambda n: (0, 0))],
        out_shape=[jax.ShapeDtypeStruct((D, N_SEG * SEG), bf16), jax.ShapeDtypeStruct((D, LANES), bf16)],
        compiler_params=_params(("arbitrary",), 48),
        name="cast_w_in",
    )(w_in_t, w_in_t)


def _inproj_body(x_ref, g_ref, w_ref, wif_ref, bif_ref, q_ref, k_ref, v_ref, zm_ref, gate_ref, *rest, attn_aux):
    if attn_aux:
        kb_ref, vb_ref, kmean_ref, h_scr = rest
    else:
        (h_scr,) = rest
    n = pl.program_id(1)
    scale = HEAD_DIM ** -0.5

    @pl.when(n == 0)
    def _():
        hb = _rmsnorm(x_ref[...], g_ref[...]).astype(bf16)
        h_scr[...] = hb
        zg = _dot(hb, wif_ref[...]) + bif_ref[...]
        lane = _iota(zg.shape, 1)
        log_sig = jnp.minimum(zg, 0.0) - jnp.log1p(jnp.exp(-jnp.abs(zg)))
        gate_ref[...] = jnp.where(lane < H_MLSTM, zg, jnp.where(lane < 2 * H_MLSTM, log_sig, 0.0))
        q_ref[...] = (_dot(hb, w_ref[...]) * scale).astype(bf16)

    def segment():
        return _dot(h_scr[...], w_ref[...])

    @pl.when(n == 1)
    def _():
        z = segment()
        k_ref[...] = z
        if attn_aux:
            kb_ref[...] = z.astype(bf16)
            for i in range(z.shape[0] // MOBA_BLOCK):
                blk = z[i * MOBA_BLOCK:(i + 1) * MOBA_BLOCK, :]
                kmean_ref[0, i:i + 1, :] = jnp.sum(blk, axis=0, keepdims=True) * (1.0 / MOBA_BLOCK)

    @pl.when(n == 2)
    def _():
        z = segment()
        v_ref[...] = z
        if attn_aux:
            vb_ref[...] = z.astype(bf16)

    @pl.when(n == 4)
    def _():
        zm_ref[...] = (segment() * scale).astype(bf16)

    @pl.when((n == 3) | (n >= 5))
    def _():
        zm_ref[...] = segment().astype(bf16)


def _inproj(x, g, w_main, w_if, b_if, *, tm, attn_aux):
    M, D = x.shape
    grid = (M // tm, N_SEG)

    def row(m, n):
        return (m, 0)

    out_specs = [
        pl.BlockSpec((tm, SEG), row),
        pl.BlockSpec((tm, SEG), row),
        pl.BlockSpec((tm, SEG), row),
        pl.BlockSpec((tm, SEG), lambda m, n: (m, jnp.maximum(n - 3, 0))),
        pl.BlockSpec((tm, LANES), row),
    ]
    out_shape = [
        jax.ShapeDtypeStruct((M, SEG), bf16),
        jax.ShapeDtypeStruct((M, SEG), f32),
        jax.ShapeDtypeStruct((M, SEG), f32),
        jax.ShapeDtypeStruct((M, 4 * SEG), bf16),
        jax.ShapeDtypeStruct((M, LANES), f32),
    ]
    if attn_aux:
        assert tm % MOBA_BLOCK == 0
        bpt = tm // MOBA_BLOCK
        out_specs += [pl.BlockSpec((tm, SEG), row), pl.BlockSpec((tm, SEG), row),
                      pl.BlockSpec((1, bpt, SEG), lambda m, n: (m, 0, 0))]
        out_shape += [jax.ShapeDtypeStruct((M, SEG), bf16), jax.ShapeDtypeStruct((M, SEG), bf16),
                      jax.ShapeDtypeStruct((M // tm, bpt, SEG), f32)]
    return pl.pallas_call(
        functools.partial(_inproj_body, attn_aux=attn_aux),
        grid=grid,
        in_specs=[
            pl.BlockSpec((tm, D), row),
            pl.BlockSpec((1, D), lambda m, n: (0, 0)),
            pl.BlockSpec((D, SEG), lambda m, n: (0, n)),
            pl.BlockSpec((D, LANES), lambda m, n: (0, 0)),
            pl.BlockSpec((1, LANES), lambda m, n: (0, 0)),
        ],
        out_specs=out_specs,
        out_shape=out_shape,
        scratch_shapes=[pltpu.VMEM((tm, D), bf16)],
        compiler_params=_params(("arbitrary", "arbitrary"), 52),
        name="inproj",
    )(x, g, w_main, w_if, b_if)


def _top_blocks_mask(scores_t, n_valid, own, n_sel):
    R = scores_t.shape[0]
    rown = _iota(scores_t.shape, 0)
    s = jnp.where(rown < n_valid, scores_t, -jnp.inf)
    mask = jnp.where(rown == own, 0.0, NEG)
    for _ in range(n_sel):
        mx = jnp.max(s, axis=0, keepdims=True)
        idx = jnp.min(jnp.where(s == mx, rown, R), axis=0, keepdims=True)
        pick = (rown == idx) & (mx > -jnp.inf)
        mask = jnp.where(pick, 0.0, mask)
        s = jnp.where(pick, -jnp.inf, s)
    return mask


ATT_HEADS = 2
ATT_CHUNK = 4


def _moba_prompt_body(slopes_ref, q_ref, kb_ref, vb_ref, kmean_ref, o_ref, kaug, vt, qaug, acc_scr, s_scr, ml_scr,
                      *, nb):
    hp = pl.program_id(1)
    qi = pl.program_id(2)
    B = MOBA_BLOCK
    CK = ATT_CHUNK * B
    heads = [slice(hh * HEAD_DIM, (hh + 1) * HEAD_DIM) for hh in range(ATT_HEADS)]

    @pl.when(qi == 0)
    def _init():
        lane = _iota((B, LANES), 1)
        s_lo = _iota((B, LANES), 0).astype(f32)
        for hh, cs in enumerate(heads):
            slope = slopes_ref[hp * ATT_HEADS + hh]
            for n in range(nb):
                c, r0 = n // ATT_CHUNK, (n % ATT_CHUNK) * B
                kaug[hh, c, r0:r0 + B, 0:HEAD_DIM] = kb_ref[0, n * B:(n + 1) * B, cs]
                ext = jnp.where(lane == n, 1.0, 0.0)
                ext = jnp.where(lane == MAX_BLOCKS, -slope, ext)
                ext = jnp.where(lane == MAX_BLOCKS + 1, slope * s_lo, ext)
                ext = jnp.where(lane == MAX_BLOCKS + 2, -slope * B, ext)
                ext = jnp.where(lane == MAX_BLOCKS + 3, slope * (B * n), ext)
                kaug[hh, c, r0:r0 + B, HEAD_DIM:2 * HEAD_DIM] = ext.astype(bf16)
                vt[hh, c, :, r0:r0 + B] = vb_ref[0, n * B:(n + 1) * B, cs].astype(f32).T.astype(bf16)

    c_own = qi // ATT_CHUNK
    n_ext = HEAD_DIM - MAX_BLOCKS
    rr = _iota((n_ext, B), 0)
    t_lo = _iota((n_ext, B), 1).astype(f32)
    q_ext = jnp.where(rr == 0, t_lo, jnp.where((rr == 1) | (rr == 3), 1.0, 0.0))
    q_ext = jnp.where(rr == 2, qi.astype(f32), q_ext).astype(bf16)
    for hh, cs in enumerate(heads):
        q_t = q_ref[0, :, cs].astype(f32).T.astype(bf16)
        scores_t = _dot(kmean_ref[0, :, cs].astype(bf16), q_t)
        mask_t = _top_blocks_mask(scores_t, qi, qi, min(MOBA_TOP_K, nb))
        if nb < MAX_BLOCKS:
            mask_t = jnp.concatenate([mask_t, jnp.full((MAX_BLOCKS - nb, B), NEG, f32)], axis=0)
        qaug[hh, 0:HEAD_DIM, :] = q_t
        qaug[hh, HEAD_DIM:HEAD_DIM + MAX_BLOCKS, :] = mask_t.astype(bf16)
        qaug[hh, HEAD_DIM + MAX_BLOCKS:, :] = q_ext
        s_scr[hh, 0] = _dot(kaug[hh, 0], qaug[hh])
        ml_scr[2 * hh:2 * hh + 1, :] = jnp.full((1, B), NEG, f32)
        ml_scr[2 * hh + 1:2 * hh + 2, :] = jnp.zeros((1, B), f32)
        acc_scr[hh] = jnp.zeros((HEAD_DIM, B), f32)

    def softmax_pv(hh, c, s):
        m = ml_scr[2 * hh:2 * hh + 1, :]
        m_new = jnp.maximum(m, jnp.max(s, axis=0, keepdims=True))
        alpha = jnp.exp(m - m_new)
        p = jnp.exp(s - m_new)
        ml_scr[2 * hh:2 * hh + 1, :] = m_new
        ml_scr[2 * hh + 1:2 * hh + 2, :] = alpha * ml_scr[2 * hh + 1:2 * hh + 2, :] + jnp.sum(p, axis=0, keepdims=True)
        acc_scr[hh] = alpha * acc_scr[hh] + _dot(vt[hh, c], p.astype(bf16))

    def stage(c, cur, nxt):
        for hh in range(ATT_HEADS):
            s_scr[hh, nxt] = _dot(kaug[hh, c + 1], qaug[hh])
        for hh in range(ATT_HEADS):
            softmax_pv(hh, c, s_scr[hh, cur])

    def body(c, carry):
        @pl.when(c % 2 == 0)
        def _():
            stage(c, 0, 1)

        @pl.when(c % 2 == 1)
        def _():
            stage(c, 1, 0)

        return carry

    lax.fori_loop(0, c_own, body, 0)
    own = qi - c_own * ATT_CHUNK
    for r in range(ATT_CHUNK):
        @pl.when(own == r)
        def _(r=r):
            nk = (r + 1) * B
            causal = (_iota((nk, B), 0) - _iota((nk, B), 1)) <= r * B
            for hh in range(ATT_HEADS):
                s = jnp.where(causal, s_scr[hh, c_own % 2, 0:nk, :], NEG)
                m = ml_scr[2 * hh:2 * hh + 1, :]
                m_new = jnp.maximum(m, jnp.max(s, axis=0, keepdims=True))
                alpha = jnp.exp(m - m_new)
                p = jnp.exp(s - m_new)
                ml_scr[2 * hh + 1:2 * hh + 2, :] = (alpha * ml_scr[2 * hh + 1:2 * hh + 2, :]
                                                    + jnp.sum(p, axis=0, keepdims=True))
                acc_scr[hh] = alpha * acc_scr[hh] + _dot(vt[hh, c_own, :, 0:nk], p.astype(bf16))

    for hh, cs in enumerate(heads):
        o_ref[0, :, cs] = (acc_scr[hh] / ml_scr[2 * hh + 1:2 * hh + 2, :]).T.astype(bf16)


def _moba_prompt(slopes, q, kb, vb, kmean):
    Bsz, T, _ = q.shape
    assert T % (MOBA_BLOCK * ATT_CHUNK) == 0 and H_ATT % ATT_HEADS == 0
    nb = T // MOBA_BLOCK
    assert nb <= MAX_BLOCKS
    blk = MOBA_BLOCK
    W = ATT_HEADS * HEAD_DIM
    nc = nb // ATT_CHUNK
    return pl.pallas_call(
        functools.partial(_moba_prompt_body, nb=nb),
        grid=(Bsz, H_ATT // ATT_HEADS, nb),
        in_specs=[
            pl.BlockSpec(memory_space=pltpu.SMEM),
            pl.BlockSpec((1, blk, W), lambda b, h, i: (b, i, h)),
            pl.BlockSpec((1, T, W), lambda b, h, i: (b, 0, h)),
            pl.BlockSpec((1, T, W), lambda b, h, i: (b, 0, h)),
            pl.BlockSpec((1, nb, W), lambda b, h, i: (b, 0, h)),
        ],
        out_specs=pl.BlockSpec((1, blk, W), lambda b, h, i: (b, i, h)),
        out_shape=jax.ShapeDtypeStruct((Bsz, T, H_ATT * HEAD_DIM), bf16),
        scratch_shapes=[
            pltpu.VMEM((ATT_HEADS, nc, ATT_CHUNK * blk, 2 * HEAD_DIM), bf16),
            pltpu.VMEM((ATT_HEADS, nc, HEAD_DIM, ATT_CHUNK * blk), bf16),
            pltpu.VMEM((ATT_HEADS, 2 * HEAD_DIM, blk), bf16),
            pltpu.VMEM((ATT_HEADS, HEAD_DIM, blk), f32),
            pltpu.VMEM((ATT_HEADS, 2, ATT_CHUNK * blk, blk), f32),
            pltpu.VMEM((2 * ATT_HEADS, blk), f32),
        ],
        compiler_params=_params(("arbitrary", "arbitrary", "arbitrary"), 52),
        name="moba_prompt",
    )(slopes, q, kb, vb, kmean)


FFN_STREAM_BLOCKS = 3


class _KStream(NamedTuple):
    page_table: jax.Array
    cache_k: jax.Array
    layer: int
    base: int
    bps: int
    total: int


def _kstream_specs(ks, step_of):
    nbp = ks.page_table.shape[1] // PAGES_PER_BLOCK

    def page_map(*ids, i):
        g = jnp.minimum(ks.base + step_of(*ids[:-1]) * ks.bps + i // PAGES_PER_BLOCK, ks.total - 1)
        return (ks.layer, ids[-1][g // nbp, (g % nbp) * PAGES_PER_BLOCK + i % PAGES_PER_BLOCK], 0, 0, 0)

    in_specs = [pl.BlockSpec((None, None, PAGE_SIZE, H_ATT, HEAD_DIM), functools.partial(page_map, i=i))
                for i in range(ks.bps * PAGES_PER_BLOCK)]
    out_spec = pl.BlockSpec((ks.bps, H_ATT, HEAD_DIM), lambda *ids: (step_of(*ids[:-1]), 0, 0))
    return in_specs, out_spec


def _kstream_reduce(pages, out_ref):
    for b in range(len(pages) // PAGES_PER_BLOCK):
        s = jnp.sum(pages[PAGES_PER_BLOCK * b][...], axis=0)
        for p in range(1, PAGES_PER_BLOCK):
            s = s + jnp.sum(pages[PAGES_PER_BLOCK * b + p][...], axis=0)
        out_ref[b] = s * (1.0 / MOBA_BLOCK)


def _qsel_body(x_ref, g_ref, w_ref, o_ref):
    h = _rmsnorm(x_ref[...], g_ref[...])
    o_ref[...] = lax.dot_general(h, w_ref[...], (((1,), (1,)), ((), ())), preferred_element_type=f32,
                                 precision=lax.Precision.HIGHEST)


def _qsel(x, g, w_in_t, *, layer, tn=256):
    M, D = x.shape
    return pl.pallas_call(
        _qsel_body,
        grid=(SEG // tn,),
        in_specs=[
            pl.BlockSpec((M, D), lambda n: (0, 0)),
            pl.BlockSpec((1, D), lambda n: (0, 0)),
            pl.BlockSpec((None, tn, D), lambda n: (layer, n, 0)),
        ],
        out_specs=pl.BlockSpec((M, tn), lambda n: (0, n)),
        out_shape=jax.ShapeDtypeStruct((M, SEG), f32),
        compiler_params=_params(("arbitrary",), 32),
        name="qsel",
    )(x, g, w_in_t)


IDX_ROWS = 16


def _select_body(km_ref, q_ref, idx_ref, *, L, n_sel):
    nbp = km_ref.shape[0]
    idx_ref[...] = jnp.zeros(idx_ref.shape, i32)
    km = km_ref[...]
    blk = _iota((nbp, H_ATT, 1), 0)
    for l in range(L):
        sc = jnp.sum(km * q_ref[l][None], axis=-1, keepdims=True)
        for r in range(n_sel):
            mx = jnp.max(sc, axis=0, keepdims=True)
            idx = jnp.min(jnp.where(sc == mx, blk, nbp), axis=0, keepdims=True)
            idx_ref[l * n_sel + r] = jnp.broadcast_to(idx[0], (H_ATT, LANES))
            sc = jnp.where(blk == idx, -jnp.inf, sc)


def _select_sample(kmean, q8, *, L, n_sel):
    DB, nbp = kmean.shape[:2]
    assert L * n_sel <= IDX_ROWS
    return pl.pallas_call(
        functools.partial(_select_body, L=L, n_sel=n_sel),
        grid=(DB,),
        in_specs=[
            pl.BlockSpec((None, nbp, H_ATT, HEAD_DIM), lambda d: (d, 0, 0, 0)),
            pl.BlockSpec((None, 8, H_ATT, HEAD_DIM), lambda d: (d, 0, 0, 0)),
        ],
        out_specs=pl.BlockSpec((None, IDX_ROWS, H_ATT, LANES), lambda d: (d, 0, 0, 0)),
        out_shape=jax.ShapeDtypeStruct((DB, IDX_ROWS, H_ATT, LANES), i32),
        compiler_params=_params(("arbitrary",), 32),
        name="select_sample",
    )(kmean, q8)


def _sattn_body(idx_ref, pt_ref, slopes_ref, q_ref, kn_ref, vn_ref, ck_hbm, cv_hbm, o_ref, kbuf, vbuf, sem,
                *, layer, L, n_sel, past):
    nblk = L * n_sel * PAGES_PER_BLOCK
    d = pl.program_id(0)
    h = pl.program_id(1)
    step = d * H_ATT + h
    slot = step % 2

    def page_copies(dd, hh, sl):
        out = []
        for i in range(nblk):
            lj, p = i // PAGES_PER_BLOCK, i % PAGES_PER_BLOCK
            blk = idx_ref[((dd * L + lj // n_sel) * H_ATT + hh) * n_sel + lj % n_sel]
            phys = pt_ref[dd, blk * PAGES_PER_BLOCK + p]
            out.append(pltpu.make_async_copy(ck_hbm.at[layer, phys, :, hh, :], kbuf.at[sl, i], sem.at[sl]))
            out.append(pltpu.make_async_copy(cv_hbm.at[layer, phys, :, hh, :], vbuf.at[sl, i], sem.at[sl]))
        return out

    @pl.when(step == 0)
    def _():
        for j, cp in enumerate(page_copies(d, h, slot)):
            cp.start(priority=j % 2)

    @pl.when(step + 1 < pl.num_programs(0) * H_ATT)
    def _():
        nxt = step + 1
        for j, cp in enumerate(page_copies(nxt // H_ATT, nxt % H_ATT, 1 - slot)):
            cp.start(priority=j % 2)

    for cp in page_copies(d, h, slot):
        cp.wait()

    kp = [kbuf.at[slot, i] for i in range(nblk)]
    vp = [vbuf.at[slot, i] for i in range(nblk)]
    slope = slopes_ref[h]
    qf = q_ref[0].astype(f32)
    kn = kn_ref[0]
    vn = vn_ref[0]
    lpos = _iota((L, 1), 0)
    slope_pos = slope * _iota((PAGE_SIZE, 1), 0).astype(f32)
    rows = []
    for l in range(L):
        ql = qf[l:l + 1, :]
        lo = jnp.sum(kn * ql, axis=1, keepdims=True)
        lo = jnp.where(lpos <= l, lo - slope * (l - lpos).astype(f32), NEG)
        logits = []
        for j in range(n_sel):
            blk = idx_ref[((d * L + l) * H_ATT + h) * n_sel + j]
            for p in range(PAGES_PER_BLOCK):
                kb = kp[(l * n_sel + j) * PAGES_PER_BLOCK + p][...]
                page_dist = (past + l - blk * MOBA_BLOCK - p * PAGE_SIZE).astype(f32)
                logits.append(jnp.sum(kb * ql, axis=1, keepdims=True) + (slope_pos - slope * page_dist))
        m = jnp.max(lo, axis=0, keepdims=True)
        for lg in logits:
            m = jnp.maximum(m, jnp.max(lg, axis=0, keepdims=True))
        p_own = jnp.exp(lo - m)
        den = jnp.sum(p_own, axis=0, keepdims=True)
        num = jnp.sum(p_own * vn, axis=0, keepdims=True)
        for i, lg in enumerate(logits):
            pi = jnp.exp(lg - m)
            den = den + jnp.sum(pi, axis=0, keepdims=True)
            num = num + jnp.sum(pi * vp[l * n_sel * PAGES_PER_BLOCK + i][...], axis=0, keepdims=True)
        rows.append(num / den)
    rows.append(jnp.zeros((8 - L, HEAD_DIM), f32))
    o_ref[0] = jnp.concatenate(rows, axis=0).astype(bf16)


def _sattn_sample(idx_flat, page_table, slopes, q8, k_new, v_new, cache_k, cache_v, *, layer, L, n_sel, past):
    DB = q8.shape[0]
    assert L <= 8
    nblk = L * n_sel * PAGES_PER_BLOCK
    return pl.pallas_call(
        functools.partial(_sattn_body, layer=layer, L=L, n_sel=n_sel, past=past),
        grid_spec=pltpu.PrefetchScalarGridSpec(
            num_scalar_prefetch=2,
            grid=(DB, H_ATT),
            in_specs=[
                pl.BlockSpec(memory_space=pltpu.SMEM),
                pl.BlockSpec((1, 8, HEAD_DIM), lambda d, h, idx, pt: (d, 0, h)),
                pl.BlockSpec((1, L, HEAD_DIM), lambda d, h, idx, pt: (d, 0, h)),
                pl.BlockSpec((1, L, HEAD_DIM), lambda d, h, idx, pt: (d, 0, h)),
                pl.BlockSpec(memory_space=pl.ANY),
                pl.BlockSpec(memory_space=pl.ANY),
            ],
            out_specs=pl.BlockSpec((1, 8, HEAD_DIM), lambda d, h, idx, pt: (d, 0, h)),
            scratch_shapes=[
                pltpu.VMEM((2, nblk, PAGE_SIZE, HEAD_DIM), f32),
                pltpu.VMEM((2, nblk, PAGE_SIZE, HEAD_DIM), f32),
                pltpu.SemaphoreType.DMA((2,)),
            ],
        ),
        out_shape=jax.ShapeDtypeStruct((DB, 8, H_ATT * HEAD_DIM), bf16),
        compiler_params=_params(("arbitrary", "arbitrary"), 32),
        name="sattn_sample",
    )(idx_flat, page_table, slopes, q8, k_new, v_new, cache_k, cache_v)


def _split3_dot(a_bf16, x):
    x1 = x.astype(bf16)
    r1 = x - x1.astype(f32)
    x2 = r1.astype(bf16)
    x3 = (r1 - x2.astype(f32)).astype(bf16)
    return _dot(a_bf16, x1) + _dot(a_bf16, x2) + _dot(a_bf16, x3)


def _mlstm_body(*refs, L, l_out, n_stream):
    if n_stream:
        refs = refs[1:]
        _kstream_reduce(refs[5:5 + n_stream], refs[5 + n_stream + 4])
        refs = refs[:5] + refs[5 + n_stream:5 + n_stream + 4] + refs[5 + n_stream + 5:]
    zm_ref, gate_ref, c0_ref, n0_ref, m0_ref, hm_ref, c_out, n_out, m_out, c_scr, n_scr, m_scr = refs
    c = pl.program_id(1)
    nc = pl.num_programs(1)

    @pl.when(c == 0)
    def _():
        c_scr[...] = c0_ref[0]
        n_scr[...] = n0_ref[0]
        m_scr[...] = m0_ref[0]

    G = gate_ref[0]
    row = _iota((L, L), 0)
    col = _iota((L, L), 1)
    causal = row >= col
    csum = _split3_dot(jnp.where(causal, 1.0, 0.0).astype(bf16), G)
    lane = _iota((L, LANES), 1)
    X = jnp.where(lane < H_MLSTM, pltpu.roll(csum, LANES - H_MLSTM, 1), pltpu.roll(G, H_MLSTM, 1) - csum)
    XT = X.T
    x1 = X.astype(bf16)
    r1 = X - x1.astype(f32)
    x2 = r1.astype(bf16)
    x3 = (r1 - x2.astype(f32)).astype(bf16)
    sel_row = _iota((LANES, LANES), 0)

    def replicated(src):
        e = jnp.where(sel_row == src, 1.0, 0.0).astype(bf16)
        return _dot(x1, e) + _dot(x2, e) + _dot(x3, e)

    def over_keys(col):
        return col[:, :L] if L <= LANES else jnp.concatenate([col] * (L // LANES), axis=1)

    ones_v = jnp.ones((L, HEAD_DIM), bf16)
    for h in range(H_MLSTM):
        sl = slice(h * HEAD_DIM, (h + 1) * HEAD_DIM)
        q = zm_ref[0, :, sl]
        k = zm_ref[0, :, SEG + h * HEAD_DIM:SEG + (h + 1) * HEAD_DIM]
        v = zm_ref[0, :, 2 * SEG + h * HEAD_DIM:2 * SEG + (h + 1) * HEAD_DIM]
        o = zm_ref[0, :, 3 * SEG + h * HEAD_DIM:3 * SEG + (h + 1) * HEAD_DIM]
        b_rep = replicated(h)
        g_rep = replicated(H_MLSTM + h)
        g_row = XT[H_MLSTM + h:H_MLSTM + h + 1, :]
        m_prev = m_scr[h:h + 1, :]
        g_max = jnp.max(jnp.where(causal, g_row, -jnp.inf), axis=1, keepdims=True)
        m_t = b_rep + jnp.maximum(m_prev, g_max)
        decay_d = jnp.exp(jnp.where(causal, g_row + over_keys(b_rep - m_t), -jnp.inf))
        s = _dot_nt(q, k) * decay_d
        a_inter = jnp.exp(b_rep + m_prev - m_t)
        c_h = c_scr[h]
        n_h = n_scr[h:h + 1, :]
        sv = _dot(s.astype(bf16), jnp.concatenate([v, ones_v], axis=1))
        n_rows = jnp.broadcast_to(n_h, (HEAD_DIM, HEAD_DIM)).astype(bf16)
        cq = _dot_nt(q, jnp.concatenate([c_h.astype(bf16), n_rows], axis=0))
        num = sv[:, :HEAD_DIM] + a_inter * cq[:, :HEAD_DIM]
        den = sv[:, HEAD_DIM:] + a_inter * cq[:, HEAD_DIM:]
        hh = num / jnp.maximum(jnp.abs(den), jnp.exp(-m_t))
        out = (jax.nn.sigmoid(o.astype(f32)) * hh).astype(bf16)
        hm_ref[0, :, sl] = out[:l_out]
        m_new = m_t[L - 1:L, :]
        b_last = b_rep[L - 1:L, :]
        decay = jnp.exp(b_last + m_prev - m_new)
        wk = jnp.exp(b_last + g_rep - m_new) * k.astype(f32)
        c_scr[h] = decay * c_h + _dot_tn(v, wk.astype(bf16))
        n_scr[h:h + 1, :] = decay * n_h + jnp.sum(wk, axis=0, keepdims=True)
        m_scr[h:h + 1, :] = m_new

    @pl.when(c == nc - 1)
    def _():
        c_out[0] = c_scr[...]
        n_out[0] = n_scr[...]
        m_out[0] = m_scr[...]


def _mlstm(zm, gates, c0, n0, m0, *, L, l_out, kstream=None):
    Bsz, T, _ = zm.shape
    nc = T // L
    in_specs = [
        pl.BlockSpec((1, L, 4 * SEG), lambda b, c, *_: (b, c, 0)),
        pl.BlockSpec((1, L, LANES), lambda b, c, *_: (b, c, 0)),
        pl.BlockSpec((1, H_MLSTM, HEAD_DIM, HEAD_DIM), lambda b, c, *_: (b, 0, 0, 0)),
        pl.BlockSpec((1, H_MLSTM, HEAD_DIM), lambda b, c, *_: (b, 0, 0)),
        pl.BlockSpec((1, H_MLSTM, LANES), lambda b, c, *_: (b, 0, 0)),
    ]
    out_specs = [
        pl.BlockSpec((1, l_out, SEG), lambda b, c, *_: (b, c, 0)),
        pl.BlockSpec((1, H_MLSTM, HEAD_DIM, HEAD_DIM), lambda b, c, *_: (b, 0, 0, 0)),
        pl.BlockSpec((1, H_MLSTM, HEAD_DIM), lambda b, c, *_: (b, 0, 0)),
        pl.BlockSpec((1, H_MLSTM, LANES), lambda b, c, *_: (b, 0, 0)),
    ]
    out_shape = [
        jax.ShapeDtypeStruct((Bsz, nc * l_out, SEG), bf16),
        jax.ShapeDtypeStruct((Bsz, H_MLSTM, HEAD_DIM, HEAD_DIM), f32),
        jax.ShapeDtypeStruct((Bsz, H_MLSTM, HEAD_DIM), f32),
        jax.ShapeDtypeStruct((Bsz, H_MLSTM, LANES), f32),
    ]
    scratch = [
        pltpu.VMEM((H_MLSTM, HEAD_DIM, HEAD_DIM), f32),
        pltpu.VMEM((H_MLSTM, HEAD_DIM), f32),
        pltpu.VMEM((H_MLSTM, LANES), f32),
    ]
    args = [zm, gates, c0, n0, m0]
    n_stream = 0
    if kstream is not None:
        page_specs, km_spec = _kstream_specs(kstream, lambda b, c: b * nc + c)
        n_stream = len(page_specs)
        in_specs += page_specs
        out_specs.append(km_spec)
        out_shape.append(jax.ShapeDtypeStruct((Bsz * nc * kstream.bps, H_ATT, HEAD_DIM), f32))
        args = [kstream.page_table] + args + [kstream.cache_k] * n_stream
    return pl.pallas_call(
        functools.partial(_mlstm_body, L=L, l_out=l_out, n_stream=n_stream),
        grid_spec=pltpu.PrefetchScalarGridSpec(
            num_scalar_prefetch=1 if n_stream else 0,
            grid=(Bsz, nc),
            in_specs=in_specs,
            out_specs=out_specs,
            scratch_shapes=scratch,
        ),
        out_shape=out_shape,
        compiler_params=_params(("arbitrary", "arbitrary"), 16 + n_stream),
        name="mlstm",
    )(*args)


def _outproj_body(x_ref, att_ref, hm_ref, wa_ref, wm_ref, o_ref):
    o_ref[...] = x_ref[...] + _dot(att_ref[...], wa_ref[...]) + _dot(hm_ref[...], wm_ref[...])


def _outproj(x, att, hm, w_out, *, tm, tn):
    M, D = x.shape
    return pl.pallas_call(
        _outproj_body,
        grid=(M // tm, D // tn),
        in_specs=[
            pl.BlockSpec((tm, tn), lambda m, n: (m, n)),
            pl.BlockSpec((tm, SEG), lambda m, n: (m, 0)),
            pl.BlockSpec((tm, SEG), lambda m, n: (m, 0)),
            pl.BlockSpec((SEG, tn), lambda m, n: (0, n)),
            pl.BlockSpec((SEG, tn), lambda m, n: (1, n)),
        ],
        out_specs=pl.BlockSpec((tm, tn), lambda m, n: (m, n)),
        out_shape=jax.ShapeDtypeStruct((M, D), f32),
        compiler_params=_params(("arbitrary", "arbitrary"), 40),
        name="outproj",
    )(x, att, hm, w_out, w_out)


def _ffn_body(*refs, tm, seq_len, carry_mode, n_stream):
    if n_stream:
        refs = refs[1:]
        _kstream_reduce(refs[7:7 + n_stream], refs[7 + n_stream + 2])
        refs = refs[:7] + refs[7 + n_stream:7 + n_stream + 2] + refs[7 + n_stream + 3:]
    if carry_mode:
        x_ref, g_ref, gf_ref, wg_ref, wu_ref, cw_ref, wd_ref, y_ref, tail_ref, h_scr, acc_scr, carry = refs
    else:
        x_ref, g_ref, gf_ref, wg_ref, wu_ref, cw_ref, wd_ref, p1_ref, p2_ref, y_ref, a_ref, h_scr, acc_scr = refs
    m = pl.program_id(0)
    f = pl.program_id(1)
    nf = pl.num_programs(1)

    @pl.when(f == 0)
    def _():
        h_scr[...] = _rmsnorm(x_ref[...], g_ref[...]).astype(bf16)
        acc_scr[...] = jnp.zeros(acc_scr.shape, f32)

    hb = h_scr[...]
    a = _dot(hb, wg_ref[...])
    u = _dot(hb, wu_ref[...])
    row = _iota(a.shape, 0)
    a1 = pltpu.roll(a, 1, 0)
    a2 = pltpu.roll(a, 2, 0)
    if carry_mode:
        prev = carry[f]
        prev = jnp.where(m % (seq_len // tm) == 0, 0.0, prev)
        a1 = jnp.where(row == 0, prev[7:8, :], a1)
        a2 = jnp.where(row == 0, prev[6:7, :], jnp.where(row == 1, prev[7:8, :], a2))
        carry[f] = a[tm - 8:, :]
        tail_ref[0] = a[tm - 8:, :]
    else:
        lpos = row % seq_len
        a1 = jnp.where(lpos >= 1, a1, p1_ref[...])
        a2 = jnp.where(lpos >= 2, a2, p2_ref[...])
        a_ref[...] = a
    cw = cw_ref[...]
    cv = cw[0:1, :] * a2 + cw[1:2, :] * a1 + cw[2:3, :] * a
    gl = 0.5 * cv * (1.0 + lax.erf(cv * (0.5 ** 0.5)))
    acc_scr[...] += _dot((gl * u).astype(bf16), wd_ref[...])

    @pl.when(f == nf - 1)
    def _():
        y_ref[...] = _rmsnorm(x_ref[...] + acc_scr[...], gf_ref[...])


def _ffn(x1, g_ffn, g_final, w_gate, w_up, conv_w, w_down, *, tm, tf, seq_len, prev=None, kstream=None):
    M, D = x1.shape
    DF = w_gate.shape[1]
    nf = DF // tf
    carry_mode = prev is None
    in_specs = [
        pl.BlockSpec((tm, D), lambda m, f, *_: (m, 0)),
        pl.BlockSpec((1, D), lambda m, f, *_: (0, 0)),
        pl.BlockSpec((1, D), lambda m, f, *_: (0, 0)),
        pl.BlockSpec((D, tf), lambda m, f, *_: (0, f)),
        pl.BlockSpec((D, tf), lambda m, f, *_: (0, f)),
        pl.BlockSpec((CONV_W, tf), lambda m, f, *_: (0, f)),
        pl.BlockSpec((tf, D), lambda m, f, *_: (f, 0)),
    ]
    args = [x1, g_ffn, g_final, w_gate, w_up, conv_w, w_down]
    scratch = [pltpu.VMEM((tm, D), bf16), pltpu.VMEM((tm, D), f32)]
    if carry_mode:
        assert seq_len % tm == 0
        out_specs = [pl.BlockSpec((tm, D), lambda m, f, *_: (m, 0)), pl.BlockSpec((1, 8, tf), lambda m, f, *_: (m, 0, f))]
        out_shape = [jax.ShapeDtypeStruct((M, D), f32), jax.ShapeDtypeStruct((M // tm, 8, DF), f32)]
        scratch.append(pltpu.VMEM((nf, 8, tf), f32))
    else:
        assert tm % seq_len == 0
        in_specs += [pl.BlockSpec((tm, tf), lambda m, f, *_: (m, f)), pl.BlockSpec((tm, tf), lambda m, f, *_: (m, f))]
        args += list(prev)
        out_specs = [pl.BlockSpec((tm, D), lambda m, f, *_: (m, 0)), pl.BlockSpec((tm, tf), lambda m, f, *_: (m, f))]
        out_shape = [jax.ShapeDtypeStruct((M, D), f32), jax.ShapeDtypeStruct((M, DF), f32)]
    n_stream = 0
    if kstream is not None:
        assert carry_mode
        page_specs, km_spec = _kstream_specs(kstream, lambda m, f: m * nf + f)
        n_stream = len(page_specs)
        in_specs += page_specs
        out_specs.append(km_spec)
        out_shape.append(jax.ShapeDtypeStruct((M // tm * nf * kstream.bps, H_ATT, HEAD_DIM), f32))
        args = [kstream.page_table] + args + [kstream.cache_k] * n_stream
    return pl.pallas_call(
        functools.partial(_ffn_body, tm=tm, seq_len=seq_len, carry_mode=carry_mode, n_stream=n_stream),
        grid_spec=pltpu.PrefetchScalarGridSpec(
            num_scalar_prefetch=1 if n_stream else 0,
            grid=(M // tm, nf),
            in_specs=in_specs,
            out_specs=out_specs,
            scratch_shapes=scratch,
        ),
        out_shape=out_shape,
        compiler_params=_params(("arbitrary", "arbitrary"), 48 + n_stream),
        name="convffn",
    )(*args)


def kernel(x_prompt, x_sample, cache_k, cache_v, state_C, state_n, state_m, state_conv, page_table,
           norm_mix, w_in, b_if, w_out, norm_ffn, w_gate, w_up, conv_w, w_down, norm_final):
    Bsz, T, D = x_prompt.shape
    DB, Ls, _ = x_sample.shape
    depth = w_in.shape[0]
    n_pages = page_table.shape[1]
    past = n_pages * PAGE_SIZE
    assert past % MOBA_BLOCK == 0 and past // MOBA_BLOCK >= 1, "sample path assumes a block-aligned, non-empty past"
    nbp = past // MOBA_BLOCK
    n_sel_s = min(MOBA_TOP_K, nbp)
    slopes = 2.0 ** (-8.0 * jnp.arange(1, H_ATT + 1, dtype=f32) / H_ATT)
    w_in_t = jnp.swapaxes(w_in, 1, 2)
    TM_P, TM_S = 512, DB * Ls
    L_P, L_S = 256, 8
    TF = 512

    xp = x_prompt.reshape(Bsz * T, D)
    xs = x_sample.reshape(DB * Ls, D)
    st_p, st_s = [], []
    for l in range(depth):
        w_main, w_if = _cast_w_in(w_in_t, layer=l)
        bif = jnp.pad(b_if[l], (0, LANES - 2 * H_MLSTM)).reshape(1, LANES)
        wo = w_out[l].astype(bf16)
        wg, wu, wd = w_gate[l].astype(bf16), w_up[l].astype(bf16), w_down[l].astype(bf16)
        g_mix, g_ffn = norm_mix[l].reshape(1, D), norm_ffn[l].reshape(1, D)
        g_fin = norm_final.reshape(1, D) if l == depth - 1 else jnp.ones((1, D), f32)
        assert depth == 1, "final norm is fused into the last layer's FFN call"

        q, k, v, zm, gates, kb, vb, kmean = _inproj(xp, g_mix, w_main, w_if, bif, tm=TM_P, attn_aux=True)
        att = _moba_prompt(slopes, q.reshape(Bsz, T, SEG), kb.reshape(Bsz, T, SEG), vb.reshape(Bsz, T, SEG),
                           kmean.reshape(Bsz, T // MOBA_BLOCK, SEG))
        n_blk = DB * nbp
        n_ffn = min(n_blk, (Bsz * T // TM_P) * (wg.shape[1] // TF) * FFN_STREAM_BLOCKS)
        ml_steps = Bsz * T // L_P
        ks_ffn = _KStream(page_table, cache_k, l, 0, FFN_STREAM_BLOCKS, n_blk)
        ks_ml = _KStream(page_table, cache_k, l, n_ffn, max(1, -(-(n_blk - n_ffn) // ml_steps)), n_blk)
        hm, c_p, n_p, m_p, km_ml = _mlstm(
            zm.reshape(Bsz, T, 4 * SEG), gates.reshape(Bsz, T, LANES),
            jnp.zeros((Bsz, H_MLSTM, HEAD_DIM, HEAD_DIM), f32), jnp.zeros((Bsz, H_MLSTM, HEAD_DIM), f32),
            jnp.zeros((Bsz, H_MLSTM, LANES), f32), L=L_P, l_out=L_P, kstream=ks_ml)
        x1 = _outproj(xp, att.reshape(Bsz * T, SEG), hm.reshape(Bsz * T, SEG), wo, tm=TM_P, tn=D)
        xp, tail, km_ffn = _ffn(x1, g_ffn, g_fin, wg, wu, conv_w[l], wd, tm=TM_P, tf=TF, seq_len=T, kstream=ks_ffn)
        kmean_s = jnp.concatenate([km_ffn[:n_ffn], km_ml[:n_blk - n_ffn]], axis=0).reshape(DB, nbp, H_ATT, HEAD_DIM)
        conv_p = tail.reshape(Bsz, T // TM_P, 8, -1)[:, -1, 8 - (CONV_W - 1):, :]
        st_p.append((k.reshape(Bsz, T, H_ATT, HEAD_DIM), v.reshape(Bsz, T, H_ATT, HEAD_DIM), c_p, n_p, m_p[:, :, 0], conv_p))

        q, k, v, zm, gates = _inproj(xs, g_mix, w_main, w_if, bif, tm=TM_S, attn_aux=False)
        q8 = jnp.pad(q.reshape(DB, Ls, SEG), ((0, 0), (0, 8 - Ls), (0, 0)))
        q_rank = _qsel(xs, g_mix, w_in_t, layer=l)
        q_rank = jnp.pad(q_rank.reshape(DB, Ls, H_ATT, HEAD_DIM), ((0, 0), (0, 8 - Ls), (0, 0), (0, 0)))
        idx = _select_sample(kmean_s, q_rank, L=Ls, n_sel=n_sel_s)
        idx = idx[:, :Ls * n_sel_s, :, 0].reshape(DB, Ls, n_sel_s, H_ATT)
        idx_flat = jnp.transpose(idx, (0, 1, 3, 2)).reshape(-1)
        att8 = _sattn_sample(idx_flat, page_table, slopes, q8, k.reshape(DB, Ls, SEG), v.reshape(DB, Ls, SEG),
                             cache_k, cache_v, layer=l, L=Ls, n_sel=n_sel_s, past=past)
        att = att8[:, :Ls].reshape(DB * Ls, SEG)
        pad_t = ((0, 0), (0, L_S - Ls), (0, 0))
        gate_pad = jnp.where(jnp.arange(LANES) < H_MLSTM, NEG, 0.0).astype(f32)
        gates_s = jnp.concatenate(
            [gates.reshape(DB, Ls, LANES), jnp.broadcast_to(gate_pad, (DB, L_S - Ls, LANES))], axis=1)
        hm8, c_s, n_s, m_s = _mlstm(
            jnp.pad(zm.reshape(DB, Ls, 4 * SEG), pad_t), gates_s, state_C[l], state_n[l],
            jnp.broadcast_to(state_m[l][:, :, None], (DB, H_MLSTM, LANES)), L=L_S, l_out=8)
        hm = hm8[:, :Ls].reshape(DB * Ls, SEG)
        x1 = _outproj(xs, att, hm, wo, tm=TM_S, tn=D)
        buf = state_conv[l]
        zero = jnp.zeros_like(buf[:, 0])
        p1 = jnp.stack([buf[:, 1]] + [zero] * (Ls - 1), axis=1).reshape(DB * Ls, -1)
        p2 = jnp.stack([buf[:, 0], buf[:, 1]] + [zero] * (Ls - 2), axis=1).reshape(DB * Ls, -1)
        xs, a_full = _ffn(x1, g_ffn, g_fin, wg, wu, conv_w[l], wd, tm=TM_S, tf=TF, seq_len=Ls, prev=(p1, p2))
        conv_s = a_full.reshape(DB, Ls, -1)[:, Ls - (CONV_W - 1):]
        st_s.append((k.reshape(DB, Ls, H_ATT, HEAD_DIM), v.reshape(DB, Ls, H_ATT, HEAD_DIM), c_s, n_s, m_s[:, :, 0], conv_s))

    k_p, v_p, C_p, n_p, m_p, conv_p = [jnp.stack(z) for z in zip(*st_p)]
    k_s, v_s, C_s, n_s, m_s, conv_s = [jnp.stack(z) for z in zip(*st_s)]
    return (xp.reshape(Bsz, T, D), xs.reshape(DB, Ls, D), k_p, v_p, C_p, n_p, m_p, conv_p,
            k_s, v_s, C_s, n_s, m_s, conv_s)
```
